```python
import math
import jax
import jax.numpy as jnp
from jax import lax
import numpy as np

D_MODEL = 1024
BATCH = 2
SEQ = 8192
DEPTH = 1

SSD_EXPAND = 2
SSD_D_INNER = SSD_EXPAND * D_MODEL
SSD_HEAD_DIM = 64
SSD_HEADS = SSD_D_INNER // SSD_HEAD_DIM
SSD_GROUPS = 8
SSD_HEADS_PER_GROUP = SSD_HEADS // SSD_GROUPS
SSD_STATE = 128
SSD_CONV = 5
SSD_CHUNK = 128
SSD_CONV_DIM = SSD_D_INNER + 2 * SSD_GROUPS * SSD_STATE
SSD_NORM_EPS = 1e-5

POOL_WIDTH = D_MODEL
POOL_WINDOWS = (2, 4, 8, 16)
POOL_GROUPS = len(POOL_WINDOWS)
POOL_GROUP_DIM = POOL_WIDTH // POOL_GROUPS

N_BRANCHES = 2
IN_PROJ_SIZES = (SSD_D_INNER, SSD_CONV_DIM, SSD_HEADS, SSD_HEADS, POOL_WIDTH, N_BRANCHES * D_MODEL)
IN_PROJ_DIM = sum(IN_PROJ_SIZES)

MOE_GROUPS = 4
MOE_EXPERTS_PER_GROUP = 8
MOE_EXPERTS = MOE_GROUPS * MOE_EXPERTS_PER_GROUP
MOE_TOP_K = 2
MOE_HIDDEN = 512

DEEPNORM_ALPHA = (2.0 * DEPTH) ** 0.25
DEEPNORM_BETA = (8.0 * DEPTH) ** -0.25
LN_EPS = 1e-5
N_ADA = 6

kernel_name = 'hybrid_bissd_multipool_hmoe_block'


def _layernorm(x):
    xf = x.astype(jnp.float32)
    mu = jnp.mean(xf, axis=-1, keepdims=True)
    var = jnp.mean(jnp.square(xf - mu), axis=-1, keepdims=True)
    return ((xf - mu) * lax.rsqrt(var + LN_EPS)).astype(x.dtype)


def _modulate(x, shift, scale):
    return _layernorm(x) * (1.0 + scale[:, None, :]) + shift[:, None, :]


def _post_ln(v, g, b):
    return _layernorm(v) * g + b


def _split_cols(p, sizes):
    idx = np.cumsum(np.array(sizes))[:-1].tolist()
    return jnp.split(p, idx, axis=-1)


def _dwconv_centred(u, w, b):
    k = w.shape[0]
    pad = k // 2
    out = lax.conv_general_dilated(u, w[:, None, :], window_strides=(1,), padding=[(pad, pad)],
                                   dimension_numbers=('NWC', 'WIO', 'NWC'),
                                   feature_group_count=u.shape[-1])
    return out + b


def _segsum(a):
    t = a.shape[-1]
    cs = jnp.cumsum(a, axis=-1)
    seg = cs[..., :, None] - cs[..., None, :]
    mask = jnp.tril(jnp.ones((t, t), dtype=bool))
    return jnp.where(mask, seg, -jnp.inf)


def _ssd_scan(xh, dt_raw, bm, cm, a_log, dt_bias):
    bsz, seq, g, r, p = xh.shape
    n = bm.shape[-1]
    q = SSD_CHUNK
    nc = seq // q
    dt = jax.nn.softplus(dt_raw.astype(jnp.float32) + dt_bias.astype(jnp.float32)).reshape(bsz, seq, g, r)
    a = -jnp.exp(a_log.astype(jnp.float32)).reshape(g, r)
    da = (dt * a).reshape(bsz, nc, q, g, r)
    xc = (xh * dt[..., None].astype(xh.dtype)).reshape(bsz, nc, q, g, r, p)
    bc = bm.reshape(bsz, nc, q, g, n)
    cc = cm.reshape(bsz, nc, q, g, n)
    a_cs = jnp.cumsum(da, axis=2)
    seg_l = jnp.exp(_segsum(jnp.transpose(da, (0, 3, 4, 1, 2)))).astype(xh.dtype)
    cb = jnp.einsum('bclgn,bcsgn->bgcls', cc, bc)
    scores = cb[:, :, None] * seg_l
    y_diag = jnp.einsum('bgrcls,bcsgrp->bclgrp', scores, xc)
    decay_states = jnp.exp(a_cs[:, :, -1:] - a_cs).astype(xh.dtype)
    states = jnp.einsum('bclgn,bclgrp->bcgrpn', bc, xc * decay_states[..., None])
    chunk_decay = jnp.exp(a_cs[:, :, -1]).astype(xh.dtype)

    def step(h, inp):
        s_c, d_c = inp
        return h * d_c[..., None, None] + s_c, h

    h0 = jnp.zeros((bsz, g, r, p, n), xh.dtype)
    _, h_in = lax.scan(step, h0, (jnp.moveaxis(states, 1, 0), jnp.moveaxis(chunk_decay, 1, 0)))
    h_in = jnp.moveaxis(h_in, 0, 1)
    decay_out = jnp.exp(a_cs).astype(xh.dtype)
    y_off = jnp.einsum('bclgn,bcgrpn->bclgrp', cc, h_in) * decay_out[..., None]
    return (y_diag + y_off).reshape(bsz, seq, g, r, p)


def _ssd_branch(z, xbc, dt_f, dt_b, conv_w, conv_b, a_log_f, a_log_b, dt_bias_f, dt_bias_b, d_skip, norm_g):
    bsz, seq, _ = z.shape
    xbc = jax.nn.silu(_dwconv_centred(xbc, conv_w, conv_b))
    xs, bm, cm = jnp.split(xbc, [SSD_D_INNER, SSD_D_INNER + SSD_GROUPS * SSD_STATE], axis=-1)
    xh = xs.reshape(bsz, seq, SSD_GROUPS, SSD_HEADS_PER_GROUP, SSD_HEAD_DIM)
    bm = bm.reshape(bsz, seq, SSD_GROUPS, SSD_STATE)
    cm = cm.reshape(bsz, seq, SSD_GROUPS, SSD_STATE)
    flip = lambda t: jnp.flip(t, axis=1)
    y_fwd = _ssd_scan(xh, dt_f, bm, cm, a_log_f, dt_bias_f)
    y_bwd = flip(_ssd_scan(flip(xh), flip(dt_b), flip(bm), flip(cm), a_log_b, dt_bias_b))
    y = y_fwd + y_bwd + xh * d_skip.reshape(SSD_GROUPS, SSD_HEADS_PER_GROUP, 1)
    yg = (y.reshape(bsz, seq, SSD_D_INNER) * jax.nn.silu(z)).reshape(bsz, seq, SSD_GROUPS, SSD_D_INNER // SSD_GROUPS)
    yf = yg.astype(jnp.float32)
    yn = yf * lax.rsqrt(jnp.mean(jnp.square(yf), axis=-1, keepdims=True) + SSD_NORM_EPS)
    return yn.reshape(bsz, seq, SSD_D_INNER).astype(z.dtype) * norm_g


def _pool_branch(u, pool_w, pool_scale):
    bsz, seq, _ = u.shape
    ug = u.reshape(bsz, seq, POOL_GROUPS, POOL_GROUP_DIM)
    cs = jnp.pad(jnp.cumsum(ug.astype(jnp.float32), axis=1), ((0, 0), (1, 0), (0, 0), (0, 0)))
    t = jnp.arange(seq)
    outs = []
    for gi, w in enumerate(POOL_WINDOWS):
        lo = jnp.clip(t - w // 2, 0, seq)
        hi = jnp.clip(t + w // 2, 0, seq)
        cg = cs[:, :, gi]
        cnt = (hi - lo).astype(jnp.float32)[None, :, None]
        outs.append((cg[:, hi] - cg[:, lo]) / cnt)
    pooled = jnp.stack(outs, axis=2).astype(u.dtype)
    mixed = jnp.einsum('bsgc,gcd->bsgd', pooled - ug, pool_w)
    return mixed.reshape(bsz, seq, POOL_WIDTH) * pool_scale


def _hier_moe(h, wg, bg, we, be, w_gate, w_up, w_down):
    bsz, seq, d = h.shape
    tok = h.reshape(bsz * seq, d)
    g_prob = jax.nn.softmax((tok @ wg + bg).astype(jnp.float32), axis=-1)
    g_w, g_idx = lax.top_k(g_prob, 1)
    e_logits = (tok @ we + be).astype(jnp.float32).reshape(-1, MOE_GROUPS, MOE_EXPERTS_PER_GROUP)
    e_in = jnp.take_along_axis(e_logits, g_idx[:, :, None], axis=1)[:, 0]
    e_w, e_idx = lax.top_k(jax.nn.softmax(e_in, axis=-1), MOE_TOP_K)
    e_w = e_w / jnp.sum(e_w, axis=-1, keepdims=True)
    weights = g_w * e_w
    expert_ids = g_idx * MOE_EXPERTS_PER_GROUP + e_idx
    combine = jnp.sum(jax.nn.one_hot(expert_ids, MOE_EXPERTS, dtype=jnp.float32) * weights[..., None],
                      axis=1).astype(tok.dtype)
    out = jnp.zeros_like(tok)
    for e in range(MOE_EXPERTS):
        he = jax.nn.silu(tok @ w_gate[e]) * (tok @ w_up[e])
        out = out + combine[:, e:e + 1] * (he @ w_down[e])
    return out.reshape(bsz, seq, d)


def _dt_bias_init(k, shape):
    dt = jnp.exp(jax.random.uniform(k, shape, jnp.float32, minval=math.log(1e-3), maxval=math.log(1e-1)))
    return dt + jnp.log(-jnp.expm1(-dt))


def setup_inputs(seed: int = 0) -> dict:
    key = jax.random.key(seed)
    ks = jax.random.split(key, 30)
    L = DEPTH

    def nrm(k, shape, scale):
        return jax.random.normal(k, shape, jnp.float32) * scale

    return {
        'x': nrm(ks[0], (BATCH, SEQ, D_MODEL), 1.0),
        'c': nrm(ks[1], (BATCH, D_MODEL), 1.0),
        'w_ada': nrm(ks[2], (L, D_MODEL, N_ADA * D_MODEL), 0.5 * D_MODEL ** -0.5),
        'b_ada': nrm(ks[3], (L, N_ADA * D_MODEL), 0.01),
        'w_in': nrm(ks[4], (L, D_MODEL, IN_PROJ_DIM), D_MODEL ** -0.5),
        'conv_w': nrm(ks[5], (L, SSD_CONV, SSD_CONV_DIM), SSD_CONV ** -0.5),
        'conv_b': nrm(ks[6], (L, SSD_CONV_DIM), 0.01),
        'a_log_f': jnp.log(jax.random.uniform(ks[7], (L, SSD_HEADS), jnp.float32, minval=1.0, maxval=16.0)),
        'a_log_b': jnp.log(jax.random.uniform(ks[8], (L, SSD_HEADS), jnp.float32, minval=1.0, maxval=16.0)),
        'dt_bias_f': _dt_bias_init(ks[9], (L, SSD_HEADS)),
        'dt_bias_b': _dt_bias_init(ks[10], (L, SSD_HEADS)),
        'd_skip': 1.0 + nrm(ks[11], (L, SSD_HEADS), 0.1),
        'ssd_norm_g': 1.0 + nrm(ks[12], (L, SSD_D_INNER), 0.1),
        'w_ssd_out': nrm(ks[13], (L, SSD_D_INNER, D_MODEL), SSD_D_INNER ** -0.5),
        'pool_w': nrm(ks[14], (L, POOL_GROUPS, POOL_GROUP_DIM, POOL_GROUP_DIM), POOL_GROUP_DIM ** -0.5),
        'pool_scale': 1.0 + nrm(ks[15], (L, POOL_WIDTH), 0.1),
        'w_pool_out': nrm(ks[16], (L, POOL_WIDTH, D_MODEL), POOL_WIDTH ** -0.5),
        'gate_b': nrm(ks[17], (L, N_BRANCHES * D_MODEL), 0.01),
        'w_o': nrm(ks[18], (L, D_MODEL, D_MODEL), DEEPNORM_BETA * D_MODEL ** -0.5),
        'ln1_g': 1.0 + nrm(ks[19], (L, D_MODEL), 0.1),
        'ln1_b': nrm(ks[20], (L, D_MODEL), 0.01),
        'router_wg': nrm(ks[21], (L, D_MODEL, MOE_GROUPS), D_MODEL ** -0.5),
        'router_bg': nrm(ks[22], (L, MOE_GROUPS), 0.01),
        'router_we': nrm(ks[23], (L, D_MODEL, MOE_EXPERTS), D_MODEL ** -0.5),
        'router_be': nrm(ks[24], (L, MOE_EXPERTS), 0.01),
        'exp_w_gate': nrm(ks[25], (L, MOE_EXPERTS, D_MODEL, MOE_HIDDEN), D_MODEL ** -0.5),
        'exp_w_up': nrm(ks[26], (L, MOE_EXPERTS, D_MODEL, MOE_HIDDEN), D_MODEL ** -0.5),
        'exp_w_down': nrm(ks[27], (L, MOE_EXPERTS, MOE_HIDDEN, D_MODEL), DEEPNORM_BETA * MOE_HIDDEN ** -0.5),
        'ln2_g': 1.0 + nrm(ks[28], (L, D_MODEL), 0.1),
        'ln2_b': nrm(ks[29], (L, D_MODEL), 0.01),
    }


def reference(x, c, w_ada, b_ada, w_in, conv_w, conv_b, a_log_f, a_log_b, dt_bias_f, dt_bias_b, d_skip,
              ssd_norm_g, w_ssd_out, pool_w, pool_scale, w_pool_out, gate_b, w_o, ln1_g, ln1_b,
              router_wg, router_bg, router_we, router_be, exp_w_gate, exp_w_up, exp_w_down, ln2_g, ln2_b):
    for l in range(DEPTH):
        mod = jax.nn.silu(c) @ w_ada[l] + b_ada[l]
        sh1, sc1, g1, sh2, sc2, g2 = jnp.split(mod, N_ADA, axis=-1)
        h = _modulate(x, sh1, sc1)
        z, xbc, dt_f, dt_b, u_pool, gate_logits = _split_cols(h @ w_in[l], IN_PROJ_SIZES)
        y_ssd = _ssd_branch(z, xbc, dt_f, dt_b, conv_w[l], conv_b[l], a_log_f[l], a_log_b[l],
                            dt_bias_f[l], dt_bias_b[l], d_skip[l], ssd_norm_g[l]) @ w_ssd_out[l]
        y_pool = _pool_branch(u_pool, pool_w[l], pool_scale[l]) @ w_pool_out[l]
        g_ssd, g_pool = jnp.split(jax.nn.sigmoid(gate_logits + gate_b[l]), N_BRANCHES, axis=-1)
        mix = (g_ssd * y_ssd + g_pool * y_pool) @ w_o[l]
        x = _post_ln(DEEPNORM_ALPHA * x + g1[:, None, :] * mix, ln1_g[l], ln1_b[l])
        h2 = _modulate(x, sh2, sc2)
        y_moe = _hier_moe(h2, router_wg[l], router_bg[l], router_we[l], router_be[l],
                          exp_w_gate[l], exp_w_up[l], exp_w_down[l])
        x = _post_ln(DEEPNORM_ALPHA * x + g2[:, None, :] * y_moe, ln2_g[l], ln2_b[l])
    return x
```

```python
import functools

import jax
import jax.numpy as jnp
from jax import lax
from jax.experimental import pallas as pl
from jax.experimental.pallas import tpu as pltpu

F32 = jnp.float32
BF16 = jnp.bfloat16
HIGHEST = lax.Precision.HIGHEST

SSD_HEAD_DIM = 64
SSD_GROUPS = 8
SSD_HEADS_PER_GROUP = 4
SSD_HEADS = SSD_GROUPS * SSD_HEADS_PER_GROUP
SSD_STATE = 128
SSD_CONV = 5
SSD_CHUNK = 128
SSD_NORM_EPS = 1e-5
POOL_WINDOWS = (2, 4, 8, 16)
MOE_GROUPS = 4
MOE_EXPERTS_PER_GROUP = 8
MOE_EXPERTS = MOE_GROUPS * MOE_EXPERTS_PER_GROUP
DEPTH = 1
DEEPNORM_ALPHA = (2.0 * DEPTH) ** 0.25
LN_EPS = 1e-5

LANES = 128
HALO = 16
VMEM_LIMIT = 48 * 1024 * 1024

TM_INPROJ = 1024
TN_INPROJ = 1152
TM_CONV = 512
TM_MIX = 256
TM_ROUTE = 512
TM_DISPATCH = 512
TM_EXPERT = 256
TM_COMBINE = 256


def _dot(a, b):
    return jnp.dot(a, b, preferred_element_type=F32)


def _split_hi_lo(v):
    hi = v.astype(BF16)
    lo = (v - hi.astype(F32)).astype(BF16)
    return hi, lo


def _dot3(a, b):
    a_hi, a_lo = _split_hi_lo(a)
    b_hi, b_lo = _split_hi_lo(b)
    return _dot(a_hi, b_hi) + _dot(a_lo, b_hi) + _dot(a_hi, b_lo)


def _layernorm(v):
    mu = jnp.mean(v, axis=-1, keepdims=True)
    vc = v - mu
    var = jnp.mean(vc * vc, axis=-1, keepdims=True)
    return vc * lax.rsqrt(var + LN_EPS)


def _silu(v):
    return v * jax.nn.sigmoid(v)


def _cparams(sem):
    return pltpu.CompilerParams(dimension_semantics=sem, vmem_limit_bytes=VMEM_LIMIT)


def _ada_kernel(c_ref, w_ref, b_ref, o_ref):
    o_ref[...] = jnp.dot(_silu(c_ref[...]), w_ref[...], precision=HIGHEST,
                         preferred_element_type=F32) + b_ref[...]


def _ada(c_pad, w, b):
    d, n = w.shape
    tn = 1024
    return pl.pallas_call(
        _ada_kernel,
        grid=(n // tn,),
        in_specs=[pl.BlockSpec((c_pad.shape[0], d), lambda j: (0, 0)),
                  pl.BlockSpec((d, tn), lambda j: (0, j)),
                  pl.BlockSpec((1, tn), lambda j: (0, j))],
        out_specs=pl.BlockSpec((c_pad.shape[0], tn), lambda j: (0, j)),
        out_shape=jax.ShapeDtypeStruct((c_pad.shape[0], n), F32),
        compiler_params=_cparams(("arbitrary",)),
        name="ada_mod",
    )(c_pad, w, b)


def _inproj_kernel(x_ref, sh_ref, sc_ref, w_ref, wdt_ref, o_ref, dt_ref, h_ref):
    @pl.when(pl.program_id(1) == 0)
    def _():
        h = _layernorm(x_ref[...]) * (1.0 + sc_ref[...]) + sh_ref[...]
        h_ref[...] = h.astype(BF16)
        dt_ref[...] = _dot3(h, wdt_ref[...])

    o_ref[...] = _dot(h_ref[...], w_ref[...]).astype(BF16)


def _inproj(x2, sh, sc, w_main, w_dt, seq):
    t, d = x2.shape
    n = w_main.shape[1]
    tm, tn = TM_INPROJ, TN_INPROJ
    tiles_per_seq = seq // tm
    return pl.pallas_call(
        _inproj_kernel,
        grid=(t // tm, n // tn),
        in_specs=[pl.BlockSpec((tm, d), lambda i, j: (i, 0)),
                  pl.BlockSpec((None, 1, d), lambda i, j: (i // tiles_per_seq, 0, 0)),
                  pl.BlockSpec((None, 1, d), lambda i, j: (i // tiles_per_seq, 0, 0)),
                  pl.BlockSpec((d, tn), lambda i, j: (0, j)),
                  pl.BlockSpec((d, LANES), lambda i, j: (0, 0))],
        out_specs=[pl.BlockSpec((tm, tn), lambda i, j: (i, j)),
                   pl.BlockSpec((tm, LANES), lambda i, j: (i, 0))],
        out_shape=[jax.ShapeDtypeStruct((t, n), BF16),
                   jax.ShapeDtypeStruct((t, LANES), F32)],
        scratch_shapes=[pltpu.VMEM((tm, d), BF16)],
        compiler_params=_cparams(("arbitrary", "arbitrary")),
        name="ln_inproj",
    )(x2, sh, sc, w_main, w_dt)


def _conv_kernel(tiles_per_seq, cur_ref, prev_ref, next_ref, w_ref, b_ref, o_ref, ext_ref):
    i = pl.program_id(0)
    tm = cur_ref.shape[0]
    first = (i % tiles_per_seq) == 0
    last = (i % tiles_per_seq) == tiles_per_seq - 1
    ext_ref[0:HALO, :] = jnp.where(first, 0.0, prev_ref[...].astype(F32))
    ext_ref[HALO:HALO + tm, :] = cur_ref[...].astype(F32)
    ext_ref[HALO + tm:, :] = jnp.where(last, 0.0, next_ref[...].astype(F32))
    pad = SSD_CONV // 2
    acc = b_ref[...] + w_ref[0:1, :] * ext_ref[pl.ds(HALO - pad, tm), :]
    for k in range(1, SSD_CONV):
        acc = acc + w_ref[k:k + 1, :] * ext_ref[pl.ds(HALO - pad + k, tm), :]
    o_ref[...] = _silu(acc).astype(BF16)


def _conv(big, col_off, conv_w, conv_b, seq):
    t = big.shape[0]
    cdim = conv_w.shape[1]
    tm, tc = TM_CONV, 1024
    tiles_per_seq = seq // tm
    cb0 = col_off // tc
    hb = tm // HALO
    nhb = t // HALO
    return pl.pallas_call(
        functools.partial(_conv_kernel, tiles_per_seq),
        grid=(t // tm, cdim // tc),
        in_specs=[pl.BlockSpec((tm, tc), lambda i, j: (i, cb0 + j)),
                  pl.BlockSpec((HALO, tc), lambda i, j: (jnp.maximum(i * hb - 1, 0), cb0 + j)),
                  pl.BlockSpec((HALO, tc), lambda i, j: (jnp.minimum((i + 1) * hb, nhb - 1), cb0 + j)),
                  pl.BlockSpec((SSD_CONV, tc), lambda i, j: (0, j)),
                  pl.BlockSpec((1, tc), lambda i, j: (0, j))],
        out_specs=pl.BlockSpec((tm, tc), lambda i, j: (i, j)),
        out_shape=jax.ShapeDtypeStruct((t, cdim), BF16),
        scratch_shapes=[pltpu.VMEM((tm + 2 * HALO, tc), F32)],
        compiler_params=_cparams(("arbitrary", "arbitrary")),
        name="conv_silu",
    )(big, big, big, conv_w, conv_b)


def _ssd_kernel(reverse, final, *refs):
    if final:
        (x_ref, b_ref, c_ref, dt_ref, alog_ref, bias_ref, z_ref, yprev_ref, dskip_ref, ng_ref,
         o_ref, st_ref) = refs
    else:
        x_ref, b_ref, c_ref, dt_ref, alog_ref, bias_ref, o_ref, st_ref = refs
    q = x_ref.shape[0]
    p = SSD_HEAD_DIM
    gw = SSD_HEADS_PER_GROUP * p

    @pl.when(pl.program_id(1) == 0)
    def _():
        st_ref[...] = jnp.zeros_like(st_ref)

    col0 = SSD_HEADS if reverse else 0
    dt = jax.nn.softplus(dt_ref[...] + bias_ref[...])
    da = dt * (-jnp.exp(alog_ref[...]))
    li = lax.broadcasted_iota(jnp.int32, (q, q), 0)
    si = lax.broadcasted_iota(jnp.int32, (q, q), 1)
    tri = (si >= li) if reverse else (si <= li)
    cs = jnp.dot(tri.astype(F32), da, precision=HIGHEST, preferred_element_type=F32)
    cs_t = cs.T
    cs_end = cs[0:1, :] if reverse else cs[q - 1:q, :]
    w_state = dt * jnp.exp(cs_end - cs)
    dec_out = jnp.exp(cs)
    dec_chunk = jnp.exp(cs_end)

    for g in range(SSD_GROUPS):
        bg = b_ref[:, g * SSD_STATE:(g + 1) * SSD_STATE]
        cg = c_ref[:, g * SSD_STATE:(g + 1) * SSD_STATE]
        cb = lax.dot_general(cg, bg, (((1,), (1,)), ((), ())), preferred_element_type=F32)
        h_in = st_ref[g]
        y_off = _dot(cg, h_in.astype(BF16))
        ys, xws, decs = [], [], []
        for r in range(SSD_HEADS_PER_GROUP):
            hh = g * SSD_HEADS_PER_GROUP + r
            ch = col0 + hh
            xh = x_ref[:, hh * p:(hh + 1) * p].astype(F32)
            seg = jnp.where(tri, cs[:, ch:ch + 1] - cs_t[ch:ch + 1, :], -jnp.inf)
            scores = (cb * jnp.exp(seg)).astype(BF16)
            y_h = _dot(scores, (xh * dt[:, ch:ch + 1]).astype(BF16))
            y_h = y_h + y_off[:, r * p:(r + 1) * p] * dec_out[:, ch:ch + 1]
            ys.append(y_h)
            xws.append((xh * w_state[:, ch:ch + 1]).astype(BF16))
            decs.append(jnp.broadcast_to(dec_chunk[:, ch:ch + 1], (1, p)))
        xw = jnp.concatenate(xws, axis=1)
        s_new = lax.dot_general(bg, xw, (((0,), (0,)), ((), ())), preferred_element_type=F32)
        st_ref[g] = h_in * jnp.concatenate(decs, axis=1) + s_new
        y_g = jnp.concatenate(ys, axis=1)
        cols = slice(g * gw, (g + 1) * gw)
        if final:
            y_g = y_g + yprev_ref[:, cols].astype(F32) + x_ref[:, cols].astype(F32) * dskip_ref[:, cols]
            yg = y_g * _silu(z_ref[:, cols].astype(F32))
            ms = jnp.mean(yg * yg, axis=-1, keepdims=True)
            o_ref[:, cols] = (yg * lax.rsqrt(ms + SSD_NORM_EPS) * ng_ref[:, cols]).astype(BF16)
        else:
            o_ref[:, cols] = y_g.astype(BF16)


def _ssd(xc, dt_raw, alog_row, bias_row, batch, reverse, final_inputs=None):
    t = xc.shape[0]
    q = SSD_CHUNK
    nc = t // batch // q
    d_inner = SSD_HEADS * SSD_HEAD_DIM
    bcw = SSD_GROUPS * SSD_STATE
    final = final_inputs is not None

    def row(b, c):
        return b * nc + ((nc - 1 - c) if reverse else c)

    in_specs = [pl.BlockSpec((q, d_inner), lambda b, c: (row(b, c), 0)),
                pl.BlockSpec((q, bcw), lambda b, c: (row(b, c), d_inner // bcw)),
                pl.BlockSpec((q, bcw), lambda b, c: (row(b, c), d_inner // bcw + 1)),
                pl.BlockSpec((q, LANES), lambda b, c: (row(b, c), 0)),
                pl.BlockSpec((1, LANES), lambda b, c: (0, 0)),
                pl.BlockSpec((1, LANES), lambda b, c: (0, 0))]
    args = [xc, xc, xc, dt_raw, alog_row, bias_row]
    if final:
        big, y_prev, dskip_row, ng_row = final_inputs
        in_specs += [pl.BlockSpec((q, d_inner), lambda b, c: (row(b, c), 0)),
                     pl.BlockSpec((q, d_inner), lambda b, c: (row(b, c), 0)),
                     pl.BlockSpec((1, d_inner), lambda b, c: (0, 0)),
                     pl.BlockSpec((1, d_inner), lambda b, c: (0, 0))]
        args += [big, y_prev, dskip_row, ng_row]
    return pl.pallas_call(
        functools.partial(_ssd_kernel, reverse, final),
        grid=(batch, nc),
        in_specs=in_specs,
        out_specs=pl.BlockSpec((q, d_inner), lambda b, c: (row(b, c), 0)),
        out_shape=jax.ShapeDtypeStruct((t, d_inner), BF16),
        scratch_shapes=[pltpu.VMEM((SSD_GROUPS, SSD_STATE, SSD_HEADS_PER_GROUP * SSD_HEAD_DIM), F32)],
        compiler_params=_cparams(("arbitrary", "arbitrary")),
        name="ssd_bwd" if reverse else "ssd_fwd",
    )(*args)


def _mix_kernel(tiles_per_seq, seq, yn_ref, u_ref, up_ref, un_ref, ga_ref, gb_ref, x_ref,
                wssd_ref, pw_ref, wpo_ref, wo_ref, wr_ref,
                gateb_ref, pscale_ref, ln1g_ref, ln1b_ref, rb_ref, g1_ref, sh2_ref, sc2_ref,
                x1_ref, h2_ref, lg_ref, ext_ref):
    i = pl.program_id(0)
    tm, width = u_ref.shape
    first = (i % tiles_per_seq) == 0
    last = (i % tiles_per_seq) == tiles_per_seq - 1
    u = u_ref[...].astype(F32)
    ext_ref[0:HALO, :] = jnp.where(first, 0.0, up_ref[...].astype(F32))
    ext_ref[HALO:HALO + tm, :] = u
    ext_ref[HALO + tm:, :] = jnp.where(last, 0.0, un_ref[...].astype(F32))
    tpos = (i % tiles_per_seq) * tm + lax.broadcasted_iota(jnp.int32, (tm, 1), 0)
    gd = width // len(POOL_WINDOWS)
    mixed = []
    for gi, w in enumerate(POOL_WINDOWS):
        cols = slice(gi * gd, (gi + 1) * gd)
        s = ext_ref[pl.ds(HALO - w // 2, tm), cols]
        for k in range(1, w):
            s = s + ext_ref[pl.ds(HALO - w // 2 + k, tm), cols]
        cnt = (jnp.minimum(tpos + w // 2, seq) - jnp.maximum(tpos - w // 2, 0)).astype(F32)
        diff = s / cnt - u[:, cols]
        mixed.append(_dot(diff.astype(BF16), pw_ref[gi]))
    mixed = jnp.concatenate(mixed, axis=1) * pscale_ref[...]
    y_pool = _dot(mixed.astype(BF16), wpo_ref[...])
    y_ssd = _dot(yn_ref[...], wssd_ref[...])
    d = y_ssd.shape[1]
    g_ssd = jax.nn.sigmoid(ga_ref[...].astype(F32) + gateb_ref[:, 0:d])
    g_pool = jax.nn.sigmoid(gb_ref[...].astype(F32) + gateb_ref[:, d:2 * d])
    mix = _dot((g_ssd * y_ssd + g_pool * y_pool).astype(BF16), wo_ref[...])
    x1 = _layernorm(DEEPNORM_ALPHA * x_ref[...] + g1_ref[...] * mix) * ln1g_ref[...] + ln1b_ref[...]
    x1_ref[...] = x1
    h2 = _layernorm(x1) * (1.0 + sc2_ref[...]) + sh2_ref[...]
    h2_ref[...] = h2
    lg_ref[...] = _dot3(h2, wr_ref[...]) + rb_ref[...]


def _mix(yn, big, x2, w_ssd_out, pool_w, w_pool_out, w_o, w_router, gate_b, pool_scale, ln1_g, ln1_b,
         r_bias, g1, sh2, sc2, seq, pool_col, gate_col):
    t, d = x2.shape
    tm = TM_MIX
    tiles_per_seq = seq // tm
    hb = tm // HALO
    nhb = t // HALO
    pcb = pool_col // d
    gcb = gate_col // d
    full = lambda a: pl.BlockSpec(a.shape, lambda i: (0,) * a.ndim)
    per_batch = pl.BlockSpec((None, 1, d), lambda i: (i // tiles_per_seq, 0, 0))
    return pl.pallas_call(
        functools.partial(_mix_kernel, tiles_per_seq, seq),
        grid=(t // tm,),
        in_specs=[pl.BlockSpec((tm, yn.shape[1]), lambda i: (i, 0)),
                  pl.BlockSpec((tm, d), lambda i: (i, pcb)),
                  pl.BlockSpec((HALO, d), lambda i: (jnp.maximum(i * hb - 1, 0), pcb)),
                  pl.BlockSpec((HALO, d), lambda i: (jnp.minimum((i + 1) * hb, nhb - 1), pcb)),
                  pl.BlockSpec((tm, d), lambda i: (i, gcb)),
                  pl.BlockSpec((tm, d), lambda i: (i, gcb + 1)),
                  pl.BlockSpec((tm, d), lambda i: (i, 0)),
                  full(w_ssd_out), full(pool_w), full(w_pool_out), full(w_o), full(w_router),
                  full(gate_b), full(pool_scale), full(ln1_g), full(ln1_b), full(r_bias),
                  per_batch, per_batch, per_batch],
        out_specs=[pl.BlockSpec((tm, d), lambda i: (i, 0)),
                   pl.BlockSpec((tm, d), lambda i: (i, 0)),
                   pl.BlockSpec((tm, LANES), lambda i: (i, 0))],
        out_shape=[jax.ShapeDtypeStruct((t, d), F32),
                   jax.ShapeDtypeStruct((t, d), F32),
                   jax.ShapeDtypeStruct((t, LANES), F32)],
        scratch_shapes=[pltpu.VMEM((tm + 2 * HALO, d), F32)],
        compiler_params=_cparams(("arbitrary",)),
        name="mix_postln",
    )(yn, big, big, big, big, big, x2, w_ssd_out, pool_w, w_pool_out, w_o, w_router,
      gate_b, pool_scale, ln1_g, ln1_b, r_bias, g1, sh2, sc2)


def _route_kernel(lg_ref, rt_ref, cnt_ref, carry_ref):
    @pl.when(pl.program_id(0) == 0)
    def _():
        carry_ref[...] = jnp.zeros_like(carry_ref)

    lg = lg_ref[...]
    tm = lg.shape[0]
    lane = lax.broadcasted_iota(jnp.int32, lg.shape, 1).astype(F32)
    neg = -jnp.inf
    big_lane = float(LANES)
    gl = jnp.where(lane < MOE_GROUPS, lg, neg)
    gmax = jnp.max(gl, axis=-1, keepdims=True)
    g_w = 1.0 / jnp.sum(jnp.exp(gl - gmax), axis=-1, keepdims=True)
    g_idx = jnp.min(jnp.where(gl == gmax, lane, big_lane), axis=-1, keepdims=True)
    lo = MOE_GROUPS + MOE_EXPERTS_PER_GROUP * g_idx
    el = jnp.where((lane >= lo) & (lane < lo + MOE_EXPERTS_PER_GROUP), lg, neg)
    m1 = jnp.max(el, axis=-1, keepdims=True)
    i1 = jnp.min(jnp.where(el == m1, lane, big_lane), axis=-1, keepdims=True)
    el2 = jnp.where(lane == i1, neg, el)
    m2 = jnp.max(el2, axis=-1, keepdims=True)
    i2 = jnp.min(jnp.where(el2 == m2, lane, big_lane), axis=-1, keepdims=True)
    e = jnp.exp(m2 - m1)
    w1 = g_w / (1.0 + e)
    w2 = g_w * e / (1.0 + e)
    onehot = jnp.where((lane == i1) | (lane == i2), 1.0, 0.0)
    ri = lax.broadcasted_iota(jnp.int32, (tm, tm), 0)
    ci = lax.broadcasted_iota(jnp.int32, (tm, tm), 1)
    earlier = jnp.where(ci < ri, 1.0, 0.0).astype(BF16)
    rank = _dot(earlier, onehot.astype(BF16)) + carry_ref[...]
    r1 = jnp.sum(jnp.where(lane == i1, rank, 0.0), axis=-1, keepdims=True)
    r2 = jnp.sum(jnp.where(lane == i2, rank, 0.0), axis=-1, keepdims=True)
    carry_ref[...] = carry_ref[...] + jnp.sum(onehot, axis=0, keepdims=True)
    cnt_ref[...] = carry_ref[...]
    out = jnp.where(lane == 0, i1 - MOE_GROUPS, 0.0)
    out = jnp.where(lane == 1, i2 - MOE_GROUPS, out)
    out = jnp.where(lane == 2, r1, out)
    out = jnp.where(lane == 3, r2, out)
    out = jnp.where(lane == 4, w1, out)
    out = jnp.where(lane == 5, w2, out)
    rt_ref[...] = out


def _route(logits):
    t = logits.shape[0]
    tm = TM_ROUTE
    return pl.pallas_call(
        _route_kernel,
        grid=(t // tm,),
        in_specs=[pl.BlockSpec((tm, LANES), lambda i: (i, 0))],
        out_specs=[pl.BlockSpec((tm, LANES), lambda i: (i, 0)),
                   pl.BlockSpec((1, LANES), lambda i: (0, 0))],
        out_shape=[jax.ShapeDtypeStruct((t, LANES), F32),
                   jax.ShapeDtypeStruct((1, LANES), F32)],
        scratch_shapes=[pltpu.VMEM((1, LANES), F32)],
        compiler_params=_cparams(("arbitrary",)),
        name="route",
    )(logits)


def _row_copy(src_ref, src_row, dst_ref, dst_row, sem):
    return pltpu.make_async_copy(src_ref.at[pl.ds(src_row, 1)], dst_ref.at[pl.ds(dst_row, 1)], sem)


def _dispatch_kernel(pos_ref, h2_ref, xs_in_ref, xs_ref, sem):
    del xs_in_ref
    tm = h2_ref.shape[0]

    def start(r, carry):
        _row_copy(h2_ref, r, xs_ref, pos_ref[0, 2 * r], sem).start()
        _row_copy(h2_ref, r, xs_ref, pos_ref[0, 2 * r + 1], sem).start()
        return carry

    def wait(r, carry):
        _row_copy(h2_ref, r, xs_ref, pos_ref[0, 2 * r], sem).wait()
        _row_copy(h2_ref, r, xs_ref, pos_ref[0, 2 * r + 1], sem).wait()
        return carry

    lax.fori_loop(0, tm, start, 0)
    lax.fori_loop(0, tm, wait, 0)


def _dispatch(pos, h2, xs_zero):
    t, d = h2.shape
    tm = TM_DISPATCH
    pos3 = pos.reshape(t // tm, 1, 2 * tm)
    return pl.pallas_call(
        _dispatch_kernel,
        grid=(t // tm,),
        in_specs=[pl.BlockSpec((None, 1, 2 * tm), lambda i: (i, 0, 0), memory_space=pltpu.SMEM),
                  pl.BlockSpec((tm, d), lambda i: (i, 0)),
                  pl.BlockSpec(memory_space=pl.ANY)],
        out_specs=pl.BlockSpec(memory_space=pl.ANY),
        out_shape=jax.ShapeDtypeStruct(xs_zero.shape, xs_zero.dtype),
        scratch_shapes=[pltpu.SemaphoreType.DMA(())],
        input_output_aliases={2: 0},
        compiler_params=_cparams(("arbitrary",)),
        name="dispatch",
    )(pos3, h2, xs_zero)


def _expert_kernel(te_ref, xs_ref, wg_ref, wu_ref, wd_ref, y_ref, wgb_ref, wub_ref, wdb_ref):
    j = pl.program_id(0)
    new_expert = jnp.logical_or(j == 0, te_ref[j] != te_ref[jnp.maximum(j - 1, 0)])

    @pl.when(new_expert)
    def _():
        wgb_ref[...] = wg_ref[...].astype(BF16)
        wub_ref[...] = wu_ref[...].astype(BF16)
        wdb_ref[...] = wd_ref[...].astype(BF16)

    xb = xs_ref[...].astype(BF16)
    act = _silu(_dot(xb, wgb_ref[...])) * _dot(xb, wub_ref[...])
    y_ref[...] = _dot(act.astype(BF16), wdb_ref[...])


def _experts(tile_expert, xs, w_gate, w_up, w_down):
    p_rows, d = xs.shape
    hdim = w_gate.shape[2]
    tm = TM_EXPERT
    grid_spec = pltpu.PrefetchScalarGridSpec(
        num_scalar_prefetch=1,
        grid=(p_rows // tm,),
        in_specs=[pl.BlockSpec((tm, d), lambda j, te: (j, 0)),
                  pl.BlockSpec((None, d, hdim), lambda j, te: (te[j], 0, 0)),
                  pl.BlockSpec((None, d, hdim), lambda j, te: (te[j], 0, 0)),
                  pl.BlockSpec((None, hdim, d), lambda j, te: (te[j], 0, 0))],
        out_specs=pl.BlockSpec((tm, d), lambda j, te: (j, 0)),
        scratch_shapes=[pltpu.VMEM((d, hdim), BF16), pltpu.VMEM((d, hdim), BF16),
                        pltpu.VMEM((hdim, d), BF16)],
    )
    return pl.pallas_call(
        _expert_kernel,
        grid_spec=grid_spec,
        out_shape=jax.ShapeDtypeStruct((p_rows, d), F32),
        compiler_params=_cparams(("arbitrary",)),
        name="experts",
    )(tile_expert, xs, w_gate, w_up, w_down)


def _combine_kernel(pos_ref, y_ref, rt_ref, x1_ref, g2_ref, lng_ref, lnb_ref, o_ref, buf_ref, sem):
    tm = x1_ref.shape[0]

    def start(r, carry):
        _row_copy(y_ref, pos_ref[0, 2 * r], buf_ref.at[0], r, sem).start()
        _row_copy(y_ref, pos_ref[0, 2 * r + 1], buf_ref.at[1], r, sem).start()
        return carry

    def wait(r, carry):
        _row_copy(y_ref, pos_ref[0, 2 * r], buf_ref.at[0], r, sem).wait()
        _row_copy(y_ref, pos_ref[0, 2 * r + 1], buf_ref.at[1], r, sem).wait()
        return carry

    lax.fori_loop(0, tm, start, 0)
    lax.fori_loop(0, tm, wait, 0)
    rt = rt_ref[...]
    y_moe = rt[:, 4:5] * buf_ref[0] + rt[:, 5:6] * buf_ref[1]
    v = DEEPNORM_ALPHA * x1_ref[...] + g2_ref[...] * y_moe
    o_ref[...] = _layernorm(v) * lng_ref[...] + lnb_ref[...]


def _combine(pos, y_sorted, rt, x1, g2, ln_g, ln_b, seq):
    t, d = x1.shape
    tm = TM_COMBINE
    tiles_per_seq = seq // tm
    pos3 = pos.reshape(t // tm, 1, 2 * tm)
    return pl.pallas_call(
        _combine_kernel,
        grid=(t // tm,),
        in_specs=[pl.BlockSpec((None, 1, 2 * tm), lambda i: (i, 0, 0), memory_space=pltpu.SMEM),
                  pl.BlockSpec(memory_space=pl.ANY),
                  pl.BlockSpec((tm, LANES), lambda i: (i, 0)),
                  pl.BlockSpec((tm, d), lambda i: (i, 0)),
                  pl.BlockSpec((None, 1, d), lambda i: (i // tiles_per_seq, 0, 0)),
                  pl.BlockSpec((1, d), lambda i: (0, 0)),
                  pl.BlockSpec((1, d), lambda i: (0, 0))],
        out_specs=pl.BlockSpec((tm, d), lambda i: (i, 0)),
        out_shape=jax.ShapeDtypeStruct((t, d), F32),
        scratch_shapes=[pltpu.VMEM((2, tm, d), F32), pltpu.SemaphoreType.DMA(())],
        compiler_params=_cparams(("arbitrary",)),
        name="combine_postln",
    )(pos3, y_sorted, rt, x1, g2, ln_g, ln_b)


def _layer(x2, c_pad, batch, seq, w_ada, b_ada, w_in, conv_w, conv_b, a_log_f, a_log_b, dt_bias_f,
           dt_bias_b, d_skip, ssd_norm_g, w_ssd_out, pool_w, pool_scale, w_pool_out, gate_b, w_o,
           ln1_g, ln1_b, router_wg, router_bg, router_we, router_be, exp_w_gate, exp_w_up,
           exp_w_down, ln2_g, ln2_b):
    t, d = x2.shape
    d_inner = SSD_HEADS * SSD_HEAD_DIM
    conv_dim = d_inner + 2 * SSD_GROUPS * SSD_STATE
    pool_width = pool_scale.shape[0]

    mod = _ada(c_pad, w_ada, b_ada[None, :])[:batch]
    sh1, sc1, g1, sh2, sc2, g2 = [m[:, None, :] for m in jnp.split(mod, 6, axis=-1)]

    o_z, o_xbc = 0, d_inner
    o_dt = o_xbc + conv_dim
    o_pool = o_dt + 2 * SSD_HEADS
    o_gate = o_pool + pool_width
    w_main = jnp.concatenate([w_in[:, :o_dt], w_in[:, o_pool:]], axis=1).astype(BF16)
    w_dt = jnp.pad(w_in[:, o_dt:o_pool], ((0, 0), (0, LANES - 2 * SSD_HEADS)))
    pool_col = o_dt
    gate_col = o_dt + pool_width
    big, dt_raw = _inproj(x2, sh1, sc1, w_main, w_dt, seq)

    xc = _conv(big, o_xbc, conv_w, conv_b[None, :], seq)

    pad_row = lambda f, b: jnp.pad(jnp.concatenate([f, b]), (0, LANES - 2 * SSD_HEADS))[None, :]
    alog_row = pad_row(a_log_f, a_log_b)
    bias_row = pad_row(dt_bias_f, dt_bias_b)
    dskip_row = jnp.repeat(d_skip, SSD_HEAD_DIM)[None, :]
    y_bwd = _ssd(xc, dt_raw, alog_row, bias_row, batch, reverse=True)
    yn = _ssd(xc, dt_raw, alog_row, bias_row, batch, reverse=False,
              final_inputs=(big, y_bwd, dskip_row, ssd_norm_g[None, :]))

    w_router = jnp.pad(jnp.concatenate([router_wg, router_we], axis=1),
                       ((0, 0), (0, LANES - MOE_GROUPS - MOE_EXPERTS)))
    r_bias = jnp.pad(jnp.concatenate([router_bg, router_be]), (0, LANES - MOE_GROUPS - MOE_EXPERTS))[None, :]
    x1, h2, logits = _mix(yn, big, x2, w_ssd_out.astype(BF16), pool_w.astype(BF16),
                          w_pool_out.astype(BF16), w_o.astype(BF16), w_router, gate_b[None, :],
                          pool_scale[None, :], ln1_g[None, :], ln1_b[None, :], r_bias, g1, sh2, sc2,
                          seq, pool_col, gate_col)

    rt, counts = _route(logits)

    tme = TM_EXPERT
    cnt = counts[0, MOE_GROUPS:MOE_GROUPS + MOE_EXPERTS].astype(jnp.int32)
    padded = ((cnt + tme - 1) // tme) * tme
    ends = jnp.cumsum(padded)
    off = ends - padded
    pos = jnp.take(off, rt[:, 0:2].astype(jnp.int32)) + rt[:, 2:4].astype(jnp.int32)
    p_rows = 2 * t + MOE_EXPERTS * tme
    n_tiles = p_rows // tme
    tile_expert = jnp.minimum(
        jnp.searchsorted(ends // tme, jnp.arange(n_tiles, dtype=jnp.int32), side="right"),
        MOE_EXPERTS - 1).astype(jnp.int32)

    xs = _dispatch(pos, h2, jnp.zeros((p_rows, d), F32))
    y_sorted = _experts(tile_expert, xs, exp_w_gate, exp_w_up, exp_w_down)
    return _combine(pos, y_sorted, rt, x1, g2, ln2_g[None, :], ln2_b[None, :], seq)


def kernel(x, c, w_ada, b_ada, w_in, conv_w, conv_b, a_log_f, a_log_b, dt_bias_f, dt_bias_b, d_skip,
           ssd_norm_g, w_ssd_out, pool_w, pool_scale, w_pool_out, gate_b, w_o, ln1_g, ln1_b,
           router_wg, router_bg, router_we, router_be, exp_w_gate, exp_w_up, exp_w_down, ln2_g, ln2_b):
    batch, seq, d = x.shape
    x2 = x.reshape(batch * seq, d)
    c_pad = jnp.pad(c, ((0, 8 - batch), (0, 0)))
    params = (w_ada, b_ada, w_in, conv_w, conv_b, a_log_f, a_log_b, dt_bias_f, dt_bias_b, d_skip,
              ssd_norm_g, w_ssd_out, pool_w, pool_scale, w_pool_out, gate_b, w_o, ln1_g, ln1_b,
              router_wg, router_bg, router_we, router_be, exp_w_gate, exp_w_up, exp_w_down, ln2_g, ln2_b)
    for l in range(w_ada.shape[0]):
        x2 = _layer(x2, c_pad, batch, seq, *[p[l] for p in params])
    return x2.reshape(batch, seq, d)
```

```python
import functools

import jax
import jax.numpy as jnp
from jax import lax
from jax.experimental import pallas as pl
from jax.experimental.pallas import tpu as pltpu

F32 = jnp.float32
BF16 = jnp.bfloat16
HIGHEST = lax.Precision.HIGHEST

SSD_HEAD_DIM = 64
SSD_GROUPS = 8
SSD_HEADS_PER_GROUP = 4
SSD_HEADS = SSD_GROUPS * SSD_HEADS_PER_GROUP
SSD_STATE = 128
SSD_CONV = 5
SSD_CHUNK = 128
SSD_NORM_EPS = 1e-5
POOL_WINDOWS = (2, 4, 8, 16)
MOE_GROUPS = 4
MOE_EXPERTS_PER_GROUP = 8
MOE_EXPERTS = MOE_GROUPS * MOE_EXPERTS_PER_GROUP
DEPTH = 1
DEEPNORM_ALPHA = (2.0 * DEPTH) ** 0.25
LN_EPS = 1e-5

LANES = 128
HALO = 16
VMEM_LIMIT = 48 * 1024 * 1024
DMA_UNROLL = 8

TM_INPROJ = 1024
TN_INPROJ = 1152
TM_CONV = 512
TM_MIX = 256
TM_ROUTE = 512
TM_DISPATCH = 512
TM_EXPERT = 256
TM_COMBINE = 256


def _dot(a, b):
    return jnp.dot(a, b, preferred_element_type=F32)


def _split_hi_lo(v):
    hi = v.astype(BF16)
    lo = (v - hi.astype(F32)).astype(BF16)
    return hi, lo


def _split3(v):
    hi = v.astype(BF16)
    r = v - hi.astype(F32)
    mid = r.astype(BF16)
    lo = (r - mid.astype(F32)).astype(BF16)
    return hi, mid, lo


def _dot3(a, b):
    a_hi, a_lo = _split_hi_lo(a)
    b_hi, b_lo = _split_hi_lo(b)
    return _dot(a_hi, b_hi) + _dot(a_lo, b_hi) + _dot(a_hi, b_lo)


def _layernorm(v):
    mu = jnp.mean(v, axis=-1, keepdims=True)
    vc = v - mu
    var = jnp.mean(vc * vc, axis=-1, keepdims=True)
    return vc * lax.rsqrt(var + LN_EPS)


def _silu(v):
    return v * jax.nn.sigmoid(v)


def _cparams(sem):
    return pltpu.CompilerParams(dimension_semantics=sem, vmem_limit_bytes=VMEM_LIMIT)


def _ada_kernel(c_ref, w_ref, b_ref, o_ref):
    o_ref[...] = jnp.dot(_silu(c_ref[...]), w_ref[...], precision=HIGHEST,
                         preferred_element_type=F32) + b_ref[...]


def _ada(c_pad, w, b):
    d, n = w.shape
    tn = 1024
    return pl.pallas_call(
        _ada_kernel,
        grid=(n // tn,),
        in_specs=[pl.BlockSpec((c_pad.shape[0], d), lambda j: (0, 0)),
                  pl.BlockSpec((d, tn), lambda j: (0, j)),
                  pl.BlockSpec((1, tn), lambda j: (0, j))],
        out_specs=pl.BlockSpec((c_pad.shape[0], tn), lambda j: (0, j)),
        out_shape=jax.ShapeDtypeStruct((c_pad.shape[0], n), F32),
        compiler_params=_cparams(("arbitrary",)),
        name="ada_mod",
    )(c_pad, w, b)


def _inproj_kernel(x_ref, sh_ref, sc_ref, w_ref, wdt_ref, o_ref, dt_ref, h_ref):
    @pl.when(pl.program_id(1) == 0)
    def _():
        h = _layernorm(x_ref[...]) * (1.0 + sc_ref[...]) + sh_ref[...]
        h_ref[...] = h.astype(BF16)
        dt_ref[...] = _dot3(h, wdt_ref[...])

    o_ref[...] = _dot(h_ref[...], w_ref[...]).astype(BF16)


def _inproj(x2, sh, sc, w_main, w_dt, seq):
    t, d = x2.shape
    n = w_main.shape[1]
    tm, tn = TM_INPROJ, TN_INPROJ
    tiles_per_seq = seq // tm
    return pl.pallas_call(
        _inproj_kernel,
        grid=(t // tm, n // tn),
        in_specs=[pl.BlockSpec((tm, d), lambda i, j: (i, 0)),
                  pl.BlockSpec((None, 1, d), lambda i, j: (i // tiles_per_seq, 0, 0)),
                  pl.BlockSpec((None, 1, d), lambda i, j: (i // tiles_per_seq, 0, 0)),
                  pl.BlockSpec((d, tn), lambda i, j: (0, j)),
                  pl.BlockSpec((d, LANES), lambda i, j: (0, 0))],
        out_specs=[pl.BlockSpec((tm, tn), lambda i, j: (i, j)),
                   pl.BlockSpec((tm, LANES), lambda i, j: (i, 0))],
        out_shape=[jax.ShapeDtypeStruct((t, n), BF16),
                   jax.ShapeDtypeStruct((t, LANES), F32)],
        scratch_shapes=[pltpu.VMEM((tm, d), BF16)],
        compiler_params=_cparams(("arbitrary", "arbitrary")),
        name="ln_inproj",
    )(x2, sh, sc, w_main, w_dt)


def _conv_kernel(tiles_per_seq, cur_ref, prev_ref, next_ref, w_ref, b_ref, o_ref, ext_ref):
    i = pl.program_id(0)
    tm = cur_ref.shape[0]
    first = (i % tiles_per_seq) == 0
    last = (i % tiles_per_seq) == tiles_per_seq - 1
    ext_ref[0:HALO, :] = jnp.where(first, 0.0, prev_ref[...].astype(F32))
    ext_ref[HALO:HALO + tm, :] = cur_ref[...].astype(F32)
    ext_ref[HALO + tm:, :] = jnp.where(last, 0.0, next_ref[...].astype(F32))
    pad = SSD_CONV // 2
    acc = b_ref[...] + w_ref[0:1, :] * ext_ref[pl.ds(HALO - pad, tm), :]
    for k in range(1, SSD_CONV):
        acc = acc + w_ref[k:k + 1, :] * ext_ref[pl.ds(HALO - pad + k, tm), :]
    o_ref[...] = _silu(acc).astype(BF16)


def _conv(big, col_off, conv_w, conv_b, seq):
    t = big.shape[0]
    cdim = conv_w.shape[1]
    tm, tc = TM_CONV, 1024
    tiles_per_seq = seq // tm
    cb0 = col_off // tc
    hb = tm // HALO
    nhb = t // HALO
    return pl.pallas_call(
        functools.partial(_conv_kernel, tiles_per_seq),
        grid=(t // tm, cdim // tc),
        in_specs=[pl.BlockSpec((tm, tc), lambda i, j: (i, cb0 + j)),
                  pl.BlockSpec((HALO, tc), lambda i, j: (jnp.maximum(i * hb - 1, 0), cb0 + j)),
                  pl.BlockSpec((HALO, tc), lambda i, j: (jnp.minimum((i + 1) * hb, nhb - 1), cb0 + j)),
                  pl.BlockSpec((SSD_CONV, tc), lambda i, j: (0, j)),
                  pl.BlockSpec((1, tc), lambda i, j: (0, j))],
        out_specs=pl.BlockSpec((tm, tc), lambda i, j: (i, j)),
        out_shape=jax.ShapeDtypeStruct((t, cdim), BF16),
        scratch_shapes=[pltpu.VMEM((tm + 2 * HALO, tc), F32)],
        compiler_params=_cparams(("arbitrary", "arbitrary")),
        name="conv_silu",
    )(big, big, big, conv_w, conv_b)


def _tri_cumsum(tri, v):
    hi, mid, lo = _split3(v)
    return _dot(tri, hi) + _dot(tri, mid) + _dot(tri, lo)


def _expand_heads(v, e_ref):
    hi, lo = _split_hi_lo(v)
    return _dot(jnp.concatenate([hi, lo], axis=1), e_ref[...])


def _state_step(g, x_g, b_ref, c_ref, st_ref, w_exp, dec_exp, end_row):
    gw = x_g.shape[1]
    cols = slice(g * gw, (g + 1) * gw)
    bg = b_ref[:, g * SSD_STATE:(g + 1) * SSD_STATE]
    cg = c_ref[:, g * SSD_STATE:(g + 1) * SSD_STATE]
    h_in = st_ref[g]
    y_off = _dot(cg, h_in.astype(BF16)) * dec_exp[:, cols]
    xw = (x_g * w_exp[:, cols]).astype(BF16)
    s_new = lax.dot_general(bg, xw, (((0,), (0,)), ((), ())), preferred_element_type=F32)
    st_ref[g] = h_in * dec_exp[end_row:end_row + 1, cols] + s_new
    return y_off


def _ssd_bwd_kernel(x_ref, b_ref, c_ref, dt_ref, alog_ref, bias_ref, e64_ref, o_ref, st_ref):
    q = x_ref.shape[0]
    gw = SSD_HEADS_PER_GROUP * SSD_HEAD_DIM

    @pl.when(pl.program_id(1) == 0)
    def _():
        st_ref[...] = jnp.zeros_like(st_ref)

    dt = jax.nn.softplus(dt_ref[...] + bias_ref[...])
    da = dt * (-jnp.exp(alog_ref[...]))
    li = lax.broadcasted_iota(jnp.int32, (q, q), 0)
    si = lax.broadcasted_iota(jnp.int32, (q, q), 1)
    cs = _tri_cumsum(jnp.where(si >= li, 1.0, 0.0).astype(BF16), da)
    w_exp = _expand_heads(dt * jnp.exp(cs[0:1, :] - cs), e64_ref)
    dec_exp = _expand_heads(jnp.exp(cs), e64_ref)
    for g in range(SSD_GROUPS):
        cols = slice(g * gw, (g + 1) * gw)
        x_g = x_ref[:, cols].astype(F32)
        o_ref[:, cols] = _state_step(g, x_g, b_ref, c_ref, st_ref, w_exp, dec_exp, 0).astype(BF16)


def _ssd_fwd_kernel(x_ref, b_ref, c_ref, dt_ref, alog_ref, bias_ref, e64_ref, ecs_ref, z_ref,
                    yb_ref, dskip_ref, ng_ref, o_ref, st_ref):
    q = x_ref.shape[0]
    p = SSD_HEAD_DIM
    hpg = SSD_HEADS_PER_GROUP
    gw = hpg * p

    @pl.when(pl.program_id(1) == 0)
    def _():
        st_ref[...] = jnp.zeros_like(st_ref)

    dt = jax.nn.softplus(dt_ref[...] + bias_ref[...])
    da = dt * (-jnp.exp(alog_ref[...]))
    li = lax.broadcasted_iota(jnp.int32, (q, q), 0)
    si = lax.broadcasted_iota(jnp.int32, (q, q), 1)
    before = si <= li
    after = si >= li
    lane = lax.broadcasted_iota(jnp.int32, (q, LANES), 1)
    tri2 = jnp.concatenate([jnp.where(before, 1.0, 0.0), jnp.where(after, 1.0, 0.0)], axis=1).astype(BF16)
    da2 = jnp.concatenate([jnp.where(lane < SSD_HEADS, da, 0.0),
                           jnp.where((lane >= SSD_HEADS) & (lane < 2 * SSD_HEADS), da, 0.0)], axis=0)
    cs = _tri_cumsum(tri2, da2)
    csp_t = (cs - jnp.log(dt)).T
    hi, mid, lo = _split3(cs)
    nhd = 2 * SSD_HEADS
    v_cs = jnp.concatenate([hi[:, :nhd].astype(F32), mid[:, :nhd].astype(F32), lo[:, :nhd].astype(F32),
                            jnp.zeros((q, nhd), F32)], axis=1).astype(BF16)
    to_end = jnp.where(lane < SSD_HEADS, cs[q - 1:q, :] - cs, 0.0)
    w_exp = _expand_heads(dt * jnp.exp(to_end), e64_ref)
    dec_exp = _expand_heads(jnp.exp(cs), e64_ref)
    half =lax.broadcasted_iota(jnp.int32, (q, 2 * p), 1) < p

    for g in range(SSD_GROUPS):
        cols = slice(g * gw, (g + 1) * gw)
        bg = b_ref[:, g * SSD_STATE:(g + 1) * SSD_STATE]
        cg = c_ref[:, g * SSD_STATE:(g + 1) * SSD_STATE]
        cb = lax.dot_general(cg, bg, (((1,), (1,)), ((), ())), preferred_element_type=F32)
        colb = _dot(v_cs, ecs_ref[:, g * 2 * hpg * q:(g + 1) * 2 * hpg * q])
        x_g = x_ref[:, cols].astype(F32)
        ms = []
        for r in range(hpg):
            hh = g * hpg + r
            seg_f = colb[:, r * q:(r + 1) * q] - csp_t[hh:hh + 1, :]
            seg_b = colb[:, (hpg + r) * q:(hpg + r + 1) * q] - csp_t[SSD_HEADS + hh:SSD_HEADS + hh + 1, :]
            decay = jnp.exp(jnp.where(before, seg_f, -jnp.inf)) + jnp.exp(jnp.where(after, seg_b, -jnp.inf))
            ms.append((cb * decay).astype(BF16))
        ys = []
        for pr in range(hpg // 2):
            xp = x_g[:, 2 * pr * p:2 * (pr + 1) * p]
            rhs = jnp.concatenate([jnp.where(half, xp, 0.0), jnp.where(half, 0.0, xp)], axis=0).astype(BF16)
            ys.append(_dot(jnp.concatenate([ms[2 * pr], ms[2 * pr + 1]], axis=1), rhs))
        y_g = jnp.concatenate(ys, axis=1)
        y_g = y_g + _state_step(g, x_g, b_ref, c_ref, st_ref, w_exp, dec_exp, q - 1)
        y_g = y_g + yb_ref[:, cols].astype(F32) + x_g * dskip_ref[:, cols]
        yg = y_g * _silu(z_ref[:, cols].astype(F32))
        ms_g = jnp.mean(yg * yg, axis=-1, keepdims=True)
        o_ref[:, cols] = (yg * lax.rsqrt(ms_g + SSD_NORM_EPS) * ng_ref[:, cols]).astype(BF16)


def _head_expansion(col0):
    j = jnp.arange(2 * LANES)[:, None]
    c = jnp.arange(SSD_HEADS * SSD_HEAD_DIM)[None, :]
    return ((j % LANES - col0) == c // SSD_HEAD_DIM).astype(BF16)


def _score_expansion():
    nhd = 2 * SSD_HEADS
    row = jnp.arange(4 * nhd)[:, None]
    col = jnp.arange(nhd * SSD_CHUNK)[None, :]
    j = row % nhd
    head = j % SSD_HEADS
    blk = ((head // SSD_HEADS_PER_GROUP) * 2 + j // SSD_HEADS) * SSD_HEADS_PER_GROUP + head % SSD_HEADS_PER_GROUP
    return ((row < 3 * nhd) & (blk == col // SSD_CHUNK)).astype(BF16)


def _ssd(xc, dt_raw, alog_row, bias_row, batch, reverse, final_inputs=None):
    t = xc.shape[0]
    q = SSD_CHUNK
    nc = t // batch // q
    d_inner = SSD_HEADS * SSD_HEAD_DIM
    bcw = SSD_GROUPS * SSD_STATE

    def row(b, c):
        return b * nc + ((nc - 1 - c) if reverse else c)

    full = lambda a: pl.BlockSpec(a.shape, lambda b, c: (0,) * a.ndim)
    e64 = _head_expansion(SSD_HEADS if reverse else 0)
    in_specs = [pl.BlockSpec((q, d_inner), lambda b, c: (row(b, c), 0)),
                pl.BlockSpec((q, bcw), lambda b, c: (row(b, c), d_inner // bcw)),
                pl.BlockSpec((q, bcw), lambda b, c: (row(b, c), d_inner // bcw + 1)),
                pl.BlockSpec((q, LANES), lambda b, c: (row(b, c), 0)),
                full(alog_row), full(bias_row), full(e64)]
    args = [xc, xc, xc, dt_raw, alog_row, bias_row, e64]
    if reverse:
        body = _ssd_bwd_kernel
    else:
        body = _ssd_fwd_kernel
        big, y_bwd, dskip_row, ng_row = final_inputs
        ecs = _score_expansion()
        in_specs += [full(ecs),
                     pl.BlockSpec((q, d_inner), lambda b, c: (row(b, c), 0)),
                     pl.BlockSpec((q, d_inner), lambda b, c: (row(b, c), 0)),
                     full(dskip_row), full(ng_row)]
        args += [ecs, big, y_bwd, dskip_row, ng_row]
    return pl.pallas_call(
        body,
        grid=(batch, nc),
        in_specs=in_specs,
        out_specs=pl.BlockSpec((q, d_inner), lambda b, c: (row(b, c), 0)),
        out_shape=jax.ShapeDtypeStruct((t, d_inner), BF16),
        scratch_shapes=[pltpu.VMEM((SSD_GROUPS, SSD_STATE, SSD_HEADS_PER_GROUP * SSD_HEAD_DIM), F32)],
        compiler_params=_cparams(("arbitrary", "arbitrary")),
        name="ssd_bwd" if reverse else "ssd_fwd",
    )(*args)


def _mix_kernel(tiles_per_seq, seq, yn_ref, u_ref, up_ref, un_ref, ga_ref, gb_ref, x_ref,
                wssd_ref, pw_ref, wpo_ref, wo_ref, wr_ref,
                gateb_ref, pscale_ref, ln1g_ref, ln1b_ref, rb_ref, g1_ref, sh2_ref, sc2_ref,
                x1_ref, h2_ref, lg_ref, ext_ref):
    i = pl.program_id(0)
    tm, width = u_ref.shape
    first = (i % tiles_per_seq) == 0
    last = (i % tiles_per_seq) == tiles_per_seq - 1
    u = u_ref[...].astype(F32)
    ext_ref[0:HALO, :] = jnp.where(first, 0.0, up_ref[...].astype(F32))
    ext_ref[HALO:HALO + tm, :] = u
    ext_ref[HALO + tm:, :] = jnp.where(last, 0.0, un_ref[...].astype(F32))
    tpos = (i % tiles_per_seq) * tm + lax.broadcasted_iota(jnp.int32, (tm, 1), 0)
    gd = width // len(POOL_WINDOWS)
    mixed = []
    for gi, w in enumerate(POOL_WINDOWS):
        cols = slice(gi * gd, (gi + 1) * gd)
        s = ext_ref[pl.ds(HALO - w // 2, tm), cols]
        for k in range(1, w):
            s = s + ext_ref[pl.ds(HALO - w // 2 + k, tm), cols]
        cnt = (jnp.minimum(tpos + w // 2, seq) - jnp.maximum(tpos - w // 2, 0)).astype(F32)
        diff = s / cnt - u[:, cols]
        mixed.append(_dot(diff.astype(BF16), pw_ref[gi]))
    mixed = jnp.concatenate(mixed, axis=1) * pscale_ref[...]
    y_pool = _dot(mixed.astype(BF16), wpo_ref[...])
    y_ssd = _dot(yn_ref[...], wssd_ref[...])
    d = y_ssd.shape[1]
    g_ssd = jax.nn.sigmoid(ga_ref[...].astype(F32) + gateb_ref[:, 0:d])
    g_pool = jax.nn.sigmoid(gb_ref[...].astype(F32) + gateb_ref[:, d:2 * d])
    mix = _dot((g_ssd * y_ssd + g_pool * y_pool).astype(BF16), wo_ref[...])
    x1 = _layernorm(DEEPNORM_ALPHA * x_ref[...] + g1_ref[...] * mix) * ln1g_ref[...] + ln1b_ref[...]
    x1_ref[...] = x1
    h2 = _layernorm(x1) * (1.0 + sc2_ref[...]) + sh2_ref[...]
    h2_ref[...] = h2
    lg_ref[...] = _dot3(h2, wr_ref[...]) + rb_ref[...]


def _mix(yn, big, x2, w_ssd_out, pool_w, w_pool_out, w_o, w_router, gate_b, pool_scale, ln1_g, ln1_b,
         r_bias, g1, sh2, sc2, seq, pool_col, gate_col):
    t, d = x2.shape
    tm = TM_MIX
    tiles_per_seq = seq // tm
    hb = tm // HALO
    nhb = t // HALO
    pcb = pool_col // d
    gcb = gate_col // d
    full = lambda a: pl.BlockSpec(a.shape, lambda i: (0,) * a.ndim)
    per_batch = pl.BlockSpec((None, 1, d), lambda i: (i // tiles_per_seq, 0, 0))
    return pl.pallas_call(
        functools.partial(_mix_kernel, tiles_per_seq, seq),
        grid=(t // tm,),
        in_specs=[pl.BlockSpec((tm, yn.shape[1]), lambda i: (i, 0)),
                  pl.BlockSpec((tm, d), lambda i: (i, pcb)),
                  pl.BlockSpec((HALO, d), lambda i: (jnp.maximum(i * hb - 1, 0), pcb)),
                  pl.BlockSpec((HALO, d), lambda i: (jnp.minimum((i + 1) * hb, nhb - 1), pcb)),
                  pl.BlockSpec((tm, d), lambda i: (i, gcb)),
                  pl.BlockSpec((tm, d), lambda i: (i, gcb + 1)),
                  pl.BlockSpec((tm, d), lambda i: (i, 0)),
                  full(w_ssd_out), full(pool_w), full(w_pool_out), full(w_o), full(w_router),
                  full(gate_b), full(pool_scale), full(ln1_g), full(ln1_b), full(r_bias),
                  per_batch, per_batch, per_batch],
        out_specs=[pl.BlockSpec((tm, d), lambda i: (i, 0)),
                   pl.BlockSpec((tm, d), lambda i: (i, 0)),
                   pl.BlockSpec((tm, LANES), lambda i: (i, 0))],
        out_shape=[jax.ShapeDtypeStruct((t, d), F32),
                   jax.ShapeDtypeStruct((t, d), F32),
                   jax.ShapeDtypeStruct((t, LANES), F32)],
        scratch_shapes=[pltpu.VMEM((tm + 2 * HALO, d), F32)],
        compiler_params=_cparams(("arbitrary",)),
        name="mix_postln",
    )(yn, big, big, big, big, big, x2, w_ssd_out, pool_w, w_pool_out, w_o, w_router,
      gate_b, pool_scale, ln1_g, ln1_b, r_bias, g1, sh2, sc2)


def _route_kernel(lg_ref, rt_ref, cnt_ref, carry_ref):
    @pl.when(pl.program_id(0) == 0)
    def _():
        carry_ref[...] = jnp.zeros_like(carry_ref)

    lg = lg_ref[...]
    tm = lg.shape[0]
    lane = lax.broadcasted_iota(jnp.int32, lg.shape, 1).astype(F32)
    neg = -jnp.inf
    big_lane = float(LANES)
    gl = jnp.where(lane < MOE_GROUPS, lg, neg)
    gmax = jnp.max(gl, axis=-1, keepdims=True)
    g_w = 1.0 / jnp.sum(jnp.exp(gl - gmax), axis=-1, keepdims=True)
    g_idx = jnp.min(jnp.where(gl == gmax, lane, big_lane), axis=-1, keepdims=True)
    lo = MOE_GROUPS + MOE_EXPERTS_PER_GROUP * g_idx
    el = jnp.where((lane >= lo) & (lane < lo + MOE_EXPERTS_PER_GROUP), lg, neg)
    m1 = jnp.max(el, axis=-1, keepdims=True)
    i1 = jnp.min(jnp.where(el == m1, lane, big_lane), axis=-1, keepdims=True)
    el2 = jnp.where(lane == i1, neg, el)
    m2 = jnp.max(el2, axis=-1, keepdims=True)
    i2 = jnp.min(jnp.where(el2 == m2, lane, big_lane), axis=-1, keepdims=True)
    e = jnp.exp(m2 - m1)
    w1 = g_w / (1.0 + e)
    w2 = g_w * e / (1.0 + e)
    onehot = jnp.where((lane == i1) | (lane == i2), 1.0, 0.0)
    ri = lax.broadcasted_iota(jnp.int32, (tm, tm), 0)
    ci = lax.broadcasted_iota(jnp.int32, (tm, tm), 1)
    earlier = jnp.where(ci < ri, 1.0, 0.0).astype(BF16)
    rank = _dot(earlier, onehot.astype(BF16)) + carry_ref[...]
    r1 = jnp.sum(jnp.where(lane == i1, rank, 0.0), axis=-1, keepdims=True)
    r2 = jnp.sum(jnp.where(lane == i2, rank, 0.0), axis=-1, keepdims=True)
    carry_ref[...] = carry_ref[...] + jnp.sum(onehot, axis=0, keepdims=True)
    cnt_ref[...] = carry_ref[...]
    out = jnp.where(lane == 0, i1 - MOE_GROUPS, 0.0)
    out = jnp.where(lane == 1, i2 - MOE_GROUPS, out)
    out = jnp.where(lane == 2, r1, out)
    out = jnp.where(lane == 3, r2, out)
    out = jnp.where(lane == 4, w1, out)
    out = jnp.where(lane == 5, w2, out)
    rt_ref[...] = out


def _route(logits):
    t = logits.shape[0]
    tm = TM_ROUTE
    return pl.pallas_call(
        _route_kernel,
        grid=(t // tm,),
        in_specs=[pl.BlockSpec((tm, LANES), lambda i: (i, 0))],
        out_specs=[pl.BlockSpec((tm, LANES), lambda i: (i, 0)),
                   pl.BlockSpec((1, LANES), lambda i: (0, 0))],
        out_shape=[jax.ShapeDtypeStruct((t, LANES), F32),
                   jax.ShapeDtypeStruct((1, LANES), F32)],
        scratch_shapes=[pltpu.VMEM((1, LANES), F32)],
        compiler_params=_cparams(("arbitrary",)),
        name="route",
    )(logits)


def _row_copy(src_ref, src_row, dst_ref, dst_row, sem):
    return pltpu.make_async_copy(src_ref.at[pl.ds(src_row, 1)], dst_ref.at[pl.ds(dst_row, 1)], sem)


def _dispatch_kernel(pos_ref, h2_ref, xs_in_ref, xs_ref, sem):
    del xs_in_ref
    tm = h2_ref.shape[0]

    def start(r, carry):
        _row_copy(h2_ref, r, xs_ref, pos_ref[0, 2 * r], sem).start()
        _row_copy(h2_ref, r, xs_ref, pos_ref[0, 2 * r + 1], sem).start()
        return carry

    def wait(r, carry):
        _row_copy(h2_ref, r, xs_ref, pos_ref[0, 2 * r], sem).wait()
        _row_copy(h2_ref, r, xs_ref, pos_ref[0, 2 * r + 1], sem).wait()
        return carry

    lax.fori_loop(0, tm, start, 0, unroll=DMA_UNROLL)
    lax.fori_loop(0, tm, wait, 0, unroll=DMA_UNROLL)


def _dispatch(pos, h2, xs_zero):
    t, d = h2.shape
    tm = TM_DISPATCH
    pos3 = pos.reshape(t // tm, 1, 2 * tm)
    return pl.pallas_call(
        _dispatch_kernel,
        grid=(t // tm,),
        in_specs=[pl.BlockSpec((None, 1, 2 * tm), lambda i: (i, 0, 0), memory_space=pltpu.SMEM),
                  pl.BlockSpec((tm, d), lambda i: (i, 0)),
                  pl.BlockSpec(memory_space=pl.ANY)],
        out_specs=pl.BlockSpec(memory_space=pl.ANY),
        out_shape=jax.ShapeDtypeStruct(xs_zero.shape, xs_zero.dtype),
        scratch_shapes=[pltpu.SemaphoreType.DMA(())],
        input_output_aliases={2: 0},
        compiler_params=_cparams(("arbitrary",)),
        name="dispatch",
    )(pos3, h2, xs_zero)


def _expert_kernel(te_ref, nu_ref, xs_ref, wg_ref, wu_ref, wd_ref, y_ref, wgb_ref, wub_ref, wdb_ref):
    j = pl.program_id(0)
    used = j < nu_ref[0]
    new_expert = jnp.logical_or(j == 0, te_ref[j] != te_ref[jnp.maximum(j - 1, 0)])

    @pl.when(jnp.logical_and(used, new_expert))
    def _():
        wgb_ref[...] = wg_ref[...].astype(BF16)
        wub_ref[...] = wu_ref[...].astype(BF16)
        wdb_ref[...] = wd_ref[...].astype(BF16)

    @pl.when(used)
    def _():
        xb = xs_ref[...].astype(BF16)
        act = _silu(_dot(xb, wgb_ref[...])) * _dot(xb, wub_ref[...])
        y_ref[...] = _dot(act.astype(BF16), wdb_ref[...])

    @pl.when(jnp.logical_not(used))
    def _():
        y_ref[...] = jnp.zeros_like(y_ref)


def _experts(tile_expert, n_used, xs, w_gate, w_up, w_down):
    p_rows, d = xs.shape
    hdim = w_gate.shape[2]
    tm = TM_EXPERT
    last_used = lambda j, nu: jnp.minimum(j, nu[0] - 1)
    grid_spec = pltpu.PrefetchScalarGridSpec(
        num_scalar_prefetch=2,
        grid=(p_rows // tm,),
        in_specs=[pl.BlockSpec((tm, d), lambda j, te, nu: (last_used(j, nu), 0)),
                  pl.BlockSpec((None, d, hdim), lambda j, te, nu: (te[last_used(j, nu)], 0, 0)),
                  pl.BlockSpec((None, d, hdim), lambda j, te, nu: (te[last_used(j, nu)], 0, 0)),
                  pl.BlockSpec((None, hdim, d), lambda j, te, nu: (te[last_used(j, nu)], 0, 0))],
        out_specs=pl.BlockSpec((tm, d), lambda j, te, nu: (j, 0)),
        scratch_shapes=[pltpu.VMEM((d, hdim), BF16), pltpu.VMEM((d, hdim), BF16),
                        pltpu.VMEM((hdim, d), BF16)],
    )
    return pl.pallas_call(
        _expert_kernel,
        grid_spec=grid_spec,
        out_shape=jax.ShapeDtypeStruct((p_rows, d), F32),
        compiler_params=_cparams(("arbitrary",)),
        name="experts",
    )(tile_expert, n_used, xs, w_gate, w_up, w_down)


def _combine_kernel(pos_ref, y_ref, rt_ref, x1_ref, g2_ref, lng_ref, lnb_ref, o_ref, buf_ref, sem):
    tm = x1_ref.shape[0]

    def start(r, carry):
        _row_copy(y_ref, pos_ref[0, 2 * r], buf_ref.at[0], r, sem).start()
        _row_copy(y_ref, pos_ref[0, 2 * r + 1], buf_ref.at[1], r, sem).start()
        return carry

    def wait(r, carry):
        _row_copy(y_ref, pos_ref[0, 2 * r], buf_ref.at[0], r, sem).wait()
        _row_copy(y_ref, pos_ref[0, 2 * r + 1], buf_ref.at[1], r, sem).wait()
        return carry

    lax.fori_loop(0, tm, start, 0, unroll=DMA_UNROLL)
    lax.fori_loop(0, tm, wait, 0, unroll=DMA_UNROLL)
    rt = rt_ref[...]
    y_moe = rt[:, 4:5] * buf_ref[0] + rt[:, 5:6] * buf_ref[1]
    v = DEEPNORM_ALPHA * x1_ref[...] + g2_ref[...] * y_moe
    o_ref[...] = _layernorm(v) * lng_ref[...] + lnb_ref[...]


def _combine(pos, y_sorted, rt, x1, g2, ln_g, ln_b, seq):
    t, d = x1.shape
    tm = TM_COMBINE
    tiles_per_seq = seq // tm
    pos3 = pos.reshape(t // tm, 1, 2 * tm)
    return pl.pallas_call(
        _combine_kernel,
        grid=(t // tm,),
        in_specs=[pl.BlockSpec((None, 1, 2 * tm), lambda i: (i, 0, 0), memory_space=pltpu.SMEM),
                  pl.BlockSpec(memory_space=pl.ANY),
                  pl.BlockSpec((tm, LANES), lambda i: (i, 0)),
                  pl.BlockSpec((tm, d), lambda i: (i, 0)),
                  pl.BlockSpec((None, 1, d), lambda i: (i // tiles_per_seq, 0, 0)),
                  pl.BlockSpec((1, d), lambda i: (0, 0)),
                  pl.BlockSpec((1, d), lambda i: (0, 0))],
        out_specs=pl.BlockSpec((tm, d), lambda i: (i, 0)),
        out_shape=jax.ShapeDtypeStruct((t, d), F32),
        scratch_shapes=[pltpu.VMEM((2, tm, d), F32), pltpu.SemaphoreType.DMA(())],
        compiler_params=_cparams(("arbitrary",)),
        name="combine_postln",
    )(pos3, y_sorted, rt, x1, g2, ln_g, ln_b)


def _layer(x2, c_pad, batch, seq, w_ada, b_ada, w_in, conv_w, conv_b, a_log_f, a_log_b, dt_bias_f,
           dt_bias_b, d_skip, ssd_norm_g, w_ssd_out, pool_w, pool_scale, w_pool_out, gate_b, w_o,
           ln1_g, ln1_b, router_wg, router_bg, router_we, router_be, exp_w_gate, exp_w_up,
           exp_w_down, ln2_g, ln2_b):
    t, d = x2.shape
    d_inner = SSD_HEADS * SSD_HEAD_DIM
    conv_dim = d_inner + 2 * SSD_GROUPS * SSD_STATE
    pool_width = pool_scale.shape[0]

    mod = _ada(c_pad, w_ada, b_ada[None, :])[:batch]
    sh1, sc1, g1, sh2, sc2, g2 = [m[:, None, :] for m in jnp.split(mod, 6, axis=-1)]

    o_xbc = d_inner
    o_dt = o_xbc + conv_dim
    o_pool = o_dt + 2 * SSD_HEADS
    w_main = jnp.concatenate([w_in[:, :o_dt], w_in[:, o_pool:]], axis=1).astype(BF16)
    w_dt = jnp.pad(w_in[:, o_dt:o_pool], ((0, 0), (0, LANES - 2 * SSD_HEADS)))
    pool_col = o_dt
    gate_col = o_dt + pool_width
    big, dt_raw = _inproj(x2, sh1, sc1, w_main, w_dt, seq)

    xc = _conv(big, o_xbc, conv_w, conv_b[None, :], seq)

    pad_row = lambda f, b: jnp.pad(jnp.concatenate([f, b]), (0, LANES - 2 * SSD_HEADS))[None, :]
    alog_row = pad_row(a_log_f, a_log_b)
    bias_row = pad_row(dt_bias_f, dt_bias_b)
    dskip_row = jnp.repeat(d_skip, SSD_HEAD_DIM)[None, :]
    y_bwd = _ssd(xc, dt_raw, alog_row, bias_row, batch, reverse=True)
    yn = _ssd(xc, dt_raw, alog_row, bias_row, batch, reverse=False,
              final_inputs=(big, y_bwd, dskip_row, ssd_norm_g[None, :]))

    w_router = jnp.pad(jnp.concatenate([router_wg, router_we], axis=1),
                       ((0, 0), (0, LANES - MOE_GROUPS - MOE_EXPERTS)))
    r_bias = jnp.pad(jnp.concatenate([router_bg, router_be]), (0, LANES - MOE_GROUPS - MOE_EXPERTS))[None, :]
    x1, h2, logits = _mix(yn, big, x2, w_ssd_out.astype(BF16), pool_w.astype(BF16),
                          w_pool_out.astype(BF16), w_o.astype(BF16), w_router, gate_b[None, :],
                          pool_scale[None, :], ln1_g[None, :], ln1_b[None, :], r_bias, g1, sh2, sc2,
                          seq, pool_col, gate_col)

    rt, counts = _route(logits)

    tme = TM_EXPERT
    cnt = counts[0, MOE_GROUPS:MOE_GROUPS + MOE_EXPERTS].astype(jnp.int32)
    padded = ((cnt + tme - 1) // tme) * tme
    ends = jnp.cumsum(padded)
    off = ends - padded
    eid = rt[:, 0:2].astype(jnp.int32)
    pos = rt[:, 2:4].astype(jnp.int32) + jnp.sum(
        jnp.where(eid[:, :, None] == jnp.arange(MOE_EXPERTS, dtype=jnp.int32), off, 0), axis=-1)
    p_rows = 2 * t + MOE_EXPERTS * tme
    n_tiles = p_rows // tme
    tile_ends = ends // tme
    tile_expert = jnp.minimum(
        jnp.sum(jnp.arange(n_tiles, dtype=jnp.int32)[:, None] >= tile_ends[None, :], axis=1),
        MOE_EXPERTS - 1).astype(jnp.int32)
    n_used = tile_ends[-1:].astype(jnp.int32)

    xs = _dispatch(pos, h2, jnp.zeros((p_rows, d), F32))
    y_sorted = _experts(tile_expert, n_used, xs, exp_w_gate, exp_w_up, exp_w_down)
    return _combine(pos, y_sorted, rt, x1, g2, ln2_g[None, :], ln2_b[None, :], seq)


def kernel(x, c, w_ada, b_ada, w_in, conv_w, conv_b, a_log_f, a_log_b, dt_bias_f, dt_bias_b, d_skip,
           ssd_norm_g, w_ssd_out, pool_w, pool_scale, w_pool_out, gate_b, w_o, ln1_g, ln1_b,
           router_wg, router_bg, router_we, router_be, exp_w_gate, exp_w_up, exp_w_down, ln2_g, ln2_b):
    batch, seq, d = x.shape
    x2 = x.reshape(batch * seq, d)
    c_pad = jnp.pad(c, ((0, 8 - batch), (0, 0)))
    params = (w_ada, b_ada, w_in, conv_w, conv_b, a_log_f, a_log_b, dt_bias_f, dt_bias_b, d_skip,
              ssd_norm_g, w_ssd_out, pool_w, pool_scale, w_pool_out, gate_b, w_o, ln1_g, ln1_b,
              router_wg, router_bg, router_we, router_be, exp_w_gate, exp_w_up, exp_w_down, ln2_g, ln2_b)
    for l in range(w_ada.shape[0]):
        x2 = _layer(x2, c_pad, batch, seq, *[p[l] for p in params])
    return x2.reshape(batch, seq, d)
```

```python
import functools

import jax
import jax.numpy as jnp
from jax import lax
from jax.experimental import pallas as pl
from jax.experimental.pallas import tpu as pltpu

F32 = jnp.float32
BF16 = jnp.bfloat16
HIGHEST = lax.Precision.HIGHEST

SSD_HEAD_DIM = 64
SSD_GROUPS = 8
SSD_HEADS_PER_GROUP = 4
SSD_HEADS = SSD_GROUPS * SSD_HEADS_PER_GROUP
SSD_STATE = 128
SSD_CONV = 5
SSD_CHUNK = 128
SSD_NORM_EPS = 1e-5
POOL_WINDOWS = (2, 4, 8, 16)
MOE_GROUPS = 4
MOE_EXPERTS_PER_GROUP = 8
MOE_EXPERTS = MOE_GROUPS * MOE_EXPERTS_PER_GROUP
DEPTH = 1
DEEPNORM_ALPHA = (2.0 * DEPTH) ** 0.25
LN_EPS = 1e-5

LANES = 128
HALO = 16
VMEM_LIMIT = 48 * 1024 * 1024
SLOTMAP_UNROLL = 8

TM_INPROJ = 1024
TN_INPROJ = 2304
TM_CONV = 512
TM_MIX = 256
TM_ROUTE = 512
TM_SLOTMAP = 1024
TM_EXPERT = 256
TM_COMBINE = 512


def _dot(a, b):
    return jnp.dot(a, b, preferred_element_type=F32)


def _split_hi_lo(v):
    hi = v.astype(BF16)
    lo = (v - hi.astype(F32)).astype(BF16)
    return hi, lo


def _split3(v):
    hi = v.astype(BF16)
    r = v - hi.astype(F32)
    mid = r.astype(BF16)
    lo = (r - mid.astype(F32)).astype(BF16)
    return hi, mid, lo


def _dot3(a, b):
    a_hi, a_lo = _split_hi_lo(a)
    b_hi, b_lo = _split_hi_lo(b)
    return _dot(a_hi, b_hi) + _dot(a_lo, b_hi) + _dot(a_hi, b_lo)


def _layernorm(v):
    mu = jnp.mean(v, axis=-1, keepdims=True)
    vc = v - mu
    var = jnp.mean(vc * vc, axis=-1, keepdims=True)
    return vc * lax.rsqrt(var + LN_EPS)


def _silu(v):
    return v * jax.nn.sigmoid(v)


def _cparams(sem):
    return pltpu.CompilerParams(dimension_semantics=sem, vmem_limit_bytes=VMEM_LIMIT)


def _ada_kernel(c_ref, w_ref, b_ref, o_ref):
    o_ref[...] = jnp.dot(_silu(c_ref[...]), w_ref[...], precision=HIGHEST,
                         preferred_element_type=F32) + b_ref[...]


def _ada(c_pad, w, b):
    d, n = w.shape
    tn = 1024
    return pl.pallas_call(
        _ada_kernel,
        grid=(n // tn,),
        in_specs=[pl.BlockSpec((c_pad.shape[0], d), lambda j: (0, 0)),
                  pl.BlockSpec((d, tn), lambda j: (0, j)),
                  pl.BlockSpec((1, tn), lambda j: (0, j))],
        out_specs=pl.BlockSpec((c_pad.shape[0], tn), lambda j: (0, j)),
        out_shape=jax.ShapeDtypeStruct((c_pad.shape[0], n), F32),
        compiler_params=_cparams(("arbitrary",)),
        name="ada_mod",
    )(c_pad, w, b)


def _inproj_kernel(x_ref, sh_ref, sc_ref, w_ref, wdt_ref, o_ref, dt_ref, h_ref):
    @pl.when(pl.program_id(1) == 0)
    def _():
        h = _layernorm(x_ref[...]) * (1.0 + sc_ref[...]) + sh_ref[...]
        h_ref[...] = h.astype(BF16)
        dt_ref[...] = _dot3(h, wdt_ref[...])

    o_ref[...] = _dot(h_ref[...], w_ref[...]).astype(BF16)


def _inproj(x2, sh, sc, w_main, w_dt, seq):
    t, d = x2.shape
    n = w_main.shape[1]
    tm, tn = TM_INPROJ, TN_INPROJ
    tiles_per_seq = seq // tm
    return pl.pallas_call(
        _inproj_kernel,
        grid=(t // tm, n // tn),
        in_specs=[pl.BlockSpec((tm, d), lambda i, j: (i, 0)),
                  pl.BlockSpec((None, 1, d), lambda i, j: (i // tiles_per_seq, 0, 0)),
                  pl.BlockSpec((None, 1, d), lambda i, j: (i // tiles_per_seq, 0, 0)),
                  pl.BlockSpec((d, tn), lambda i, j: (0, j)),
                  pl.BlockSpec((d, LANES), lambda i, j: (0, 0))],
        out_specs=[pl.BlockSpec((tm, tn), lambda i, j: (i, j)),
                   pl.BlockSpec((tm, LANES), lambda i, j: (i, 0))],
        out_shape=[jax.ShapeDtypeStruct((t, n), BF16),
                   jax.ShapeDtypeStruct((t, LANES), F32)],
        scratch_shapes=[pltpu.VMEM((tm, d), BF16)],
        compiler_params=_cparams(("arbitrary", "arbitrary")),
        name="ln_inproj",
    )(x2, sh, sc, w_main, w_dt)


def _conv_kernel(tiles_per_seq, cur_ref, prev_ref, next_ref, w_ref, b_ref, o_ref, ext_ref):
    i = pl.program_id(0)
    tm = cur_ref.shape[0]
    first = (i % tiles_per_seq) == 0
    last = (i % tiles_per_seq) == tiles_per_seq - 1
    ext_ref[0:HALO, :] = jnp.where(first, 0.0, prev_ref[...].astype(F32))
    ext_ref[HALO:HALO + tm, :] = cur_ref[...].astype(F32)
    ext_ref[HALO + tm:, :] = jnp.where(last, 0.0, next_ref[...].astype(F32))
    pad = SSD_CONV // 2
    acc = b_ref[...] + w_ref[0:1, :] * ext_ref[pl.ds(HALO - pad, tm), :]
    for k in range(1, SSD_CONV):
        acc = acc + w_ref[k:k + 1, :] * ext_ref[pl.ds(HALO - pad + k, tm), :]
    o_ref[...] = _silu(acc).astype(BF16)


def _conv(big, col_off, conv_w, conv_b, seq):
    t = big.shape[0]
    cdim = conv_w.shape[1]
    tm, tc = TM_CONV, 1024
    tiles_per_seq = seq // tm
    cb0 = col_off // tc
    hb = tm // HALO
    nhb = t // HALO
    return pl.pallas_call(
        functools.partial(_conv_kernel, tiles_per_seq),
        grid=(t // tm, cdim // tc),
        in_specs=[pl.BlockSpec((tm, tc), lambda i, j: (i, cb0 + j)),
                  pl.BlockSpec((HALO, tc), lambda i, j: (jnp.maximum(i * hb - 1, 0), cb0 + j)),
                  pl.BlockSpec((HALO, tc), lambda i, j: (jnp.minimum((i + 1) * hb, nhb - 1), cb0 + j)),
                  pl.BlockSpec((SSD_CONV, tc), lambda i, j: (0, j)),
                  pl.BlockSpec((1, tc), lambda i, j: (0, j))],
        out_specs=pl.BlockSpec((tm, tc), lambda i, j: (i, j)),
        out_shape=jax.ShapeDtypeStruct((t, cdim), BF16),
        scratch_shapes=[pltpu.VMEM((tm + 2 * HALO, tc), F32)],
        compiler_params=_cparams(("arbitrary", "arbitrary")),
        name="conv_silu",
    )(big, big, big, conv_w, conv_b)


def _tri_cumsum(tri, v):
    hi, mid, lo = _split3(v)
    return _dot(tri, hi) + _dot(tri, mid) + _dot(tri, lo)


def _expand_heads(v, e_ref):
    hi, lo = _split_hi_lo(v)
    return _dot(jnp.concatenate([hi, lo], axis=1), e_ref[...])


def _state_step(g, x_g, b_ref, c_ref, st_ref, w_exp, dec_exp, end_row):
    gw = x_g.shape[1]
    cols = slice(g * gw, (g + 1) * gw)
    bg = b_ref[:, g * SSD_STATE:(g + 1) * SSD_STATE]
    cg = c_ref[:, g * SSD_STATE:(g + 1) * SSD_STATE]
    h_in = st_ref[g]
    y_off = _dot(cg, h_in.astype(BF16)) * dec_exp[:, cols]
    xw = (x_g * w_exp[:, cols]).astype(BF16)
    s_new = lax.dot_general(bg, xw, (((0,), (0,)), ((), ())), preferred_element_type=F32)
    st_ref[g] = h_in * dec_exp[end_row:end_row + 1, cols] + s_new
    return y_off


def _ssd_bwd_kernel(x_ref, b_ref, c_ref, dt_ref, alog_ref, bias_ref, e64_ref, o_ref, st_ref):
    q = x_ref.shape[0]
    gw = SSD_HEADS_PER_GROUP * SSD_HEAD_DIM

    @pl.when(pl.program_id(1) == 0)
    def _():
        st_ref[...] = jnp.zeros_like(st_ref)

    dt = jax.nn.softplus(dt_ref[...] + bias_ref[...])
    da = dt * (-jnp.exp(alog_ref[...]))
    li = lax.broadcasted_iota(jnp.int32, (q, q), 0)
    si = lax.broadcasted_iota(jnp.int32, (q, q), 1)
    cs = _tri_cumsum(jnp.where(si >= li, 1.0, 0.0).astype(BF16), da)
    w_exp = _expand_heads(dt * jnp.exp(cs[0:1, :] - cs), e64_ref)
    dec_exp = _expand_heads(jnp.exp(cs), e64_ref)
    for g in range(SSD_GROUPS):
        cols = slice(g * gw, (g + 1) * gw)
        x_g = x_ref[:, cols].astype(F32)
        o_ref[:, cols] = _state_step(g, x_g, b_ref, c_ref, st_ref, w_exp, dec_exp, 0).astype(BF16)


def _ssd_fwd_kernel(x_ref, b_ref, c_ref, dt_ref, alog_ref, bias_ref, e64_ref, ecs_ref, z_ref,
                    yb_ref, dskip_ref, ng_ref, o_ref, st_ref):
    q = x_ref.shape[0]
    p = SSD_HEAD_DIM
    hpg = SSD_HEADS_PER_GROUP
    gw = hpg * p

    @pl.when(pl.program_id(1) == 0)
    def _():
        st_ref[...] = jnp.zeros_like(st_ref)

    dt = jax.nn.softplus(dt_ref[...] + bias_ref[...])
    da = dt * (-jnp.exp(alog_ref[...]))
    li = lax.broadcasted_iota(jnp.int32, (q, q), 0)
    si = lax.broadcasted_iota(jnp.int32, (q, q), 1)
    before = si <= li
    after = si >= li
    lane = lax.broadcasted_iota(jnp.int32, (q, LANES), 1)
    tri2 = jnp.concatenate([jnp.where(before, 1.0, 0.0), jnp.where(after, 1.0, 0.0)], axis=1).astype(BF16)
    da2 = jnp.concatenate([jnp.where(lane < SSD_HEADS, da, 0.0),
                           jnp.where((lane >= SSD_HEADS) & (lane < 2 * SSD_HEADS), da, 0.0)], axis=0)
    cs = _tri_cumsum(tri2, da2)
    csp_t = (cs - jnp.log(dt)).T
    hi, mid, lo = _split3(cs)
    nhd = 2 * SSD_HEADS
    v_cs = jnp.concatenate([hi[:, :nhd].astype(F32), mid[:, :nhd].astype(F32), lo[:, :nhd].astype(F32),
                            jnp.zeros((q, nhd), F32)], axis=1).astype(BF16)
    to_end = jnp.where(lane < SSD_HEADS, cs[q - 1:q, :] - cs, 0.0)
    w_exp = _expand_heads(dt * jnp.exp(to_end), e64_ref)
    dec_exp = _expand_heads(jnp.exp(cs), e64_ref)
    half =lax.broadcasted_iota(jnp.int32, (q, 2 * p), 1) < p

    for g in range(SSD_GROUPS):
        cols = slice(g * gw, (g + 1) * gw)
        bg = b_ref[:, g * SSD_STATE:(g + 1) * SSD_STATE]
        cg = c_ref[:, g * SSD_STATE:(g + 1) * SSD_STATE]
        cb = lax.dot_general(cg, bg, (((1,), (1,)), ((), ())), preferred_element_type=F32)
        colb = _dot(v_cs, ecs_ref[:, g * 2 * hpg * q:(g + 1) * 2 * hpg * q])
        x_g = x_ref[:, cols].astype(F32)
        ms = []
        for r in range(hpg):
            hh = g * hpg + r
            seg_f = colb[:, r * q:(r + 1) * q] - csp_t[hh:hh + 1, :]
            seg_b = colb[:, (hpg + r) * q:(hpg + r + 1) * q] - csp_t[SSD_HEADS + hh:SSD_HEADS + hh + 1, :]
            decay = jnp.exp(jnp.where(before, seg_f, -jnp.inf)) + jnp.exp(jnp.where(after, seg_b, -jnp.inf))
            ms.append((cb * decay).astype(BF16))
        ys = []
        for pr in range(hpg // 2):
            xp = x_g[:, 2 * pr * p:2 * (pr + 1) * p]
            rhs = jnp.concatenate([jnp.where(half, xp, 0.0), jnp.where(half, 0.0, xp)], axis=0).astype(BF16)
            ys.append(_dot(jnp.concatenate([ms[2 * pr], ms[2 * pr + 1]], axis=1), rhs))
        y_g = jnp.concatenate(ys, axis=1)
        y_g = y_g + _state_step(g, x_g, b_ref, c_ref, st_ref, w_exp, dec_exp, q - 1)
        y_g = y_g + yb_ref[:, cols].astype(F32) + x_g * dskip_ref[:, cols]
        yg = y_g * _silu(z_ref[:, cols].astype(F32))
        ms_g = jnp.mean(yg * yg, axis=-1, keepdims=True)
        o_ref[:, cols] = (yg * lax.rsqrt(ms_g + SSD_NORM_EPS) * ng_ref[:, cols]).astype(BF16)


def _head_expansion(col0):
    j = jnp.arange(2 * LANES)[:, None]
    c = jnp.arange(SSD_HEADS * SSD_HEAD_DIM)[None, :]
    return ((j % LANES - col0) == c // SSD_HEAD_DIM).astype(BF16)


def _score_expansion():
    nhd = 2 * SSD_HEADS
    row = jnp.arange(4 * nhd)[:, None]
    col = jnp.arange(nhd * SSD_CHUNK)[None, :]
    j = row % nhd
    head = j % SSD_HEADS
    blk = ((head // SSD_HEADS_PER_GROUP) * 2 + j // SSD_HEADS) * SSD_HEADS_PER_GROUP + head % SSD_HEADS_PER_GROUP
    return ((row < 3 * nhd) & (blk == col // SSD_CHUNK)).astype(BF16)


def _ssd(xc, dt_raw, alog_row, bias_row, batch, reverse, final_inputs=None):
    t = xc.shape[0]
    q = SSD_CHUNK
    nc = t // batch // q
    d_inner = SSD_HEADS * SSD_HEAD_DIM
    bcw = SSD_GROUPS * SSD_STATE

    def row(b, c):
        return b * nc + ((nc - 1 - c) if reverse else c)

    full = lambda a: pl.BlockSpec(a.shape, lambda b, c: (0,) * a.ndim)
    e64 = _head_expansion(SSD_HEADS if reverse else 0)
    in_specs = [pl.BlockSpec((q, d_inner), lambda b, c: (row(b, c), 0)),
                pl.BlockSpec((q, bcw), lambda b, c: (row(b, c), d_inner // bcw)),
                pl.BlockSpec((q, bcw), lambda b, c: (row(b, c), d_inner // bcw + 1)),
                pl.BlockSpec((q, LANES), lambda b, c: (row(b, c), 0)),
                full(alog_row), full(bias_row), full(e64)]
    args = [xc, xc, xc, dt_raw, alog_row, bias_row, e64]
    if reverse:
        body = _ssd_bwd_kernel
    else:
        body = _ssd_fwd_kernel
        big, y_bwd, dskip_row, ng_row = final_inputs
        ecs = _score_expansion()
        in_specs += [full(ecs),
                     pl.BlockSpec((q, d_inner), lambda b, c: (row(b, c), 0)),
                     pl.BlockSpec((q, d_inner), lambda b, c: (row(b, c), 0)),
                     full(dskip_row), full(ng_row)]
        args += [ecs, big, y_bwd, dskip_row, ng_row]
    return pl.pallas_call(
        body,
        grid=(batch, nc),
        in_specs=in_specs,
        out_specs=pl.BlockSpec((q, d_inner), lambda b, c: (row(b, c), 0)),
        out_shape=jax.ShapeDtypeStruct((t, d_inner), BF16),
        scratch_shapes=[pltpu.VMEM((SSD_GROUPS, SSD_STATE, SSD_HEADS_PER_GROUP * SSD_HEAD_DIM), F32)],
        compiler_params=_cparams(("arbitrary", "arbitrary")),
        name="ssd_bwd" if reverse else "ssd_fwd",
    )(*args)


def _mix_kernel(tiles_per_seq, seq, yn_ref, u_ref, up_ref, un_ref, ga_ref, gb_ref, x_ref,
                wssd_ref, pw_ref, wpo_ref, wo_ref, wr_ref,
                gateb_ref, pscale_ref, ln1g_ref, ln1b_ref, rb_ref, g1_ref, sh2_ref, sc2_ref,
                x1_ref, h2_ref, lg_ref, ext_ref):
    i = pl.program_id(0)
    tm, width = u_ref.shape
    first = (i % tiles_per_seq) == 0
    last = (i % tiles_per_seq) == tiles_per_seq - 1
    u = u_ref[...].astype(F32)
    ext_ref[0:HALO, :] = jnp.where(first, 0.0, up_ref[...].astype(F32))
    ext_ref[HALO:HALO + tm, :] = u
    ext_ref[HALO + tm:, :] = jnp.where(last, 0.0, un_ref[...].astype(F32))
    tpos = (i % tiles_per_seq) * tm + lax.broadcasted_iota(jnp.int32, (tm, 1), 0)
    gd = width // len(POOL_WINDOWS)
    mixed = []
    for gi, w in enumerate(POOL_WINDOWS):
        cols = slice(gi * gd, (gi + 1) * gd)
        s = ext_ref[pl.ds(HALO - w // 2, tm), cols]
        for k in range(1, w):
            s = s + ext_ref[pl.ds(HALO - w // 2 + k, tm), cols]
        cnt = (jnp.minimum(tpos + w // 2, seq) - jnp.maximum(tpos - w // 2, 0)).astype(F32)
        diff = s / cnt - u[:, cols]
        mixed.append(_dot(diff.astype(BF16), pw_ref[gi]))
    mixed = jnp.concatenate(mixed, axis=1) * pscale_ref[...]
    y_pool = _dot(mixed.astype(BF16), wpo_ref[...])
    y_ssd = _dot(yn_ref[...], wssd_ref[...])
    d = y_ssd.shape[1]
    g_ssd = jax.nn.sigmoid(ga_ref[...].astype(F32) + gateb_ref[:, 0:d])
    g_pool = jax.nn.sigmoid(gb_ref[...].astype(F32) + gateb_ref[:, d:2 * d])
    mix = _dot((g_ssd * y_ssd + g_pool * y_pool).astype(BF16), wo_ref[...])
    x1 = _layernorm(DEEPNORM_ALPHA * x_ref[...] + g1_ref[...] * mix) * ln1g_ref[...] + ln1b_ref[...]
    x1_ref[...] = x1
    h2 = _layernorm(x1) * (1.0 + sc2_ref[...]) + sh2_ref[...]
    h2_ref[...] = h2
    lg_ref[...] = _dot3(h2, wr_ref[...]) + rb_ref[...]


def _mix(yn, big, x2, w_ssd_out, pool_w, w_pool_out, w_o, w_router, gate_b, pool_scale, ln1_g, ln1_b,
         r_bias, g1, sh2, sc2, seq, pool_col, gate_col):
    t, d = x2.shape
    tm = TM_MIX
    tiles_per_seq = seq // tm
    hb = tm // HALO
    nhb = t // HALO
    pcb = pool_col // d
    gcb = gate_col // d
    full = lambda a: pl.BlockSpec(a.shape, lambda i: (0,) * a.ndim)
    per_batch = pl.BlockSpec((None, 1, d), lambda i: (i // tiles_per_seq, 0, 0))
    return pl.pallas_call(
        functools.partial(_mix_kernel, tiles_per_seq, seq),
        grid=(t // tm,),
        in_specs=[pl.BlockSpec((tm, yn.shape[1]), lambda i: (i, 0)),
                  pl.BlockSpec((tm, d), lambda i: (i, pcb)),
                  pl.BlockSpec((HALO, d), lambda i: (jnp.maximum(i * hb - 1, 0), pcb)),
                  pl.BlockSpec((HALO, d), lambda i: (jnp.minimum((i + 1) * hb, nhb - 1), pcb)),
                  pl.BlockSpec((tm, d), lambda i: (i, gcb)),
                  pl.BlockSpec((tm, d), lambda i: (i, gcb + 1)),
                  pl.BlockSpec((tm, d), lambda i: (i, 0)),
                  full(w_ssd_out), full(pool_w), full(w_pool_out), full(w_o), full(w_router),
                  full(gate_b), full(pool_scale), full(ln1_g), full(ln1_b), full(r_bias),
                  per_batch, per_batch, per_batch],
        out_specs=[pl.BlockSpec((tm, d), lambda i: (i, 0)),
                   pl.BlockSpec((tm, d), lambda i: (i, 0)),
                   pl.BlockSpec((tm, LANES), lambda i: (i, 0))],
        out_shape=[jax.ShapeDtypeStruct((t, d), F32),
                   jax.ShapeDtypeStruct((t, d), F32),
                   jax.ShapeDtypeStruct((t, LANES), F32)],
        scratch_shapes=[pltpu.VMEM((tm + 2 * HALO, d), F32)],
        compiler_params=_cparams(("arbitrary",)),
        name="mix_postln",
    )(yn, big, big, big, big, big, x2, w_ssd_out, pool_w, w_pool_out, w_o, w_router,
      gate_b, pool_scale, ln1_g, ln1_b, r_bias, g1, sh2, sc2)


def _route_kernel(lg_ref, rt_ref, cnt_ref, carry_ref):
    @pl.when(pl.program_id(0) == 0)
    def _():
        carry_ref[...] = jnp.zeros_like(carry_ref)

    lg = lg_ref[...]
    tm = lg.shape[0]
    lane = lax.broadcasted_iota(jnp.int32, lg.shape, 1).astype(F32)
    neg = -jnp.inf
    big_lane = float(LANES)
    gl = jnp.where(lane < MOE_GROUPS, lg, neg)
    gmax = jnp.max(gl, axis=-1, keepdims=True)
    g_w = 1.0 / jnp.sum(jnp.exp(gl - gmax), axis=-1, keepdims=True)
    g_idx = jnp.min(jnp.where(gl == gmax, lane, big_lane), axis=-1, keepdims=True)
    lo = MOE_GROUPS + MOE_EXPERTS_PER_GROUP * g_idx
    el = jnp.where((lane >= lo) & (lane < lo + MOE_EXPERTS_PER_GROUP), lg, neg)
    m1 = jnp.max(el, axis=-1, keepdims=True)
    i1 = jnp.min(jnp.where(el == m1, lane, big_lane), axis=-1, keepdims=True)
    el2 = jnp.where(lane == i1, neg, el)
    m2 = jnp.max(el2, axis=-1, keepdims=True)
    i2 = jnp.min(jnp.where(el2 == m2, lane, big_lane), axis=-1, keepdims=True)
    e = jnp.exp(m2 - m1)
    w1 = g_w / (1.0 + e)
    w2 = g_w * e / (1.0 + e)
    onehot = jnp.where((lane == i1) | (lane == i2), 1.0, 0.0)
    ri = lax.broadcasted_iota(jnp.int32, (tm, tm), 0)
    ci = lax.broadcasted_iota(jnp.int32, (tm, tm), 1)
    earlier = jnp.where(ci < ri, 1.0, 0.0).astype(BF16)
    rank = _dot(earlier, onehot.astype(BF16)) + carry_ref[...]
    r1 = jnp.sum(jnp.where(lane == i1, rank, 0.0), axis=-1, keepdims=True)
    r2 = jnp.sum(jnp.where(lane == i2, rank, 0.0), axis=-1, keepdims=True)
    carry_ref[...] = carry_ref[...] + jnp.sum(onehot, axis=0, keepdims=True)
    cnt_ref[...] = carry_ref[...]
    out = jnp.where(lane == 0, i1 - MOE_GROUPS, 0.0)
    out = jnp.where(lane == 1, i2 - MOE_GROUPS, out)
    out = jnp.where(lane == 2, r1, out)
    out = jnp.where(lane == 3, r2, out)
    out = jnp.where(lane == 4, w1, out)
    out = jnp.where(lane == 5, w2, out)
    rt_ref[...] = out


def _route(logits):
    t = logits.shape[0]
    tm = TM_ROUTE
    return pl.pallas_call(
        _route_kernel,
        grid=(t // tm,),
        in_specs=[pl.BlockSpec((tm, LANES), lambda i: (i, 0))],
        out_specs=[pl.BlockSpec((tm, LANES), lambda i: (i, 0)),
                   pl.BlockSpec((1, LANES), lambda i: (0, 0))],
        out_shape=[jax.ShapeDtypeStruct((t, LANES), F32),
                   jax.ShapeDtypeStruct((1, LANES), F32)],
        scratch_shapes=[pltpu.VMEM((1, LANES), F32)],
        compiler_params=_cparams(("arbitrary",)),
        name="route",
    )(logits)


def _slotmap_kernel(n_tokens, pos_ref, init_ref, inv_ref, sem):
    i = pl.program_id(0)
    tm = pos_ref.shape[1] // 2

    @pl.when(i == 0)
    def _():
        cp = pltpu.make_async_copy(init_ref, inv_ref, sem)
        cp.start()
        cp.wait()

    def body(r, carry):
        tok = i * tm + r
        inv_ref[pos_ref[0, 2 * r]] = tok
        inv_ref[pos_ref[0, 2 * r + 1]] = n_tokens + tok
        return carry

    lax.fori_loop(0, tm, body, 0, unroll=SLOTMAP_UNROLL)


def _slotmap(pos, init):
    t = pos.shape[0]
    tm = TM_SLOTMAP
    pos3 = pos.reshape(t // tm, 1, 2 * tm)
    return pl.pallas_call(
        functools.partial(_slotmap_kernel, t),
        grid=(t // tm,),
        in_specs=[pl.BlockSpec((None, 1, 2 * tm), lambda i: (i, 0, 0), memory_space=pltpu.SMEM),
                  pl.BlockSpec(memory_space=pl.ANY)],
        out_specs=pl.BlockSpec(memory_space=pltpu.SMEM),
        out_shape=jax.ShapeDtypeStruct(init.shape, jnp.int32),
        scratch_shapes=[pltpu.SemaphoreType.DMA(())],
        compiler_params=_cparams(("arbitrary",)),
        name="slotmap",
    )(pos3, init)


def _row_copy(src_ref, src_row, dst_ref, dst_row, sem):
    return pltpu.make_async_copy(src_ref.at[pl.ds(src_row, 1)], dst_ref.at[pl.ds(dst_row, 1)], sem)


def _expert_kernel(n_tokens, te_ref, nu_ref, inv_prev_ref, inv_nxt_ref, h2_ref, wg_ref, wu_ref, wd_ref,
                   y2_ref, xbuf_ref, ybuf_ref, wgb_ref, wub_ref, wdb_ref, gsem, ssem):
    j = pl.program_id(0)
    n_used = nu_ref[0]
    tm = ybuf_ref.shape[1]
    slot = j % 2
    tok_mask = n_tokens - 1

    def gather_start(inv_ref, s):
        for r in range(tm):
            _row_copy(h2_ref, inv_ref[0, r] & tok_mask, xbuf_ref.at[s], r, gsem.at[s]).start()

    def gather_wait(s):
        pltpu.make_async_copy(h2_ref.at[pl.ds(0, tm)], xbuf_ref.at[s], gsem.at[s]).wait()

    def scatter_start(s):
        for r in range(tm):
            _row_copy(ybuf_ref.at[s], r, y2_ref, inv_prev_ref[0, r], ssem.at[s]).start()

    def scatter_wait(s):
        pltpu.make_async_copy(ybuf_ref.at[s], y2_ref.at[pl.ds(0, tm)], ssem.at[s]).wait()

    def spare_fill(k):
        return pltpu.make_async_copy(ybuf_ref.at[1], y2_ref.at[pl.ds(2 * n_tokens + k * tm, tm)], ssem.at[1])

    @pl.when(j == 0)
    def _():
        gather_start(inv_prev_ref, 0)
        ybuf_ref[1] = jnp.zeros(ybuf_ref.shape[1:], ybuf_ref.dtype)
        n_spare = (y2_ref.shape[0] - 2 * n_tokens) // tm
        for k in range(n_spare):
            spare_fill(k).start()
        for k in range(n_spare):
            spare_fill(k).wait()

    new_expert = jnp.logical_or(j == 0, te_ref[j] != te_ref[jnp.maximum(j - 1, 0)])

    @pl.when(jnp.logical_and(j < n_used, new_expert))
    def _():
        wgb_ref[...] = wg_ref[...].astype(BF16)
        wub_ref[...] = wu_ref[...].astype(BF16)
        wdb_ref[...] = wd_ref[...].astype(BF16)

    def tile_step(tiles_before):
        gather_wait(slot)
        xb = xbuf_ref[slot].astype(BF16)
        gather_start(inv_nxt_ref, 1 - slot)
        if tiles_before >= 1:
            scatter_start(1 - slot)
        act = _silu(_dot(xb, wgb_ref[...])) * _dot(xb, wub_ref[...])
        y = _dot(act.astype(BF16), wdb_ref[...])
        if tiles_before >= 2:
            scatter_wait(slot)
        ybuf_ref[slot] = y

    for before in (0, 1):
        pl.when(jnp.logical_and(j < n_used, j == before))(functools.partial(tile_step, before))
    pl.when(jnp.logical_and(j < n_used, j >= 2))(functools.partial(tile_step, 2))

    @pl.when(j == n_used)
    def _():
        gather_wait(slot)
        scatter_start(1 - slot)
        scatter_wait(slot)
        scatter_wait(1 - slot)


def _experts(tile_expert, n_used, inv, h2, w_gate, w_up, w_down, y2_rows):
    t, d = h2.shape
    hdim = w_gate.shape[2]
    tm = TM_EXPERT
    n_tiles = inv.shape[0] // tm
    inv3 = inv.reshape(n_tiles, 1, tm)
    last_used = lambda j, nu: jnp.minimum(j, nu[0] - 1)
    grid_spec = pltpu.PrefetchScalarGridSpec(
        num_scalar_prefetch=2,
        grid=(n_tiles + 1,),
        in_specs=[pl.BlockSpec((None, 1, tm), lambda j, te, nu: (jnp.maximum(j - 1, 0), 0, 0),
                               memory_space=pltpu.SMEM),
                  pl.BlockSpec((None, 1, tm), lambda j, te, nu: (jnp.minimum(j + 1, n_tiles - 1), 0, 0),
                               memory_space=pltpu.SMEM),
                  pl.BlockSpec(memory_space=pl.ANY),
                  pl.BlockSpec((None, d, hdim), lambda j, te, nu: (te[last_used(j, nu)], 0, 0)),
                  pl.BlockSpec((None, d, hdim), lambda j, te, nu: (te[last_used(j, nu)], 0, 0)),
                  pl.BlockSpec((None, hdim, d), lambda j, te, nu: (te[last_used(j, nu)], 0, 0))],
        out_specs=pl.BlockSpec(memory_space=pl.ANY),
        scratch_shapes=[pltpu.VMEM((2, tm, d), F32), pltpu.VMEM((2, tm, d), F32),
                        pltpu.VMEM((d, hdim), BF16), pltpu.VMEM((d, hdim), BF16),
                        pltpu.VMEM((hdim, d), BF16),
                        pltpu.SemaphoreType.DMA((2,)), pltpu.SemaphoreType.DMA((2,))],
    )
    return pl.pallas_call(
        functools.partial(_expert_kernel, t),
        grid_spec=grid_spec,
        out_shape=jax.ShapeDtypeStruct((y2_rows, d), F32),
        compiler_params=_cparams(("arbitrary",)),
        name="experts",
    )(tile_expert, n_used, inv3, inv3, h2, w_gate, w_up, w_down)


def _combine_kernel(y0_ref, y1_ref, rt_ref, x1_ref, g2_ref, lng_ref, lnb_ref, o_ref):
    rt = rt_ref[...]
    y_moe = rt[:, 4:5] * y0_ref[...] + rt[:, 5:6] * y1_ref[...]
    v = DEEPNORM_ALPHA * x1_ref[...] + g2_ref[...] * y_moe
    o_ref[...] = _layernorm(v) * lng_ref[...] + lnb_ref[...]


def _combine(y2, rt, x1, g2, ln_g, ln_b, seq):
    t, d = x1.shape
    tm = TM_COMBINE
    tiles_per_seq = seq // tm
    nt = t // tm
    return pl.pallas_call(
        _combine_kernel,
        grid=(nt,),
        in_specs=[pl.BlockSpec((tm, d), lambda i: (i, 0)),
                  pl.BlockSpec((tm, d), lambda i: (nt + i, 0)),
                  pl.BlockSpec((tm, LANES), lambda i: (i, 0)),
                  pl.BlockSpec((tm, d), lambda i: (i, 0)),
                  pl.BlockSpec((None, 1, d), lambda i: (i // tiles_per_seq, 0, 0)),
                  pl.BlockSpec((1, d), lambda i: (0, 0)),
                  pl.BlockSpec((1, d), lambda i: (0, 0))],
        out_specs=pl.BlockSpec((tm, d), lambda i: (i, 0)),
        out_shape=jax.ShapeDtypeStruct((t, d), F32),
        compiler_params=_cparams(("arbitrary",)),
        name="combine_postln",
    )(y2, y2, rt, x1, g2, ln_g, ln_b)


def _layer(x2, c_pad, batch, seq, w_ada, b_ada, w_in, conv_w, conv_b, a_log_f, a_log_b, dt_bias_f,
           dt_bias_b, d_skip, ssd_norm_g, w_ssd_out, pool_w, pool_scale, w_pool_out, gate_b, w_o,
           ln1_g, ln1_b, router_wg, router_bg, router_we, router_be, exp_w_gate, exp_w_up,
           exp_w_down, ln2_g, ln2_b):
    t, d = x2.shape
    assert t & (t - 1) == 0, "the slot map packs (slot, token) as slot * tokens + token"
    assert 2 * t >= 2 * TM_EXPERT, "the expert pipeline assumes at least two used tiles"
    d_inner = SSD_HEADS * SSD_HEAD_DIM
    conv_dim = d_inner + 2 * SSD_GROUPS * SSD_STATE
    pool_width = pool_scale.shape[0]

    mod = _ada(c_pad, w_ada, b_ada[None, :])[:batch]
    sh1, sc1, g1, sh2, sc2, g2 = [m[:, None, :] for m in jnp.split(mod, 6, axis=-1)]

    o_xbc = d_inner
    o_dt = o_xbc + conv_dim
    o_pool = o_dt + 2 * SSD_HEADS
    w_main = jnp.concatenate([w_in[:, :o_dt], w_in[:, o_pool:]], axis=1).astype(BF16)
    w_dt = jnp.pad(w_in[:, o_dt:o_pool], ((0, 0), (0, LANES - 2 * SSD_HEADS)))
    pool_col = o_dt
    gate_col = o_dt + pool_width
    big, dt_raw = _inproj(x2, sh1, sc1, w_main, w_dt, seq)

    xc = _conv(big, o_xbc, conv_w, conv_b[None, :], seq)

    pad_row = lambda f, b: jnp.pad(jnp.concatenate([f, b]), (0, LANES - 2 * SSD_HEADS))[None, :]
    alog_row = pad_row(a_log_f, a_log_b)
    bias_row = pad_row(dt_bias_f, dt_bias_b)
    dskip_row = jnp.repeat(d_skip, SSD_HEAD_DIM)[None, :]
    y_bwd = _ssd(xc, dt_raw, alog_row, bias_row, batch, reverse=True)
    yn = _ssd(xc, dt_raw, alog_row, bias_row, batch, reverse=False,
              final_inputs=(big, y_bwd, dskip_row, ssd_norm_g[None, :]))

    w_router = jnp.pad(jnp.concatenate([router_wg, router_we], axis=1),
                       ((0, 0), (0, LANES - MOE_GROUPS - MOE_EXPERTS)))
    r_bias = jnp.pad(jnp.concatenate([router_bg, router_be]), (0, LANES - MOE_GROUPS - MOE_EXPERTS))[None, :]
    x1, h2, logits = _mix(yn, big, x2, w_ssd_out.astype(BF16), pool_w.astype(BF16),
                          w_pool_out.astype(BF16), w_o.astype(BF16), w_router, gate_b[None, :],
                          pool_scale[None, :], ln1_g[None, :], ln1_b[None, :], r_bias, g1, sh2, sc2,
                          seq, pool_col, gate_col)

    rt, counts = _route(logits)

    tme = TM_EXPERT
    cnt = counts[0, MOE_GROUPS:MOE_GROUPS + MOE_EXPERTS].astype(jnp.int32)
    padded = ((cnt + tme - 1) // tme) * tme
    ends = jnp.cumsum(padded)
    off = ends - padded
    eid = rt[:, 0:2].astype(jnp.int32)
    pos = rt[:, 2:4].astype(jnp.int32) + jnp.sum(
        jnp.where(eid[:, :, None] == jnp.arange(MOE_EXPERTS, dtype=jnp.int32), off, 0), axis=-1)
    p_rows = 2 * t + MOE_EXPERTS * tme
    n_tiles = p_rows // tme
    tile_ends = ends // tme
    tile_expert = jnp.minimum(
        jnp.sum(jnp.arange(n_tiles + 1, dtype=jnp.int32)[:, None] >= tile_ends[None, :], axis=1),
        MOE_EXPERTS - 1).astype(jnp.int32)
    n_used = tile_ends[-1:].astype(jnp.int32)

    rows = jnp.arange(p_rows, dtype=jnp.int32)
    real_before = jnp.sum(jnp.where(rows[:, None] >= off[None, :], cnt[None, :], 0), axis=1)
    inv = _slotmap(pos, 2 * t + rows - real_before)
    y2 = _experts(tile_expert, n_used, inv, h2, exp_w_gate, exp_w_up, exp_w_down, p_rows)
    return _combine(y2, rt, x1, g2, ln2_g[None, :], ln2_b[None, :], seq)


def kernel(x, c, w_ada, b_ada, w_in, conv_w, conv_b, a_log_f, a_log_b, dt_bias_f, dt_bias_b, d_skip,
           ssd_norm_g, w_ssd_out, pool_w, pool_scale, w_pool_out, gate_b, w_o, ln1_g, ln1_b,
           router_wg, router_bg, router_we, router_be, exp_w_gate, exp_w_up, exp_w_down, ln2_g, ln2_b):
    batch, seq, d = x.shape
    x2 = x.reshape(batch * seq, d)
    c_pad = jnp.pad(c, ((0, 8 - batch), (0, 0)))
    params = (w_ada, b_ada, w_in, conv_w, conv_b, a_log_f, a_log_b, dt_bias_f, dt_bias_b, d_skip,
              ssd_norm_g, w_ssd_out, pool_w, pool_scale, w_pool_out, gate_b, w_o, ln1_g, ln1_b,
              router_wg, router_bg, router_we, router_be, exp_w_gate, exp_w_up, exp_w_down, ln2_g, ln2_b)
    for l in range(w_ada.shape[0]):
        x2 = _layer(x2, c_pad, batch, seq, *[p[l] for p in params])
    return x2.reshape(batch, seq, d)
```

```python
import functools

import jax
import jax.numpy as jnp
from jax import lax
from jax.experimental import pallas as pl
from jax.experimental.pallas import tpu as pltpu

F32 = jnp.float32
BF16 = jnp.bfloat16
HIGHEST = lax.Precision.HIGHEST

SSD_HEAD_DIM = 64
SSD_GROUPS = 8
SSD_HEADS_PER_GROUP = 4
SSD_HEADS = SSD_GROUPS * SSD_HEADS_PER_GROUP
SSD_STATE = 128
SSD_CONV = 5
SSD_CHUNK = 128
SSD_NORM_EPS = 1e-5
POOL_WINDOWS = (2, 4, 8, 16)
MOE_GROUPS = 4
MOE_EXPERTS_PER_GROUP = 8
MOE_EXPERTS = MOE_GROUPS * MOE_EXPERTS_PER_GROUP
DEPTH = 1
DEEPNORM_ALPHA = (2.0 * DEPTH) ** 0.25
LN_EPS = 1e-5

LANES = 128
HALO = 16
VMEM_LIMIT = 48 * 1024 * 1024
SLOTMAP_UNROLL = 8
DMA_PRIORITIES = 2

TM_INPROJ = 1024
TN_INPROJ = 2304
TM_CONV = 512
CONV_SUB = 128
TM_MIX = 256
POOL_SUB = 128
TM_ROUTE = 512
TM_SLOTMAP = 1024
TM_EXPERT = 256
TM_COMBINE = 512


def _dot(a, b):
    return jnp.dot(a, b, preferred_element_type=F32)


def _split_hi_lo(v):
    hi = v.astype(BF16)
    lo = (v - hi.astype(F32)).astype(BF16)
    return hi, lo


def _split3(v):
    hi = v.astype(BF16)
    r = v - hi.astype(F32)
    mid = r.astype(BF16)
    lo = (r - mid.astype(F32)).astype(BF16)
    return hi, mid, lo


def _dot3(a, b):
    a_hi, a_lo = _split_hi_lo(a)
    b_hi, b_lo = _split_hi_lo(b)
    return _dot(a_hi, b_hi) + _dot(a_lo, b_hi) + _dot(a_hi, b_lo)


def _layernorm(v):
    mu = jnp.mean(v, axis=-1, keepdims=True)
    vc = v - mu
    var = jnp.mean(vc * vc, axis=-1, keepdims=True)
    return vc * lax.rsqrt(var + LN_EPS)


def _silu(v):
    return v * jax.nn.sigmoid(v)


def _cparams(sem):
    return pltpu.CompilerParams(dimension_semantics=sem, vmem_limit_bytes=VMEM_LIMIT)


def _ada_kernel(c_ref, w_ref, b_ref, o_ref):
    o_ref[...] = jnp.dot(_silu(c_ref[...]), w_ref[...], precision=HIGHEST,
                         preferred_element_type=F32) + b_ref[...]


def _ada(c_pad, w, b):
    d, n = w.shape
    tn = 1024
    return pl.pallas_call(
        _ada_kernel,
        grid=(n // tn,),
        in_specs=[pl.BlockSpec((c_pad.shape[0], d), lambda j: (0, 0)),
                  pl.BlockSpec((d, tn), lambda j: (0, j)),
                  pl.BlockSpec((1, tn), lambda j: (0, j))],
        out_specs=pl.BlockSpec((c_pad.shape[0], tn), lambda j: (0, j)),
        out_shape=jax.ShapeDtypeStruct((c_pad.shape[0], n), F32),
        compiler_params=_cparams(("arbitrary",)),
        name="ada_mod",
    )(c_pad, w, b)


def _inproj_kernel(x_ref, sh_ref, sc_ref, w_ref, wdt_ref, o_ref, dt_ref, h_ref):
    @pl.when(pl.program_id(1) == 0)
    def _():
        h = _layernorm(x_ref[...]) * (1.0 + sc_ref[...]) + sh_ref[...]
        h_ref[...] = h.astype(BF16)
        dt_ref[...] = _dot3(h, wdt_ref[...])

    o_ref[...] = _dot(h_ref[...], w_ref[...]).astype(BF16)


def _inproj(x2, sh, sc, w_main, w_dt, seq):
    t, d = x2.shape
    n = w_main.shape[1]
    tm, tn = TM_INPROJ, TN_INPROJ
    tiles_per_seq = seq // tm
    return pl.pallas_call(
        _inproj_kernel,
        grid=(t // tm, n // tn),
        in_specs=[pl.BlockSpec((tm, d), lambda i, j: (i, 0)),
                  pl.BlockSpec((None, 1, d), lambda i, j: (i // tiles_per_seq, 0, 0)),
                  pl.BlockSpec((None, 1, d), lambda i, j: (i // tiles_per_seq, 0, 0)),
                  pl.BlockSpec((d, tn), lambda i, j: (0, j)),
                  pl.BlockSpec((d, LANES), lambda i, j: (0, 0))],
        out_specs=[pl.BlockSpec((tm, tn), lambda i, j: (i, j)),
                   pl.BlockSpec((tm, LANES), lambda i, j: (i, 0))],
        out_shape=[jax.ShapeDtypeStruct((t, n), BF16),
                   jax.ShapeDtypeStruct((t, LANES), F32)],
        scratch_shapes=[pltpu.VMEM((tm, d), BF16)],
        compiler_params=_cparams(("arbitrary", "arbitrary")),
        name="ln_inproj",
    )(x2, sh, sc, w_main, w_dt)


def _conv_kernel(tiles_per_seq, cur_ref, prev_ref, next_ref, w_ref, b_ref, o_ref, ext_ref):
    i = pl.program_id(0)
    tm = cur_ref.shape[0]
    first = (i % tiles_per_seq) == 0
    last = (i % tiles_per_seq) == tiles_per_seq - 1
    ext_ref[0:HALO, :] = jnp.where(first, jnp.zeros_like(prev_ref), prev_ref[...])
    ext_ref[HALO:HALO + tm, :] = cur_ref[...]
    ext_ref[HALO + tm:, :] = jnp.where(last, jnp.zeros_like(next_ref), next_ref[...])
    pad = SSD_CONV // 2
    sub = CONV_SUB
    win = sub + 2 * HALO
    ri = lax.broadcasted_iota(jnp.int32, (sub, win), 0)
    ci = lax.broadcasted_iota(jnp.int32, (sub, win), 1)
    picks = [jnp.where(ci == ri + HALO + k - pad, 1.0, 0.0).astype(BF16) for k in range(SSD_CONV)]
    for m in range(tm // sub):
        window = ext_ref[m * sub:m * sub + win, :]
        acc = b_ref[...] + w_ref[pad:pad + 1, :] * ext_ref[HALO + m * sub:HALO + (m + 1) * sub, :].astype(F32)
        for k in range(SSD_CONV):
            if k != pad:
                acc = acc + w_ref[k:k + 1, :] * _dot(picks[k], window)
        o_ref[m * sub:(m + 1) * sub, :] = _silu(acc).astype(BF16)


def _conv(big, col_off, conv_w, conv_b, seq):
    t = big.shape[0]
    cdim = conv_w.shape[1]
    tm, tc = TM_CONV, 1024
    tiles_per_seq = seq // tm
    cb0 = col_off // tc
    hb = tm // HALO
    nhb = t // HALO
    return pl.pallas_call(
        functools.partial(_conv_kernel, tiles_per_seq),
        grid=(t // tm, cdim // tc),
        in_specs=[pl.BlockSpec((tm, tc), lambda i, j: (i, cb0 + j)),
                  pl.BlockSpec((HALO, tc), lambda i, j: (jnp.maximum(i * hb - 1, 0), cb0 + j)),
                  pl.BlockSpec((HALO, tc), lambda i, j: (jnp.minimum((i + 1) * hb, nhb - 1), cb0 + j)),
                  pl.BlockSpec((SSD_CONV, tc), lambda i, j: (0, j)),
                  pl.BlockSpec((1, tc), lambda i, j: (0, j))],
        out_specs=pl.BlockSpec((tm, tc), lambda i, j: (i, j)),
        out_shape=jax.ShapeDtypeStruct((t, cdim), BF16),
        scratch_shapes=[pltpu.VMEM((tm + 2 * HALO, tc), BF16)],
        compiler_params=_cparams(("arbitrary", "arbitrary")),
        name="conv_silu",
    )(big, big, big, conv_w, conv_b)


def _tri_cumsum(tri, v):
    hi, mid, lo = _split3(v)
    return _dot(tri, hi) + _dot(tri, mid) + _dot(tri, lo)


def _expand_heads(v, e_ref):
    hi, lo = _split_hi_lo(v)
    return _dot(jnp.concatenate([hi, lo], axis=1), e_ref[...])


def _state_step(g, x_g, b_ref, c_ref, st_ref, w_exp, dec_exp, end_row):
    gw = x_g.shape[1]
    cols = slice(g * gw, (g + 1) * gw)
    bg = b_ref[:, g * SSD_STATE:(g + 1) * SSD_STATE]
    cg = c_ref[:, g * SSD_STATE:(g + 1) * SSD_STATE]
    h_in = st_ref[g]
    y_off = _dot(cg, h_in.astype(BF16)) * dec_exp[:, cols]
    xw = (x_g * w_exp[:, cols]).astype(BF16)
    s_new = lax.dot_general(bg, xw, (((0,), (0,)), ((), ())), preferred_element_type=F32)
    st_ref[g] = h_in * dec_exp[end_row:end_row + 1, cols] + s_new
    return y_off


def _ssd_bwd_kernel(x_ref, b_ref, c_ref, dt_ref, alog_ref, bias_ref, e64_ref, o_ref, st_ref):
    q = x_ref.shape[0]
    gw = SSD_HEADS_PER_GROUP * SSD_HEAD_DIM

    @pl.when(pl.program_id(1) == 0)
    def _():
        st_ref[...] = jnp.zeros_like(st_ref)

    dt = jax.nn.softplus(dt_ref[...] + bias_ref[...])
    da = dt * (-jnp.exp(alog_ref[...]))
    li = lax.broadcasted_iota(jnp.int32, (q, q), 0)
    si = lax.broadcasted_iota(jnp.int32, (q, q), 1)
    cs = _tri_cumsum(jnp.where(si >= li, 1.0, 0.0).astype(BF16), da)
    w_exp = _expand_heads(dt * jnp.exp(cs[0:1, :] - cs), e64_ref)
    dec_exp = _expand_heads(jnp.exp(cs), e64_ref)
    for g in range(SSD_GROUPS):
        cols = slice(g * gw, (g + 1) * gw)
        x_g = x_ref[:, cols].astype(F32)
        o_ref[:, cols] = _state_step(g, x_g, b_ref, c_ref, st_ref, w_exp, dec_exp, 0).astype(BF16)


def _ssd_fwd_kernel(x_ref, b_ref, c_ref, dt_ref, alog_ref, bias_ref, e64_ref, ecs_ref, z_ref,
                    yb_ref, dskip_ref, ng_ref, o_ref, st_ref):
    q = x_ref.shape[0]
    p = SSD_HEAD_DIM
    hpg = SSD_HEADS_PER_GROUP
    gw = hpg * p

    @pl.when(pl.program_id(1) == 0)
    def _():
        st_ref[...] = jnp.zeros_like(st_ref)

    dt = jax.nn.softplus(dt_ref[...] + bias_ref[...])
    da = dt * (-jnp.exp(alog_ref[...]))
    li = lax.broadcasted_iota(jnp.int32, (q, q), 0)
    si = lax.broadcasted_iota(jnp.int32, (q, q), 1)
    before = si <= li
    after = si >= li
    lane = lax.broadcasted_iota(jnp.int32, (q, LANES), 1)
    tri2 = jnp.concatenate([jnp.where(before, 1.0, 0.0), jnp.where(after, 1.0, 0.0)], axis=1).astype(BF16)
    da2 = jnp.concatenate([jnp.where(lane < SSD_HEADS, da, 0.0),
                           jnp.where((lane >= SSD_HEADS) & (lane < 2 * SSD_HEADS), da, 0.0)], axis=0)
    cs = _tri_cumsum(tri2, da2)
    csp_t = (cs - jnp.log(dt)).T
    hi, mid, lo = _split3(cs)
    nhd = 2 * SSD_HEADS
    v_cs = jnp.concatenate([hi[:, :nhd].astype(F32), mid[:, :nhd].astype(F32), lo[:, :nhd].astype(F32),
                            jnp.zeros((q, nhd), F32)], axis=1).astype(BF16)
    to_end = jnp.where(lane < SSD_HEADS, cs[q - 1:q, :] - cs, 0.0)
    w_exp = _expand_heads(dt * jnp.exp(to_end), e64_ref)
    dec_exp = _expand_heads(jnp.exp(cs), e64_ref)
    half =lax.broadcasted_iota(jnp.int32, (q, 2 * p), 1) < p

    for g in range(SSD_GROUPS):
        cols = slice(g * gw, (g + 1) * gw)
        bg = b_ref[:, g * SSD_STATE:(g + 1) * SSD_STATE]
        cg = c_ref[:, g * SSD_STATE:(g + 1) * SSD_STATE]
        cb = lax.dot_general(cg, bg, (((1,), (1,)), ((), ())), preferred_element_type=F32)
        colb = _dot(v_cs, ecs_ref[:, g * 2 * hpg * q:(g + 1) * 2 * hpg * q])
        x_g = x_ref[:, cols].astype(F32)
        ms = []
        for r in range(hpg):
            hh = g * hpg + r
            seg_f = colb[:, r * q:(r + 1) * q] - csp_t[hh:hh + 1, :]
            seg_b = colb[:, (hpg + r) * q:(hpg + r + 1) * q] - csp_t[SSD_HEADS + hh:SSD_HEADS + hh + 1, :]
            decay = jnp.exp(jnp.where(before, seg_f, -jnp.inf)) + jnp.exp(jnp.where(after, seg_b, -jnp.inf))
            ms.append((cb * decay).astype(BF16))
        ys = []
        for pr in range(hpg // 2):
            xp = x_g[:, 2 * pr * p:2 * (pr + 1) * p]
            rhs = jnp.concatenate([jnp.where(half, xp, 0.0), jnp.where(half, 0.0, xp)], axis=0).astype(BF16)
            ys.append(_dot(jnp.concatenate([ms[2 * pr], ms[2 * pr + 1]], axis=1), rhs))
        y_g = jnp.concatenate(ys, axis=1)
        y_g = y_g + _state_step(g, x_g, b_ref, c_ref, st_ref, w_exp, dec_exp, q - 1)
        y_g = y_g + yb_ref[:, cols].astype(F32) + x_g * dskip_ref[:, cols]
        yg = y_g * _silu(z_ref[:, cols].astype(F32))
        ms_g = jnp.mean(yg * yg, axis=-1, keepdims=True)
        o_ref[:, cols] = (yg * lax.rsqrt(ms_g + SSD_NORM_EPS) * ng_ref[:, cols]).astype(BF16)


def _head_expansion(col0):
    j = jnp.arange(2 * LANES)[:, None]
    c = jnp.arange(SSD_HEADS * SSD_HEAD_DIM)[None, :]
    return ((j % LANES - col0) == c // SSD_HEAD_DIM).astype(BF16)


def _score_expansion():
    nhd = 2 * SSD_HEADS
    row = jnp.arange(4 * nhd)[:, None]
    col = jnp.arange(nhd * SSD_CHUNK)[None, :]
    j = row % nhd
    head = j % SSD_HEADS
    blk = ((head // SSD_HEADS_PER_GROUP) * 2 + j // SSD_HEADS) * SSD_HEADS_PER_GROUP + head % SSD_HEADS_PER_GROUP
    return ((row < 3 * nhd) & (blk == col // SSD_CHUNK)).astype(BF16)


def _ssd(xc, dt_raw, alog_row, bias_row, batch, reverse, final_inputs=None):
    t = xc.shape[0]
    q = SSD_CHUNK
    nc = t // batch // q
    d_inner = SSD_HEADS * SSD_HEAD_DIM
    bcw = SSD_GROUPS * SSD_STATE

    def row(b, c):
        return b * nc + ((nc - 1 - c) if reverse else c)

    full = lambda a: pl.BlockSpec(a.shape, lambda b, c: (0,) * a.ndim)
    e64 = _head_expansion(SSD_HEADS if reverse else 0)
    in_specs = [pl.BlockSpec((q, d_inner), lambda b, c: (row(b, c), 0)),
                pl.BlockSpec((q, bcw), lambda b, c: (row(b, c), d_inner // bcw)),
                pl.BlockSpec((q, bcw), lambda b, c: (row(b, c), d_inner // bcw + 1)),
                pl.BlockSpec((q, LANES), lambda b, c: (row(b, c), 0)),
                full(alog_row), full(bias_row), full(e64)]
    args = [xc, xc, xc, dt_raw, alog_row, bias_row, e64]
    if reverse:
        body = _ssd_bwd_kernel
    else:
        body = _ssd_fwd_kernel
        big, y_bwd, dskip_row, ng_row = final_inputs
        ecs = _score_expansion()
        in_specs += [full(ecs),
                     pl.BlockSpec((q, d_inner), lambda b, c: (row(b, c), 0)),
                     pl.BlockSpec((q, d_inner), lambda b, c: (row(b, c), 0)),
                     full(dskip_row), full(ng_row)]
        args += [ecs, big, y_bwd, dskip_row, ng_row]
    return pl.pallas_call(
        body,
        grid=(batch, nc),
        in_specs=in_specs,
        out_specs=pl.BlockSpec((q, d_inner), lambda b, c: (row(b, c), 0)),
        out_shape=jax.ShapeDtypeStruct((t, d_inner), BF16),
        scratch_shapes=[pltpu.VMEM((SSD_GROUPS, SSD_STATE, SSD_HEADS_PER_GROUP * SSD_HEAD_DIM), F32)],
        compiler_params=_cparams(("arbitrary", "arbitrary")),
        name="ssd_bwd" if reverse else "ssd_fwd",
    )(*args)


def _mix_kernel(tiles_per_seq, seq, yn_ref, u_ref, up_ref, un_ref, ga_ref, gb_ref, x_ref,
                wssd_ref, pw_ref, wpo_ref, wo_ref, wr_ref,
                gateb_ref, pscale_ref, ln1g_ref, ln1b_ref, rb_ref, g1_ref, sh2_ref, sc2_ref,
                x1_ref, h2_ref, lg_ref, ext_ref):
    i = pl.program_id(0)
    tm, width = u_ref.shape
    first = (i % tiles_per_seq) == 0
    last = (i % tiles_per_seq) == tiles_per_seq - 1
    u = u_ref[...].astype(F32)
    ext_ref[0:HALO, :] = jnp.where(first, jnp.zeros_like(up_ref), up_ref[...])
    ext_ref[HALO:HALO + tm, :] = u_ref[...]
    ext_ref[HALO + tm:, :] = jnp.where(last, jnp.zeros_like(un_ref), un_ref[...])
    tpos = (i % tiles_per_seq) * tm + lax.broadcasted_iota(jnp.int32, (tm, 1), 0)
    gd = width // len(POOL_WINDOWS)
    sub = POOL_SUB
    win = sub + 2 * HALO
    ri = lax.broadcasted_iota(jnp.int32, (sub, win), 0) + HALO
    ci = lax.broadcasted_iota(jnp.int32, (sub, win), 1)
    mixed = []
    for gi, w in enumerate(POOL_WINDOWS):
        cols = slice(gi * gd, (gi + 1) * gd)
        band = jnp.where((ci >= ri - w // 2) & (ci < ri + w // 2), 1.0, 0.0).astype(BF16)
        s = jnp.concatenate([_dot(band, ext_ref[m * sub:m * sub + win, cols]) for m in range(tm // sub)], axis=0)
        cnt = (jnp.minimum(tpos + w // 2, seq) - jnp.maximum(tpos - w // 2, 0)).astype(F32)
        diff = s / cnt - u[:, cols]
        mixed.append(_dot(diff.astype(BF16), pw_ref[gi]))
    mixed = jnp.concatenate(mixed, axis=1) * pscale_ref[...]
    y_pool = _dot(mixed.astype(BF16), wpo_ref[...])
    y_ssd = _dot(yn_ref[...], wssd_ref[...])
    d = y_ssd.shape[1]
    g_ssd = jax.nn.sigmoid(ga_ref[...].astype(F32) + gateb_ref[:, 0:d])
    g_pool = jax.nn.sigmoid(gb_ref[...].astype(F32) + gateb_ref[:, d:2 * d])
    mix = _dot((g_ssd * y_ssd + g_pool * y_pool).astype(BF16), wo_ref[...])
    x1 = _layernorm(DEEPNORM_ALPHA * x_ref[...] + g1_ref[...] * mix) * ln1g_ref[...] + ln1b_ref[...]
    x1_ref[...] = x1
    h2 = _layernorm(x1) * (1.0 + sc2_ref[...]) + sh2_ref[...]
    h2_ref[...] = h2
    lg_ref[...] = _dot3(h2, wr_ref[...]) + rb_ref[...]


def _mix(yn, big, x2, w_ssd_out, pool_w, w_pool_out, w_o, w_router, gate_b, pool_scale, ln1_g, ln1_b,
         r_bias, g1, sh2, sc2, seq, pool_col, gate_col):
    t, d = x2.shape
    tm = TM_MIX
    tiles_per_seq = seq // tm
    hb = tm // HALO
    nhb = t // HALO
    pcb = pool_col // d
    gcb = gate_col // d
    full = lambda a: pl.BlockSpec(a.shape, lambda i: (0,) * a.ndim)
    per_batch = pl.BlockSpec((None, 1, d), lambda i: (i // tiles_per_seq, 0, 0))
    return pl.pallas_call(
        functools.partial(_mix_kernel, tiles_per_seq, seq),
        grid=(t // tm,),
        in_specs=[pl.BlockSpec((tm, yn.shape[1]), lambda i: (i, 0)),
                  pl.BlockSpec((tm, d), lambda i: (i, pcb)),
                  pl.BlockSpec((HALO, d), lambda i: (jnp.maximum(i * hb - 1, 0), pcb)),
                  pl.BlockSpec((HALO, d), lambda i: (jnp.minimum((i + 1) * hb, nhb - 1), pcb)),
                  pl.BlockSpec((tm, d), lambda i: (i, gcb)),
                  pl.BlockSpec((tm, d), lambda i: (i, gcb + 1)),
                  pl.BlockSpec((tm, d), lambda i: (i, 0)),
                  full(w_ssd_out), full(pool_w), full(w_pool_out), full(w_o), full(w_router),
                  full(gate_b), full(pool_scale), full(ln1_g), full(ln1_b), full(r_bias),
                  per_batch, per_batch, per_batch],
        out_specs=[pl.BlockSpec((tm, d), lambda i: (i, 0)),
                   pl.BlockSpec((tm, d), lambda i: (i, 0)),
                   pl.BlockSpec((tm, LANES), lambda i: (i, 0))],
        out_shape=[jax.ShapeDtypeStruct((t, d), F32),
                   jax.ShapeDtypeStruct((t, d), F32),
                   jax.ShapeDtypeStruct((t, LANES), F32)],
        scratch_shapes=[pltpu.VMEM((tm + 2 * HALO, d), BF16)],
        compiler_params=_cparams(("arbitrary",)),
        name="mix_postln",
    )(yn, big, big, big, big, big, x2, w_ssd_out, pool_w, w_pool_out, w_o, w_router,
      gate_b, pool_scale, ln1_g, ln1_b, r_bias, g1, sh2, sc2)


def _route_kernel(lg_ref, rt_ref, cnt_ref, carry_ref):
    @pl.when(pl.program_id(0) == 0)
    def _():
        carry_ref[...] = jnp.zeros_like(carry_ref)

    lg = lg_ref[...]
    tm = lg.shape[0]
    lane = lax.broadcasted_iota(jnp.int32, lg.shape, 1).astype(F32)
    neg = -jnp.inf
    big_lane = float(LANES)
    gl = jnp.where(lane < MOE_GROUPS, lg, neg)
    gmax = jnp.max(gl, axis=-1, keepdims=True)
    g_w = 1.0 / jnp.sum(jnp.exp(gl - gmax), axis=-1, keepdims=True)
    g_idx = jnp.min(jnp.where(gl == gmax, lane, big_lane), axis=-1, keepdims=True)
    lo = MOE_GROUPS + MOE_EXPERTS_PER_GROUP * g_idx
    el = jnp.where((lane >= lo) & (lane < lo + MOE_EXPERTS_PER_GROUP), lg, neg)
    m1 = jnp.max(el, axis=-1, keepdims=True)
    i1 = jnp.min(jnp.where(el == m1, lane, big_lane), axis=-1, keepdims=True)
    el2 = jnp.where(lane == i1, neg, el)
    m2 = jnp.max(el2, axis=-1, keepdims=True)
    i2 = jnp.min(jnp.where(el2 == m2, lane, big_lane), axis=-1, keepdims=True)
    e = jnp.exp(m2 - m1)
    w1 = g_w / (1.0 + e)
    w2 = g_w * e / (1.0 + e)
    onehot = jnp.where((lane == i1) | (lane == i2), 1.0, 0.0)
    ri = lax.broadcasted_iota(jnp.int32, (tm, tm), 0)
    ci = lax.broadcasted_iota(jnp.int32, (tm, tm), 1)
    earlier = jnp.where(ci < ri, 1.0, 0.0).astype(BF16)
    rank = _dot(earlier, onehot.astype(BF16)) + carry_ref[...]
    r1 = jnp.sum(jnp.where(lane == i1, rank, 0.0), axis=-1, keepdims=True)
    r2 = jnp.sum(jnp.where(lane == i2, rank, 0.0), axis=-1, keepdims=True)
    carry_ref[...] = carry_ref[...] + jnp.sum(onehot, axis=0, keepdims=True)
    cnt_ref[...] = carry_ref[...]
    out = jnp.where(lane == 0, i1 - MOE_GROUPS, 0.0)
    out = jnp.where(lane == 1, i2 - MOE_GROUPS, out)
    out = jnp.where(lane == 2, r1, out)
    out = jnp.where(lane == 3, r2, out)
    out = jnp.where(lane == 4, w1, out)
    out = jnp.where(lane == 5, w2, out)
    rt_ref[...] = out


def _route(logits):
    t = logits.shape[0]
    tm = TM_ROUTE
    return pl.pallas_call(
        _route_kernel,
        grid=(t // tm,),
        in_specs=[pl.BlockSpec((tm, LANES), lambda i: (i, 0))],
        out_specs=[pl.BlockSpec((tm, LANES), lambda i: (i, 0)),
                   pl.BlockSpec((1, LANES), lambda i: (0, 0))],
        out_shape=[jax.ShapeDtypeStruct((t, LANES), F32),
                   jax.ShapeDtypeStruct((1, LANES), F32)],
        scratch_shapes=[pltpu.VMEM((1, LANES), F32)],
        compiler_params=_cparams(("arbitrary",)),
        name="route",
    )(logits)


def _slotmap_kernel(n_tokens, pos_ref, init_ref, inv_ref, sem):
    i = pl.program_id(0)
    tm = pos_ref.shape[1] // 2

    @pl.when(i == 0)
    def _():
        cp = pltpu.make_async_copy(init_ref, inv_ref, sem)
        cp.start()
        cp.wait()

    def body(r, carry):
        tok = i * tm + r
        inv_ref[pos_ref[0, 2 * r]] = tok
        inv_ref[pos_ref[0, 2 * r + 1]] = n_tokens + tok
        return carry

    lax.fori_loop(0, tm, body, 0, unroll=SLOTMAP_UNROLL)


def _slotmap(pos, init):
    t = pos.shape[0]
    tm = TM_SLOTMAP
    pos3 = pos.reshape(t // tm, 1, 2 * tm)
    return pl.pallas_call(
        functools.partial(_slotmap_kernel, t),
        grid=(t // tm,),
        in_specs=[pl.BlockSpec((None, 1, 2 * tm), lambda i: (i, 0, 0), memory_space=pltpu.SMEM),
                  pl.BlockSpec(memory_space=pl.ANY)],
        out_specs=pl.BlockSpec(memory_space=pltpu.SMEM),
        out_shape=jax.ShapeDtypeStruct(init.shape, jnp.int32),
        scratch_shapes=[pltpu.SemaphoreType.DMA(())],
        compiler_params=_cparams(("arbitrary",)),
        name="slotmap",
    )(pos3, init)


def _row_copy(src_ref, src_row, dst_ref, dst_row, sem):
    return pltpu.make_async_copy(src_ref.at[pl.ds(src_row, 1)], dst_ref.at[pl.ds(dst_row, 1)], sem)


def _expert_kernel(n_tokens, te_ref, nu_ref, inv_prev_ref, inv_nxt_ref, h2_ref, wg_ref, wu_ref, wd_ref,
                   y2_ref, xbuf_ref, ybuf_ref, wgb_ref, wub_ref, wdb_ref, gsem, ssem):
    j = pl.program_id(0)
    n_used = nu_ref[0]
    tm = ybuf_ref.shape[1]
    slot = j % 2
    tok_mask = n_tokens - 1

    def gather_start(inv_ref, s):
        for r in range(tm):
            cp = _row_copy(h2_ref, inv_ref[0, r] & tok_mask, xbuf_ref.at[s], r, gsem.at[s])
            cp.start(priority=r % DMA_PRIORITIES)

    def gather_wait(s):
        pltpu.make_async_copy(h2_ref.at[pl.ds(0, tm)], xbuf_ref.at[s], gsem.at[s]).wait()

    def scatter_start(s):
        for r in range(tm):
            cp = _row_copy(ybuf_ref.at[s], r, y2_ref, inv_prev_ref[0, r], ssem.at[s])
            cp.start(priority=r % DMA_PRIORITIES)

    def scatter_wait(s):
        pltpu.make_async_copy(ybuf_ref.at[s], y2_ref.at[pl.ds(0, tm)], ssem.at[s]).wait()

    def spare_fill(k):
        return pltpu.make_async_copy(ybuf_ref.at[1], y2_ref.at[pl.ds(2 * n_tokens + k * tm, tm)], ssem.at[1])

    @pl.when(j == 0)
    def _():
        gather_start(inv_prev_ref, 0)
        ybuf_ref[1] = jnp.zeros(ybuf_ref.shape[1:], ybuf_ref.dtype)
        n_spare = (y2_ref.shape[0] - 2 * n_tokens) // tm
        for k in range(n_spare):
            spare_fill(k).start()
        for k in range(n_spare):
            spare_fill(k).wait()

    new_expert = jnp.logical_or(j == 0, te_ref[j] != te_ref[jnp.maximum(j - 1, 0)])

    @pl.when(jnp.logical_and(j < n_used, new_expert))
    def _():
        wgb_ref[...] = wg_ref[...].astype(BF16)
        wub_ref[...] = wu_ref[...].astype(BF16)
        wdb_ref[...] = wd_ref[...].astype(BF16)

    def tile_step(tiles_before):
        gather_wait(slot)
        xb = xbuf_ref[slot].astype(BF16)
        gather_start(inv_nxt_ref, 1 - slot)
        if tiles_before >= 1:
            scatter_start(1 - slot)
        act = _silu(_dot(xb, wgb_ref[...])) * _dot(xb, wub_ref[...])
        y = _dot(act.astype(BF16), wdb_ref[...])
        if tiles_before >= 2:
            scatter_wait(slot)
        ybuf_ref[slot] = y

    for before in (0, 1):
        pl.when(jnp.logical_and(j < n_used, j == before))(functools.partial(tile_step, before))
    pl.when(jnp.logical_and(j < n_used, j >= 2))(functools.partial(tile_step, 2))

    @pl.when(j == n_used)
    def _():
        gather_wait(slot)
        scatter_start(1 - slot)
        scatter_wait(slot)
        scatter_wait(1 - slot)


def _experts(tile_expert, n_used, inv, h2, w_gate, w_up, w_down, y2_rows):
    t, d = h2.shape
    hdim = w_gate.shape[2]
    tm = TM_EXPERT
    n_tiles = inv.shape[0] // tm
    inv3 = inv.reshape(n_tiles, 1, tm)
    last_used = lambda j, nu: jnp.minimum(j, nu[0] - 1)
    grid_spec = pltpu.PrefetchScalarGridSpec(
        num_scalar_prefetch=2,
        grid=(n_tiles + 1,),
        in_specs=[pl.BlockSpec((None, 1, tm), lambda j, te, nu: (jnp.maximum(j - 1, 0), 0, 0),
                               memory_space=pltpu.SMEM),
                  pl.BlockSpec((None, 1, tm), lambda j, te, nu: (jnp.minimum(j + 1, n_tiles - 1), 0, 0),
                               memory_space=pltpu.SMEM),
                  pl.BlockSpec(memory_space=pl.ANY),
                  pl.BlockSpec((None, d, hdim), lambda j, te, nu: (te[last_used(j, nu)], 0, 0)),
                  pl.BlockSpec((None, d, hdim), lambda j, te, nu: (te[last_used(j, nu)], 0, 0)),
                  pl.BlockSpec((None, hdim, d), lambda j, te, nu: (te[last_used(j, nu)], 0, 0))],
        out_specs=pl.BlockSpec(memory_space=pl.ANY),
        scratch_shapes=[pltpu.VMEM((2, tm, d), F32), pltpu.VMEM((2, tm, d), F32),
                        pltpu.VMEM((d, hdim), BF16), pltpu.VMEM((d, hdim), BF16),
                        pltpu.VMEM((hdim, d), BF16),
                        pltpu.SemaphoreType.DMA((2,)), pltpu.SemaphoreType.DMA((2,))],
    )
    return pl.pallas_call(
        functools.partial(_expert_kernel, t),
        grid_spec=grid_spec,
        out_shape=jax.ShapeDtypeStruct((y2_rows, d), F32),
        compiler_params=_cparams(("arbitrary",)),
        name="experts",
    )(tile_expert, n_used, inv3, inv3, h2, w_gate, w_up, w_down)


def _combine_kernel(y0_ref, y1_ref, rt_ref, x1_ref, g2_ref, lng_ref, lnb_ref, o_ref):
    rt = rt_ref[...]
    y_moe = rt[:, 4:5] * y0_ref[...] + rt[:, 5:6] * y1_ref[...]
    v = DEEPNORM_ALPHA * x1_ref[...] + g2_ref[...] * y_moe
    o_ref[...] = _layernorm(v) * lng_ref[...] + lnb_ref[...]


def _combine(y2, rt, x1, g2, ln_g, ln_b, seq):
    t, d = x1.shape
    tm = TM_COMBINE
    tiles_per_seq = seq // tm
    nt = t // tm
    return pl.pallas_call(
        _combine_kernel,
        grid=(nt,),
        in_specs=[pl.BlockSpec((tm, d), lambda i: (i, 0)),
                  pl.BlockSpec((tm, d), lambda i: (nt + i, 0)),
                  pl.BlockSpec((tm, LANES), lambda i: (i, 0)),
                  pl.BlockSpec((tm, d), lambda i: (i, 0)),
                  pl.BlockSpec((None, 1, d), lambda i: (i // tiles_per_seq, 0, 0)),
                  pl.BlockSpec((1, d), lambda i: (0, 0)),
                  pl.BlockSpec((1, d), lambda i: (0, 0))],
        out_specs=pl.BlockSpec((tm, d), lambda i: (i, 0)),
        out_shape=jax.ShapeDtypeStruct((t, d), F32),
        compiler_params=_cparams(("arbitrary",)),
        name="combine_postln",
    )(y2, y2, rt, x1, g2, ln_g, ln_b)


def _layer(x2, c_pad, batch, seq, w_ada, b_ada, w_in, conv_w, conv_b, a_log_f, a_log_b, dt_bias_f,
           dt_bias_b, d_skip, ssd_norm_g, w_ssd_out, pool_w, pool_scale, w_pool_out, gate_b, w_o,
           ln1_g, ln1_b, router_wg, router_bg, router_we, router_be, exp_w_gate, exp_w_up,
           exp_w_down, ln2_g, ln2_b):
    t, d = x2.shape
    assert t & (t - 1) == 0, "the slot map packs (slot, token) as slot * tokens + token"
    assert 2 * t >= 2 * TM_EXPERT, "the expert pipeline assumes at least two used tiles"
    d_inner = SSD_HEADS * SSD_HEAD_DIM
    conv_dim = d_inner + 2 * SSD_GROUPS * SSD_STATE
    pool_width = pool_scale.shape[0]

    mod = _ada(c_pad, w_ada, b_ada[None, :])[:batch]
    sh1, sc1, g1, sh2, sc2, g2 = [m[:, None, :] for m in jnp.split(mod, 6, axis=-1)]

    o_xbc = d_inner
    o_dt = o_xbc + conv_dim
    o_pool = o_dt + 2 * SSD_HEADS
    w_main = jnp.concatenate([w_in[:, :o_dt], w_in[:, o_pool:]], axis=1).astype(BF16)
    w_dt = jnp.pad(w_in[:, o_dt:o_pool], ((0, 0), (0, LANES - 2 * SSD_HEADS)))
    pool_col = o_dt
    gate_col = o_dt + pool_width
    big, dt_raw = _inproj(x2, sh1, sc1, w_main, w_dt, seq)

    xc = _conv(big, o_xbc, conv_w, conv_b[None, :], seq)

    pad_row = lambda f, b: jnp.pad(jnp.concatenate([f, b]), (0, LANES - 2 * SSD_HEADS))[None, :]
    alog_row = pad_row(a_log_f, a_log_b)
    bias_row = pad_row(dt_bias_f, dt_bias_b)
    dskip_row = jnp.repeat(d_skip, SSD_HEAD_DIM)[None, :]
    y_bwd = _ssd(xc, dt_raw, alog_row, bias_row, batch, reverse=True)
    yn = _ssd(xc, dt_raw, alog_row, bias_row, batch, reverse=False,
              final_inputs=(big, y_bwd, dskip_row, ssd_norm_g[None, :]))

    w_router = jnp.pad(jnp.concatenate([router_wg, router_we], axis=1),
                       ((0, 0), (0, LANES - MOE_GROUPS - MOE_EXPERTS)))
    r_bias = jnp.pad(jnp.concatenate([router_bg, router_be]), (0, LANES - MOE_GROUPS - MOE_EXPERTS))[None, :]
    x1, h2, logits = _mix(yn, big, x2, w_ssd_out.astype(BF16), pool_w.astype(BF16),
                          w_pool_out.astype(BF16), w_o.astype(BF16), w_router, gate_b[None, :],
                          pool_scale[None, :], ln1_g[None, :], ln1_b[None, :], r_bias, g1, sh2, sc2,
                          seq, pool_col, gate_col)

    rt, counts = _route(logits)

    tme = TM_EXPERT
    cnt = counts[0, MOE_GROUPS:MOE_GROUPS + MOE_EXPERTS].astype(jnp.int32)
    padded = ((cnt + tme - 1) // tme) * tme
    ends = jnp.cumsum(padded)
    off = ends - padded
    eid = rt[:, 0:2].astype(jnp.int32)
    pos = rt[:, 2:4].astype(jnp.int32) + jnp.sum(
        jnp.where(eid[:, :, None] == jnp.arange(MOE_EXPERTS, dtype=jnp.int32), off, 0), axis=-1)
    p_rows = 2 * t + MOE_EXPERTS * tme
    n_tiles = p_rows // tme
    tile_ends = ends // tme
    tile_expert = jnp.minimum(
        jnp.sum(jnp.arange(n_tiles + 1, dtype=jnp.int32)[:, None] >= tile_ends[None, :], axis=1),
        MOE_EXPERTS - 1).astype(jnp.int32)
    n_used = tile_ends[-1:].astype(jnp.int32)

    rows = jnp.arange(p_rows, dtype=jnp.int32)
    real_before = jnp.sum(jnp.where(rows[:, None] >= off[None, :], cnt[None, :], 0), axis=1)
    inv = _slotmap(pos, 2 * t + rows - real_before)
    y2 = _experts(tile_expert, n_used, inv, h2, exp_w_gate, exp_w_up, exp_w_down, p_rows)
    return _combine(y2, rt, x1, g2, ln2_g[None, :], ln2_b[None, :], seq)


def kernel(x, c, w_ada, b_ada, w_in, conv_w, conv_b, a_log_f, a_log_b, dt_bias_f, dt_bias_b, d_skip,
           ssd_norm_g, w_ssd_out, pool_w, pool_scale, w_pool_out, gate_b, w_o, ln1_g, ln1_b,
           router_wg, router_bg, router_we, router_be, exp_w_gate, exp_w_up, exp_w_down, ln2_g, ln2_b):
    batch, seq, d = x.shape
    x2 = x.reshape(batch * seq, d)
    c_pad = jnp.pad(c, ((0, 8 - batch), (0, 0)))
    params = (w_ada, b_ada, w_in, conv_w, conv_b, a_log_f, a_log_b, dt_bias_f, dt_bias_b, d_skip,
              ssd_norm_g, w_ssd_out, pool_w, pool_scale, w_pool_out, gate_b, w_o, ln1_g, ln1_b,
              router_wg, router_bg, router_we, router_be, exp_w_gate, exp_w_up, exp_w_down, ln2_g, ln2_b)
    for l in range(w_ada.shape[0]):
        x2 = _layer(x2, c_pad, batch, seq, *[p[l] for p in params])
    return x2.reshape(batch, seq, d)
```

```python
import functools

import jax
import jax.numpy as jnp
from jax import lax
from jax.experimental import pallas as pl
from jax.experimental.pallas import tpu as pltpu

F32 = jnp.float32
BF16 = jnp.bfloat16
HIGHEST = lax.Precision.HIGHEST

SSD_HEAD_DIM = 64
SSD_GROUPS = 8
SSD_HEADS_PER_GROUP = 4
SSD_HEADS = SSD_GROUPS * SSD_HEADS_PER_GROUP
SSD_STATE = 128
SSD_CONV = 5
SSD_CHUNK = 128
SSD_NORM_EPS = 1e-5
POOL_WINDOWS = (2, 4, 8, 16)
MOE_GROUPS = 4
MOE_EXPERTS_PER_GROUP = 8
MOE_EXPERTS = MOE_GROUPS * MOE_EXPERTS_PER_GROUP
DEPTH = 1
DEEPNORM_ALPHA = (2.0 * DEPTH) ** 0.25
LN_EPS = 1e-5

LANES = 128
HALO = 16
VMEM_LIMIT = 48 * 1024 * 1024
SLOTMAP_UNROLL = 8
DMA_PRIORITIES = 2

TM_INPROJ = 1024
TN_INPROJ = 2304
TM_CONV = 512
CONV_SUB = 128
TM_MIX = 256
POOL_SUB = 128
TM_ROUTE = 512
TM_SLOTMAP = 1024
TM_EXPERT = 256
TM_COMBINE = 512


def _dot(a, b):
    return jnp.dot(a, b, preferred_element_type=F32)


def _split_hi_lo(v):
    hi = v.astype(BF16)
    lo = (v - hi.astype(F32)).astype(BF16)
    return hi, lo


def _split3(v):
    hi = v.astype(BF16)
    r = v - hi.astype(F32)
    mid = r.astype(BF16)
    lo = (r - mid.astype(F32)).astype(BF16)
    return hi, mid, lo


def _dot3(a, b):
    a_hi, a_lo = _split_hi_lo(a)
    b_hi, b_lo = _split_hi_lo(b)
    return _dot(a_hi, b_hi) + _dot(a_lo, b_hi) + _dot(a_hi, b_lo)


def _layernorm(v):
    mu = jnp.mean(v, axis=-1, keepdims=True)
    vc = v - mu
    var = jnp.mean(vc * vc, axis=-1, keepdims=True)
    return vc * lax.rsqrt(var + LN_EPS)


def _silu(v):
    return v * jax.nn.sigmoid(v)


def _pack_bf16_pairs(v):
    m = v.shape[1] // 2
    hi = lax.bitcast_convert_type(v[:, :m].astype(BF16).astype(F32), jnp.int32)
    lo = lax.bitcast_convert_type(v[:, m:].astype(BF16).astype(F32), jnp.int32)
    return hi | lax.shift_right_logical(lo, 16)


def _unpack_bf16_pairs(p):
    hi = lax.bitcast_convert_type(p & jnp.int32(-65536), F32)
    lo = lax.bitcast_convert_type(lax.shift_left(p, 16), F32)
    return hi, lo


def _cparams(sem):
    return pltpu.CompilerParams(dimension_semantics=sem, vmem_limit_bytes=VMEM_LIMIT)


def _ada_kernel(c_ref, w_ref, b_ref, o_ref):
    o_ref[...] = jnp.dot(_silu(c_ref[...]), w_ref[...], precision=HIGHEST,
                         preferred_element_type=F32) + b_ref[...]


def _ada(c_pad, w, b):
    d, n = w.shape
    tn = 1024
    return pl.pallas_call(
        _ada_kernel,
        grid=(n // tn,),
        in_specs=[pl.BlockSpec((c_pad.shape[0], d), lambda j: (0, 0)),
                  pl.BlockSpec((d, tn), lambda j: (0, j)),
                  pl.BlockSpec((1, tn), lambda j: (0, j))],
        out_specs=pl.BlockSpec((c_pad.shape[0], tn), lambda j: (0, j)),
        out_shape=jax.ShapeDtypeStruct((c_pad.shape[0], n), F32),
        compiler_params=_cparams(("arbitrary",)),
        name="ada_mod",
    )(c_pad, w, b)


def _inproj_kernel(x_ref, sh_ref, sc_ref, w_ref, wdt_ref, o_ref, dt_ref, h_ref):
    @pl.when(pl.program_id(1) == 0)
    def _():
        h = _layernorm(x_ref[...]) * (1.0 + sc_ref[...]) + sh_ref[...]
        h_ref[...] = h.astype(BF16)
        dt_ref[...] = _dot3(h, wdt_ref[...])

    o_ref[...] = _dot(h_ref[...], w_ref[...]).astype(BF16)


def _inproj(x2, sh, sc, w_main, w_dt, seq):
    t, d = x2.shape
    n = w_main.shape[1]
    tm, tn = TM_INPROJ, TN_INPROJ
    tiles_per_seq = seq // tm
    return pl.pallas_call(
        _inproj_kernel,
        grid=(t // tm, n // tn),
        in_specs=[pl.BlockSpec((tm, d), lambda i, j: (i, 0)),
                  pl.BlockSpec((None, 1, d), lambda i, j: (i // tiles_per_seq, 0, 0)),
                  pl.BlockSpec((None, 1, d), lambda i, j: (i // tiles_per_seq, 0, 0)),
                  pl.BlockSpec((d, tn), lambda i, j: (0, j)),
                  pl.BlockSpec((d, LANES), lambda i, j: (0, 0))],
        out_specs=[pl.BlockSpec((tm, tn), lambda i, j: (i, j)),
                   pl.BlockSpec((tm, LANES), lambda i, j: (i, 0))],
        out_shape=[jax.ShapeDtypeStruct((t, n), BF16),
                   jax.ShapeDtypeStruct((t, LANES), F32)],
        scratch_shapes=[pltpu.VMEM((tm, d), BF16)],
        compiler_params=_cparams(("arbitrary", "arbitrary")),
        name="ln_inproj",
    )(x2, sh, sc, w_main, w_dt)


def _conv_kernel(tiles_per_seq, cur_ref, prev_ref, next_ref, w_ref, b_ref, o_ref, ext_ref):
    i = pl.program_id(0)
    tm = cur_ref.shape[0]
    first = (i % tiles_per_seq) == 0
    last = (i % tiles_per_seq) == tiles_per_seq - 1
    ext_ref[0:HALO, :] = jnp.where(first, jnp.zeros_like(prev_ref), prev_ref[...])
    ext_ref[HALO:HALO + tm, :] = cur_ref[...]
    ext_ref[HALO + tm:, :] = jnp.where(last, jnp.zeros_like(next_ref), next_ref[...])
    pad = SSD_CONV // 2
    sub = CONV_SUB
    win = sub + 2 * HALO
    ri = lax.broadcasted_iota(jnp.int32, (sub, win), 0)
    ci = lax.broadcasted_iota(jnp.int32, (sub, win), 1)
    picks = [jnp.where(ci == ri + HALO + k - pad, 1.0, 0.0).astype(BF16) for k in range(SSD_CONV)]
    for m in range(tm // sub):
        window = ext_ref[m * sub:m * sub + win, :]
        acc = b_ref[...] + w_ref[pad:pad + 1, :] * ext_ref[HALO + m * sub:HALO + (m + 1) * sub, :].astype(F32)
        for k in range(SSD_CONV):
            if k != pad:
                acc = acc + w_ref[k:k + 1, :] * _dot(picks[k], window)
        o_ref[m * sub:(m + 1) * sub, :] = _silu(acc).astype(BF16)


def _conv(big, col_off, conv_w, conv_b, seq):
    t = big.shape[0]
    cdim = conv_w.shape[1]
    tm, tc = TM_CONV, 1024
    tiles_per_seq = seq // tm
    cb0 = col_off // tc
    hb = tm // HALO
    nhb = t // HALO
    return pl.pallas_call(
        functools.partial(_conv_kernel, tiles_per_seq),
        grid=(t // tm, cdim // tc),
        in_specs=[pl.BlockSpec((tm, tc), lambda i, j: (i, cb0 + j)),
                  pl.BlockSpec((HALO, tc), lambda i, j: (jnp.maximum(i * hb - 1, 0), cb0 + j)),
                  pl.BlockSpec((HALO, tc), lambda i, j: (jnp.minimum((i + 1) * hb, nhb - 1), cb0 + j)),
                  pl.BlockSpec((SSD_CONV, tc), lambda i, j: (0, j)),
                  pl.BlockSpec((1, tc), lambda i, j: (0, j))],
        out_specs=pl.BlockSpec((tm, tc), lambda i, j: (i, j)),
        out_shape=jax.ShapeDtypeStruct((t, cdim), BF16),
        scratch_shapes=[pltpu.VMEM((tm + 2 * HALO, tc), BF16)],
        compiler_params=_cparams(("arbitrary", "arbitrary")),
        name="conv_silu",
    )(big, big, big, conv_w, conv_b)


def _tri_cumsum(tri, v):
    hi, mid, lo = _split3(v)
    return _dot(tri, hi) + _dot(tri, mid) + _dot(tri, lo)


def _expand_heads(v, e_ref):
    hi, lo = _split_hi_lo(v)
    return _dot(jnp.concatenate([hi, lo], axis=1), e_ref[...])


def _state_step(g, x_g, b_ref, c_ref, st_ref, w_exp, dec_exp, end_row):
    gw = x_g.shape[1]
    cols = slice(g * gw, (g + 1) * gw)
    bg = b_ref[:, g * SSD_STATE:(g + 1) * SSD_STATE]
    cg = c_ref[:, g * SSD_STATE:(g + 1) * SSD_STATE]
    h_in = st_ref[g]
    y_off = _dot(cg, h_in.astype(BF16)) * dec_exp[:, cols]
    xw = (x_g * w_exp[:, cols]).astype(BF16)
    s_new = lax.dot_general(bg, xw, (((0,), (0,)), ((), ())), preferred_element_type=F32)
    st_ref[g] = h_in * dec_exp[end_row:end_row + 1, cols] + s_new
    return y_off


def _ssd_bwd_kernel(x_ref, b_ref, c_ref, dt_ref, alog_ref, bias_ref, e64_ref, o_ref, st_ref):
    q = x_ref.shape[0]
    gw = SSD_HEADS_PER_GROUP * SSD_HEAD_DIM

    @pl.when(pl.program_id(1) == 0)
    def _():
        st_ref[...] = jnp.zeros_like(st_ref)

    dt = jax.nn.softplus(dt_ref[...] + bias_ref[...])
    da = dt * (-jnp.exp(alog_ref[...]))
    li = lax.broadcasted_iota(jnp.int32, (q, q), 0)
    si = lax.broadcasted_iota(jnp.int32, (q, q), 1)
    cs = _tri_cumsum(jnp.where(si >= li, 1.0, 0.0).astype(BF16), da)
    w_exp = _expand_heads(dt * jnp.exp(cs[0:1, :] - cs), e64_ref)
    dec_exp = _expand_heads(jnp.exp(cs), e64_ref)
    for g in range(SSD_GROUPS):
        cols = slice(g * gw, (g + 1) * gw)
        x_g = x_ref[:, cols].astype(F32)
        o_ref[:, cols] = _state_step(g, x_g, b_ref, c_ref, st_ref, w_exp, dec_exp, 0).astype(BF16)


def _ssd_fwd_kernel(x_ref, b_ref, c_ref, dt_ref, alog_ref, bias_ref, e64_ref, ecs_ref, z_ref,
                    yb_ref, dskip_ref, ng_ref, o_ref, st_ref):
    q = x_ref.shape[0]
    p = SSD_HEAD_DIM
    hpg = SSD_HEADS_PER_GROUP
    gw = hpg * p

    @pl.when(pl.program_id(1) == 0)
    def _():
        st_ref[...] = jnp.zeros_like(st_ref)

    dt = jax.nn.softplus(dt_ref[...] + bias_ref[...])
    da = dt * (-jnp.exp(alog_ref[...]))
    li = lax.broadcasted_iota(jnp.int32, (q, q), 0)
    si = lax.broadcasted_iota(jnp.int32, (q, q), 1)
    before = si <= li
    after = si >= li
    lane = lax.broadcasted_iota(jnp.int32, (q, LANES), 1)
    tri2 = jnp.concatenate([jnp.where(before, 1.0, 0.0), jnp.where(after, 1.0, 0.0)], axis=1).astype(BF16)
    da2 = jnp.concatenate([jnp.where(lane < SSD_HEADS, da, 0.0),
                           jnp.where((lane >= SSD_HEADS) & (lane < 2 * SSD_HEADS), da, 0.0)], axis=0)
    cs = _tri_cumsum(tri2, da2)
    csp_t = (cs - jnp.log(dt)).T
    hi, mid, lo = _split3(cs)
    nhd = 2 * SSD_HEADS
    v_cs = jnp.concatenate([hi[:, :nhd].astype(F32), mid[:, :nhd].astype(F32), lo[:, :nhd].astype(F32),
                            jnp.zeros((q, nhd), F32)], axis=1).astype(BF16)
    to_end = jnp.where(lane < SSD_HEADS, cs[q - 1:q, :] - cs, 0.0)
    w_exp = _expand_heads(dt * jnp.exp(to_end), e64_ref)
    dec_exp = _expand_heads(jnp.exp(cs), e64_ref)
    half =lax.broadcasted_iota(jnp.int32, (q, 2 * p), 1) < p

    for g in range(SSD_GROUPS):
        cols = slice(g * gw, (g + 1) * gw)
        bg = b_ref[:, g * SSD_STATE:(g + 1) * SSD_STATE]
        cg = c_ref[:, g * SSD_STATE:(g + 1) * SSD_STATE]
        cb = lax.dot_general(cg, bg, (((1,), (1,)), ((), ())), preferred_element_type=F32)
        colb = _dot(v_cs, ecs_ref[:, g * 2 * hpg * q:(g + 1) * 2 * hpg * q])
        x_g = x_ref[:, cols].astype(F32)
        ms = []
        for r in range(hpg):
            hh = g * hpg + r
            seg_f = colb[:, r * q:(r + 1) * q] - csp_t[hh:hh + 1, :]
            seg_b = colb[:, (hpg + r) * q:(hpg + r + 1) * q] - csp_t[SSD_HEADS + hh:SSD_HEADS + hh + 1, :]
            decay = jnp.exp(jnp.where(before, seg_f, -jnp.inf)) + jnp.exp(jnp.where(after, seg_b, -jnp.inf))
            ms.append((cb * decay).astype(BF16))
        ys = []
        for pr in range(hpg // 2):
            xp = x_g[:, 2 * pr * p:2 * (pr + 1) * p]
            rhs = jnp.concatenate([jnp.where(half, xp, 0.0), jnp.where(half, 0.0, xp)], axis=0).astype(BF16)
            ys.append(_dot(jnp.concatenate([ms[2 * pr], ms[2 * pr + 1]], axis=1), rhs))
        y_g = jnp.concatenate(ys, axis=1)
        y_g = y_g + _state_step(g, x_g, b_ref, c_ref, st_ref, w_exp, dec_exp, q - 1)
        y_g = y_g + yb_ref[:, cols].astype(F32) + x_g * dskip_ref[:, cols]
        yg = y_g * _silu(z_ref[:, cols].astype(F32))
        ms_g = jnp.mean(yg * yg, axis=-1, keepdims=True)
        o_ref[:, cols] = (yg * lax.rsqrt(ms_g + SSD_NORM_EPS) * ng_ref[:, cols]).astype(BF16)


def _head_expansion(col0):
    j = jnp.arange(2 * LANES)[:, None]
    c = jnp.arange(SSD_HEADS * SSD_HEAD_DIM)[None, :]
    return ((j % LANES - col0) == c // SSD_HEAD_DIM).astype(BF16)


def _score_expansion():
    nhd = 2 * SSD_HEADS
    row = jnp.arange(4 * nhd)[:, None]
    col = jnp.arange(nhd * SSD_CHUNK)[None, :]
    j = row % nhd
    head = j % SSD_HEADS
    blk = ((head // SSD_HEADS_PER_GROUP) * 2 + j // SSD_HEADS) * SSD_HEADS_PER_GROUP + head % SSD_HEADS_PER_GROUP
    return ((row < 3 * nhd) & (blk == col // SSD_CHUNK)).astype(BF16)


def _ssd(xc, dt_raw, alog_row, bias_row, batch, reverse, final_inputs=None):
    t = xc.shape[0]
    q = SSD_CHUNK
    nc = t // batch // q
    d_inner = SSD_HEADS * SSD_HEAD_DIM
    bcw = SSD_GROUPS * SSD_STATE

    def row(b, c):
        return b * nc + ((nc - 1 - c) if reverse else c)

    full = lambda a: pl.BlockSpec(a.shape, lambda b, c: (0,) * a.ndim)
    e64 = _head_expansion(SSD_HEADS if reverse else 0)
    in_specs = [pl.BlockSpec((q, d_inner), lambda b, c: (row(b, c), 0)),
                pl.BlockSpec((q, bcw), lambda b, c: (row(b, c), d_inner // bcw)),
                pl.BlockSpec((q, bcw), lambda b, c: (row(b, c), d_inner // bcw + 1)),
                pl.BlockSpec((q, LANES), lambda b, c: (row(b, c), 0)),
                full(alog_row), full(bias_row), full(e64)]
    args = [xc, xc, xc, dt_raw, alog_row, bias_row, e64]
    if reverse:
        body = _ssd_bwd_kernel
    else:
        body = _ssd_fwd_kernel
        big, y_bwd, dskip_row, ng_row = final_inputs
        ecs = _score_expansion()
        in_specs += [full(ecs),
                     pl.BlockSpec((q, d_inner), lambda b, c: (row(b, c), 0)),
                     pl.BlockSpec((q, d_inner), lambda b, c: (row(b, c), 0)),
                     full(dskip_row), full(ng_row)]
        args += [ecs, big, y_bwd, dskip_row, ng_row]
    return pl.pallas_call(
        body,
        grid=(batch, nc),
        in_specs=in_specs,
        out_specs=pl.BlockSpec((q, d_inner), lambda b, c: (row(b, c), 0)),
        out_shape=jax.ShapeDtypeStruct((t, d_inner), BF16),
        scratch_shapes=[pltpu.VMEM((SSD_GROUPS, SSD_STATE, SSD_HEADS_PER_GROUP * SSD_HEAD_DIM), F32)],
        compiler_params=_cparams(("arbitrary", "arbitrary")),
        name="ssd_bwd" if reverse else "ssd_fwd",
    )(*args)


def _mix_kernel(tiles_per_seq, seq, yn_ref, u_ref, up_ref, un_ref, ga_ref, gb_ref, x_ref,
                wssd_ref, pw_ref, wpo_ref, wo_ref, wr_ref,
                gateb_ref, pscale_ref, ln1g_ref, ln1b_ref, rb_ref, g1_ref, sh2_ref, sc2_ref,
                x1_ref, h2_ref, lg_ref, ext_ref):
    i = pl.program_id(0)
    tm, width = u_ref.shape
    first = (i % tiles_per_seq) == 0
    last = (i % tiles_per_seq) == tiles_per_seq - 1
    u = u_ref[...].astype(F32)
    ext_ref[0:HALO, :] = jnp.where(first, jnp.zeros_like(up_ref), up_ref[...])
    ext_ref[HALO:HALO + tm, :] = u_ref[...]
    ext_ref[HALO + tm:, :] = jnp.where(last, jnp.zeros_like(un_ref), un_ref[...])
    tpos = (i % tiles_per_seq) * tm + lax.broadcasted_iota(jnp.int32, (tm, 1), 0)
    gd = width // len(POOL_WINDOWS)
    sub = POOL_SUB
    win = sub + 2 * HALO
    ri = lax.broadcasted_iota(jnp.int32, (sub, win), 0) + HALO
    ci = lax.broadcasted_iota(jnp.int32, (sub, win), 1)
    mixed = []
    for gi, w in enumerate(POOL_WINDOWS):
        cols = slice(gi * gd, (gi + 1) * gd)
        band = jnp.where((ci >= ri - w // 2) & (ci < ri + w // 2), 1.0, 0.0).astype(BF16)
        s = jnp.concatenate([_dot(band, ext_ref[m * sub:m * sub + win, cols]) for m in range(tm // sub)], axis=0)
        cnt = (jnp.minimum(tpos + w // 2, seq) - jnp.maximum(tpos - w // 2, 0)).astype(F32)
        diff = s / cnt - u[:, cols]
        mixed.append(_dot(diff.astype(BF16), pw_ref[gi]))
    mixed = jnp.concatenate(mixed, axis=1) * pscale_ref[...]
    y_pool = _dot(mixed.astype(BF16), wpo_ref[...])
    y_ssd = _dot(yn_ref[...], wssd_ref[...])
    d = y_ssd.shape[1]
    g_ssd = jax.nn.sigmoid(ga_ref[...].astype(F32) + gateb_ref[:, 0:d])
    g_pool = jax.nn.sigmoid(gb_ref[...].astype(F32) + gateb_ref[:, d:2 * d])
    mix = _dot((g_ssd * y_ssd + g_pool * y_pool).astype(BF16), wo_ref[...])
    x1 = _layernorm(DEEPNORM_ALPHA * x_ref[...] + g1_ref[...] * mix) * ln1g_ref[...] + ln1b_ref[...]
    x1_ref[...] = x1
    h2 = _layernorm(x1) * (1.0 + sc2_ref[...]) + sh2_ref[...]
    h2_ref[...] = _pack_bf16_pairs(h2)
    lg_ref[...] = _dot3(h2, wr_ref[...]) + rb_ref[...]


def _mix(yn, big, x2, w_ssd_out, pool_w, w_pool_out, w_o, w_router, gate_b, pool_scale, ln1_g, ln1_b,
         r_bias, g1, sh2, sc2, seq, pool_col, gate_col):
    t, d = x2.shape
    tm = TM_MIX
    tiles_per_seq = seq // tm
    hb = tm // HALO
    nhb = t // HALO
    pcb = pool_col // d
    gcb = gate_col // d
    full = lambda a: pl.BlockSpec(a.shape, lambda i: (0,) * a.ndim)
    per_batch = pl.BlockSpec((None, 1, d), lambda i: (i // tiles_per_seq, 0, 0))
    return pl.pallas_call(
        functools.partial(_mix_kernel, tiles_per_seq, seq),
        grid=(t // tm,),
        in_specs=[pl.BlockSpec((tm, yn.shape[1]), lambda i: (i, 0)),
                  pl.BlockSpec((tm, d), lambda i: (i, pcb)),
                  pl.BlockSpec((HALO, d), lambda i: (jnp.maximum(i * hb - 1, 0), pcb)),
                  pl.BlockSpec((HALO, d), lambda i: (jnp.minimum((i + 1) * hb, nhb - 1), pcb)),
                  pl.BlockSpec((tm, d), lambda i: (i, gcb)),
                  pl.BlockSpec((tm, d), lambda i: (i, gcb + 1)),
                  pl.BlockSpec((tm, d), lambda i: (i, 0)),
                  full(w_ssd_out), full(pool_w), full(w_pool_out), full(w_o), full(w_router),
                  full(gate_b), full(pool_scale), full(ln1_g), full(ln1_b), full(r_bias),
                  per_batch, per_batch, per_batch],
        out_specs=[pl.BlockSpec((tm, d), lambda i: (i, 0)),
                   pl.BlockSpec((tm, d // 2), lambda i: (i, 0)),
                   pl.BlockSpec((tm, LANES), lambda i: (i, 0))],
        out_shape=[jax.ShapeDtypeStruct((t, d), F32),
                   jax.ShapeDtypeStruct((t, d // 2), jnp.int32),
                   jax.ShapeDtypeStruct((t, LANES), F32)],
        scratch_shapes=[pltpu.VMEM((tm + 2 * HALO, d), BF16)],
        compiler_params=_cparams(("arbitrary",)),
        name="mix_postln",
    )(yn, big, big, big, big, big, x2, w_ssd_out, pool_w, w_pool_out, w_o, w_router,
      gate_b, pool_scale, ln1_g, ln1_b, r_bias, g1, sh2, sc2)


def _route_kernel(lg_ref, rt_ref, cnt_ref, carry_ref):
    @pl.when(pl.program_id(0) == 0)
    def _():
        carry_ref[...] = jnp.zeros_like(carry_ref)

    lg = lg_ref[...]
    tm = lg.shape[0]
    lane = lax.broadcasted_iota(jnp.int32, lg.shape, 1).astype(F32)
    neg = -jnp.inf
    big_lane = float(LANES)
    gl = jnp.where(lane < MOE_GROUPS, lg, neg)
    gmax = jnp.max(gl, axis=-1, keepdims=True)
    g_w = 1.0 / jnp.sum(jnp.exp(gl - gmax), axis=-1, keepdims=True)
    g_idx = jnp.min(jnp.where(gl == gmax, lane, big_lane), axis=-1, keepdims=True)
    lo = MOE_GROUPS + MOE_EXPERTS_PER_GROUP * g_idx
    el = jnp.where((lane >= lo) & (lane < lo + MOE_EXPERTS_PER_GROUP), lg, neg)
    m1 = jnp.max(el, axis=-1, keepdims=True)
    i1 = jnp.min(jnp.where(el == m1, lane, big_lane), axis=-1, keepdims=True)
    el2 = jnp.where(lane == i1, neg, el)
    m2 = jnp.max(el2, axis=-1, keepdims=True)
    i2 = jnp.min(jnp.where(el2 == m2, lane, big_lane), axis=-1, keepdims=True)
    e = jnp.exp(m2 - m1)
    w1 = g_w / (1.0 + e)
    w2 = g_w * e / (1.0 + e)
    onehot = jnp.where((lane == i1) | (lane == i2), 1.0, 0.0)
    ri = lax.broadcasted_iota(jnp.int32, (tm, tm), 0)
    ci = lax.broadcasted_iota(jnp.int32, (tm, tm), 1)
    earlier = jnp.where(ci < ri, 1.0, 0.0).astype(BF16)
    rank = _dot(earlier, onehot.astype(BF16)) + carry_ref[...]
    r1 = jnp.sum(jnp.where(lane == i1, rank, 0.0), axis=-1, keepdims=True)
    r2 = jnp.sum(jnp.where(lane == i2, rank, 0.0), axis=-1, keepdims=True)
    carry_ref[...] = carry_ref[...] + jnp.sum(onehot, axis=0, keepdims=True)
    cnt_ref[...] = carry_ref[...]
    out = jnp.where(lane == 0, i1 - MOE_GROUPS, 0.0)
    out = jnp.where(lane == 1, i2 - MOE_GROUPS, out)
    out = jnp.where(lane == 2, r1, out)
    out = jnp.where(lane == 3, r2, out)
    out = jnp.where(lane == 4, w1, out)
    out = jnp.where(lane == 5, w2, out)
    rt_ref[...] = out


def _route(logits):
    t = logits.shape[0]
    tm = TM_ROUTE
    return pl.pallas_call(
        _route_kernel,
        grid=(t // tm,),
        in_specs=[pl.BlockSpec((tm, LANES), lambda i: (i, 0))],
        out_specs=[pl.BlockSpec((tm, LANES), lambda i: (i, 0)),
                   pl.BlockSpec((1, LANES), lambda i: (0, 0))],
        out_shape=[jax.ShapeDtypeStruct((t, LANES), F32),
                   jax.ShapeDtypeStruct((1, LANES), F32)],
        scratch_shapes=[pltpu.VMEM((1, LANES), F32)],
        compiler_params=_cparams(("arbitrary",)),
        name="route",
    )(logits)


def _slotmap_kernel(n_tokens, pos_ref, init_ref, inv_ref, sem):
    i = pl.program_id(0)
    tm = pos_ref.shape[1] // 2

    @pl.when(i == 0)
    def _():
        cp = pltpu.make_async_copy(init_ref, inv_ref, sem)
        cp.start()
        cp.wait()

    def body(r, carry):
        tok = i * tm + r
        inv_ref[pos_ref[0, 2 * r]] = tok
        inv_ref[pos_ref[0, 2 * r + 1]] = n_tokens + tok
        return carry

    lax.fori_loop(0, tm, body, 0, unroll=SLOTMAP_UNROLL)


def _slotmap(pos, init):
    t = pos.shape[0]
    tm = TM_SLOTMAP
    pos3 = pos.reshape(t // tm, 1, 2 * tm)
    return pl.pallas_call(
        functools.partial(_slotmap_kernel, t),
        grid=(t // tm,),
        in_specs=[pl.BlockSpec((None, 1, 2 * tm), lambda i: (i, 0, 0), memory_space=pltpu.SMEM),
                  pl.BlockSpec(memory_space=pl.ANY)],
        out_specs=pl.BlockSpec(memory_space=pltpu.SMEM),
        out_shape=jax.ShapeDtypeStruct(init.shape, jnp.int32),
        scratch_shapes=[pltpu.SemaphoreType.DMA(())],
        compiler_params=_cparams(("arbitrary",)),
        name="slotmap",
    )(pos3, init)


def _row_copy(src_ref, src_row, dst_ref, dst_row, sem):
    return pltpu.make_async_copy(src_ref.at[pl.ds(src_row, 1)], dst_ref.at[pl.ds(dst_row, 1)], sem)


def _expert_kernel(n_tokens, te_ref, nu_ref, inv_prev_ref, inv_nxt_ref, h2_ref, wg_ref, wu_ref, wd_ref,
                   y2_ref, xbuf_ref, ybuf_ref, wgb_ref, wub_ref, wdb_ref, gsem, ssem):
    j = pl.program_id(0)
    n_used = nu_ref[0]
    tm = ybuf_ref.shape[1]
    slot = j % 2
    tok_mask = n_tokens - 1

    def gather_start(inv_ref, s):
        for r in range(tm):
            cp = _row_copy(h2_ref, inv_ref[0, r] & tok_mask, xbuf_ref.at[s], r, gsem.at[s])
            cp.start(priority=r % DMA_PRIORITIES)

    def gather_wait(s):
        pltpu.make_async_copy(h2_ref.at[pl.ds(0, tm)], xbuf_ref.at[s], gsem.at[s]).wait()

    def scatter_start(s):
        for r in range(tm):
            cp = _row_copy(ybuf_ref.at[s], r, y2_ref, inv_prev_ref[0, r], ssem.at[s])
            cp.start(priority=r % DMA_PRIORITIES)

    def scatter_wait(s):
        pltpu.make_async_copy(ybuf_ref.at[s], y2_ref.at[pl.ds(0, tm)], ssem.at[s]).wait()

    def spare_fill(k):
        return pltpu.make_async_copy(ybuf_ref.at[1], y2_ref.at[pl.ds(2 * n_tokens + k * tm, tm)], ssem.at[1])

    @pl.when(j == 0)
    def _():
        gather_start(inv_prev_ref, 0)
        ybuf_ref[1] = jnp.zeros(ybuf_ref.shape[1:], ybuf_ref.dtype)
        n_spare = (y2_ref.shape[0] - 2 * n_tokens) // tm
        for k in range(n_spare):
            spare_fill(k).start()
        for k in range(n_spare):
            spare_fill(k).wait()

    new_expert = jnp.logical_or(j == 0, te_ref[j] != te_ref[jnp.maximum(j - 1, 0)])

    @pl.when(jnp.logical_and(j < n_used, new_expert))
    def _():
        wgb_ref[...] = wg_ref[...].astype(BF16)
        wub_ref[...] = wu_ref[...].astype(BF16)
        wdb_ref[...] = wd_ref[...].astype(BF16)

    def tile_step(tiles_before):
        gather_wait(slot)
        xl, xr = [v.astype(BF16) for v in _unpack_bf16_pairs(xbuf_ref[slot])]
        gather_start(inv_nxt_ref, 1 - slot)
        if tiles_before >= 1:
            scatter_start(1 - slot)
        half = xl.shape[1]
        gate = _dot(xl, wgb_ref[0:half, :]) + _dot(xr, wgb_ref[half:, :])
        up = _dot(xl, wub_ref[0:half, :]) + _dot(xr, wub_ref[half:, :])
        y = _dot((_silu(gate) * up).astype(BF16), wdb_ref[...])
        if tiles_before >= 2:
            scatter_wait(slot)
        ybuf_ref[slot] = _pack_bf16_pairs(y)

    for before in (0, 1):
        pl.when(jnp.logical_and(j < n_used, j == before))(functools.partial(tile_step, before))
    pl.when(jnp.logical_and(j < n_used, j >= 2))(functools.partial(tile_step, 2))

    @pl.when(j == n_used)
    def _():
        gather_wait(slot)
        scatter_start(1 - slot)
        scatter_wait(slot)
        scatter_wait(1 - slot)


def _experts(tile_expert, n_used, inv, h2, w_gate, w_up, w_down, y2_rows):
    t = h2.shape[0]
    _, d, hdim = w_gate.shape
    assert h2.shape[1] == d // 2 and h2.dtype == jnp.int32, "rows arrive as packed bf16 column pairs"
    tm = TM_EXPERT
    n_tiles = inv.shape[0] // tm
    inv3 = inv.reshape(n_tiles, 1, tm)
    last_used = lambda j, nu: jnp.minimum(j, nu[0] - 1)
    grid_spec = pltpu.PrefetchScalarGridSpec(
        num_scalar_prefetch=2,
        grid=(n_tiles + 1,),
        in_specs=[pl.BlockSpec((None, 1, tm), lambda j, te, nu: (jnp.maximum(j - 1, 0), 0, 0),
                               memory_space=pltpu.SMEM),
                  pl.BlockSpec((None, 1, tm), lambda j, te, nu: (jnp.minimum(j + 1, n_tiles - 1), 0, 0),
                               memory_space=pltpu.SMEM),
                  pl.BlockSpec(memory_space=pl.ANY),
                  pl.BlockSpec((None, d, hdim), lambda j, te, nu: (te[last_used(j, nu)], 0, 0)),
                  pl.BlockSpec((None, d, hdim), lambda j, te, nu: (te[last_used(j, nu)], 0, 0)),
                  pl.BlockSpec((None, hdim, d), lambda j, te, nu: (te[last_used(j, nu)], 0, 0))],
        out_specs=pl.BlockSpec(memory_space=pl.ANY),
        scratch_shapes=[pltpu.VMEM((2, tm, d // 2), jnp.int32), pltpu.VMEM((2, tm, d // 2), jnp.int32),
                        pltpu.VMEM((d, hdim), BF16), pltpu.VMEM((d, hdim), BF16),
                        pltpu.VMEM((hdim, d), BF16),
                        pltpu.SemaphoreType.DMA((2,)), pltpu.SemaphoreType.DMA((2,))],
    )
    return pl.pallas_call(
        functools.partial(_expert_kernel, t),
        grid_spec=grid_spec,
        out_shape=jax.ShapeDtypeStruct((y2_rows, d // 2), jnp.int32),
        compiler_params=_cparams(("arbitrary",)),
        name="experts",
    )(tile_expert, n_used, inv3, inv3, h2, w_gate, w_up, w_down)


def _combine_kernel(y0_ref, y1_ref, rt_ref, x1_ref, g2_ref, lng_ref, lnb_ref, o_ref):
    rt = rt_ref[...]
    w0, w1 = rt[:, 4:5], rt[:, 5:6]
    l0, r0 = _unpack_bf16_pairs(y0_ref[...])
    l1, r1 = _unpack_bf16_pairs(y1_ref[...])
    y_moe = jnp.concatenate([w0 * l0 + w1 * l1, w0 * r0 + w1 * r1], axis=1)
    v = DEEPNORM_ALPHA * x1_ref[...] + g2_ref[...] * y_moe
    o_ref[...] = _layernorm(v) * lng_ref[...] + lnb_ref[...]


def _combine(y2, rt, x1, g2, ln_g, ln_b, seq):
    t, d = x1.shape
    tm = TM_COMBINE
    tiles_per_seq = seq // tm
    nt = t // tm
    return pl.pallas_call(
        _combine_kernel,
        grid=(nt,),
        in_specs=[pl.BlockSpec((tm, d // 2), lambda i: (i, 0)),
                  pl.BlockSpec((tm, d // 2), lambda i: (nt + i, 0)),
                  pl.BlockSpec((tm, LANES), lambda i: (i, 0)),
                  pl.BlockSpec((tm, d), lambda i: (i, 0)),
                  pl.BlockSpec((None, 1, d), lambda i: (i // tiles_per_seq, 0, 0)),
                  pl.BlockSpec((1, d), lambda i: (0, 0)),
                  pl.BlockSpec((1, d), lambda i: (0, 0))],
        out_specs=pl.BlockSpec((tm, d), lambda i: (i, 0)),
        out_shape=jax.ShapeDtypeStruct((t, d), F32),
        compiler_params=_cparams(("arbitrary",)),
        name="combine_postln",
    )(y2, y2, rt, x1, g2, ln_g, ln_b)


def _layer(x2, c_pad, batch, seq, w_ada, b_ada, w_in, conv_w, conv_b, a_log_f, a_log_b, dt_bias_f,
           dt_bias_b, d_skip, ssd_norm_g, w_ssd_out, pool_w, pool_scale, w_pool_out, gate_b, w_o,
           ln1_g, ln1_b, router_wg, router_bg, router_we, router_be, exp_w_gate, exp_w_up,
           exp_w_down, ln2_g, ln2_b):
    t, d = x2.shape
    assert t & (t - 1) == 0, "the slot map packs (slot, token) as slot * tokens + token"
    assert 2 * t >= 2 * TM_EXPERT, "the expert pipeline assumes at least two used tiles"
    d_inner = SSD_HEADS * SSD_HEAD_DIM
    conv_dim = d_inner + 2 * SSD_GROUPS * SSD_STATE
    pool_width = pool_scale.shape[0]

    mod = _ada(c_pad, w_ada, b_ada[None, :])[:batch]
    sh1, sc1, g1, sh2, sc2, g2 = [m[:, None, :] for m in jnp.split(mod, 6, axis=-1)]

    o_xbc = d_inner
    o_dt = o_xbc + conv_dim
    o_pool = o_dt + 2 * SSD_HEADS
    w_main = jnp.concatenate([w_in[:, :o_dt], w_in[:, o_pool:]], axis=1).astype(BF16)
    w_dt = jnp.pad(w_in[:, o_dt:o_pool], ((0, 0), (0, LANES - 2 * SSD_HEADS)))
    pool_col = o_dt
    gate_col = o_dt + pool_width
    big, dt_raw = _inproj(x2, sh1, sc1, w_main, w_dt, seq)

    xc = _conv(big, o_xbc, conv_w, conv_b[None, :], seq)

    pad_row = lambda f, b: jnp.pad(jnp.concatenate([f, b]), (0, LANES - 2 * SSD_HEADS))[None, :]
    alog_row = pad_row(a_log_f, a_log_b)
    bias_row = pad_row(dt_bias_f, dt_bias_b)
    dskip_row = jnp.repeat(d_skip, SSD_HEAD_DIM)[None, :]
    y_bwd = _ssd(xc, dt_raw, alog_row, bias_row, batch, reverse=True)
    yn = _ssd(xc, dt_raw, alog_row, bias_row, batch, reverse=False,
              final_inputs=(big, y_bwd, dskip_row, ssd_norm_g[None, :]))

    w_router = jnp.pad(jnp.concatenate([router_wg, router_we], axis=1),
                       ((0, 0), (0, LANES - MOE_GROUPS - MOE_EXPERTS)))
    r_bias = jnp.pad(jnp.concatenate([router_bg, router_be]), (0, LANES - MOE_GROUPS - MOE_EXPERTS))[None, :]
    x1, h2, logits = _mix(yn, big, x2, w_ssd_out.astype(BF16), pool_w.astype(BF16),
                          w_pool_out.astype(BF16), w_o.astype(BF16), w_router, gate_b[None, :],
                          pool_scale[None, :], ln1_g[None, :], ln1_b[None, :], r_bias, g1, sh2, sc2,
                          seq, pool_col, gate_col)

    rt, counts = _route(logits)

    tme = TM_EXPERT
    cnt = counts[0, MOE_GROUPS:MOE_GROUPS + MOE_EXPERTS].astype(jnp.int32)
    padded = ((cnt + tme - 1) // tme) * tme
    ends = jnp.cumsum(padded)
    off = ends - padded
    eid = rt[:, 0:2].astype(jnp.int32)
    pos = rt[:, 2:4].astype(jnp.int32) + jnp.sum(
        jnp.where(eid[:, :, None] == jnp.arange(MOE_EXPERTS, dtype=jnp.int32), off, 0), axis=-1)
    p_rows = 2 * t + MOE_EXPERTS * tme
    n_tiles = p_rows // tme
    tile_ends = ends // tme
    tile_expert = jnp.minimum(
        jnp.sum(jnp.arange(n_tiles + 1, dtype=jnp.int32)[:, None] >= tile_ends[None, :], axis=1),
        MOE_EXPERTS - 1).astype(jnp.int32)
    n_used = tile_ends[-1:].astype(jnp.int32)

    rows = jnp.arange(p_rows, dtype=jnp.int32)
    real_before = jnp.sum(jnp.where(rows[:, None] >= off[None, :], cnt[None, :], 0), axis=1)
    inv = _slotmap(pos, 2 * t + rows - real_before)
    y2 = _experts(tile_expert, n_used, inv, h2, exp_w_gate, exp_w_up, exp_w_down, p_rows)
    return _combine(y2, rt, x1, g2, ln2_g[None, :], ln2_b[None, :], seq)


def kernel(x, c, w_ada, b_ada, w_in, conv_w, conv_b, a_log_f, a_log_b, dt_bias_f, dt_bias_b, d_skip,
           ssd_norm_g, w_ssd_out, pool_w, pool_scale, w_pool_out, gate_b, w_o, ln1_g, ln1_b,
           router_wg, router_bg, router_we, router_be, exp_w_gate, exp_w_up, exp_w_down, ln2_g, ln2_b):
    batch, seq, d = x.shape
    x2 = x.reshape(batch * seq, d)
    c_pad = jnp.pad(c, ((0, 8 - batch), (0, 0)))
    params = (w_ada, b_ada, w_in, conv_w, conv_b, a_log_f, a_log_b, dt_bias_f, dt_bias_b, d_skip,
              ssd_norm_g, w_ssd_out, pool_w, pool_scale, w_pool_out, gate_b, w_o, ln1_g, ln1_b,
              router_wg, router_bg, router_we, router_be, exp_w_gate, exp_w_up, exp_w_down, ln2_g, ln2_b)
    for l in range(w_ada.shape[0]):
        x2 = _layer(x2, c_pad, batch, seq, *[p[l] for p in params])
    return x2.reshape(batch, seq, d)
```

```python
import functools

import jax
import jax.numpy as jnp
from jax import lax
from jax.experimental import pallas as pl
from jax.experimental.pallas import tpu as pltpu

F32 = jnp.float32
BF16 = jnp.bfloat16
HIGHEST = lax.Precision.HIGHEST

SSD_HEAD_DIM = 64
SSD_GROUPS = 8
SSD_HEADS_PER_GROUP = 4
SSD_HEADS = SSD_GROUPS * SSD_HEADS_PER_GROUP
SSD_STATE = 128
SSD_CONV = 5
SSD_CHUNK = 128
SSD_NORM_EPS = 1e-5
POOL_WINDOWS = (2, 4, 8, 16)
MOE_GROUPS = 4
MOE_EXPERTS_PER_GROUP = 8
MOE_EXPERTS = MOE_GROUPS * MOE_EXPERTS_PER_GROUP
DEPTH = 1
DEEPNORM_ALPHA = (2.0 * DEPTH) ** 0.25
LN_EPS = 1e-5

LANES = 128
HALO = 16
VMEM_LIMIT = 48 * 1024 * 1024
SLOTMAP_UNROLL = 8
DMA_PRIORITIES = 2

TM_INPROJ = 1024
TN_INPROJ = 2304
TM_CONV = 512
CONV_SUB = 128
TM_MIX = 256
POOL_SUB = 128
TM_ROUTE = 512
TM_SLOTMAP = 1024
TM_EXPERT = 256
TM_COMBINE = 512


def _dot(a, b):
    return jnp.dot(a, b, preferred_element_type=F32)


def _split_hi_lo(v):
    hi = v.astype(BF16)
    lo = (v - hi.astype(F32)).astype(BF16)
    return hi, lo


def _split3(v):
    hi = v.astype(BF16)
    r = v - hi.astype(F32)
    mid = r.astype(BF16)
    lo = (r - mid.astype(F32)).astype(BF16)
    return hi, mid, lo


def _dot3(a, b):
    a_hi, a_lo = _split_hi_lo(a)
    b_hi, b_lo = _split_hi_lo(b)
    return _dot(a_hi, b_hi) + _dot(a_lo, b_hi) + _dot(a_hi, b_lo)


def _layernorm(v):
    mu = jnp.mean(v, axis=-1, keepdims=True)
    vc = v - mu
    var = jnp.mean(vc * vc, axis=-1, keepdims=True)
    return vc * lax.rsqrt(var + LN_EPS)


def _silu(v):
    return v * jax.nn.sigmoid(v)


def _pack_bf16_pairs(v):
    m = v.shape[1] // 2
    hi = lax.bitcast_convert_type(v[:, :m].astype(BF16).astype(F32), jnp.int32)
    lo = lax.bitcast_convert_type(v[:, m:].astype(BF16).astype(F32), jnp.int32)
    return hi | lax.shift_right_logical(lo, 16)


def _unpack_bf16_pairs(p):
    hi = lax.bitcast_convert_type(p & jnp.int32(-65536), F32)
    lo = lax.bitcast_convert_type(lax.shift_left(p, 16), F32)
    return hi, lo


def _cparams(sem):
    return pltpu.CompilerParams(dimension_semantics=sem, vmem_limit_bytes=VMEM_LIMIT)


def _ada_kernel(c_ref, w_ref, b_ref, o_ref):
    o_ref[...] = jnp.dot(_silu(c_ref[...]), w_ref[...], precision=HIGHEST,
                         preferred_element_type=F32) + b_ref[...]


def _ada(c_pad, w, b):
    d, n = w.shape
    tn = 1024
    return pl.pallas_call(
        _ada_kernel,
        grid=(n // tn,),
        in_specs=[pl.BlockSpec((c_pad.shape[0], d), lambda j: (0, 0)),
                  pl.BlockSpec((d, tn), lambda j: (0, j)),
                  pl.BlockSpec((1, tn), lambda j: (0, j))],
        out_specs=pl.BlockSpec((c_pad.shape[0], tn), lambda j: (0, j)),
        out_shape=jax.ShapeDtypeStruct((c_pad.shape[0], n), F32),
        compiler_params=_cparams(("arbitrary",)),
        name="ada_mod",
    )(c_pad, w, b)


def _inproj_kernel(x_ref, sh_ref, sc_ref, w_ref, wdt_ref, o_ref, dt_ref, h_ref):
    @pl.when(pl.program_id(1) == 0)
    def _():
        h = _layernorm(x_ref[...]) * (1.0 + sc_ref[...]) + sh_ref[...]
        h_ref[...] = h.astype(BF16)
        dt_ref[...] = _dot3(h, wdt_ref[...])

    o_ref[...] = _dot(h_ref[...], w_ref[...]).astype(BF16)


def _inproj(x2, sh, sc, w_main, w_dt, seq):
    t, d = x2.shape
    n = w_main.shape[1]
    tm, tn = TM_INPROJ, TN_INPROJ
    tiles_per_seq = seq // tm
    return pl.pallas_call(
        _inproj_kernel,
        grid=(t // tm, n // tn),
        in_specs=[pl.BlockSpec((tm, d), lambda i, j: (i, 0)),
                  pl.BlockSpec((None, 1, d), lambda i, j: (i // tiles_per_seq, 0, 0)),
                  pl.BlockSpec((None, 1, d), lambda i, j: (i // tiles_per_seq, 0, 0)),
                  pl.BlockSpec((d, tn), lambda i, j: (0, j)),
                  pl.BlockSpec((d, LANES), lambda i, j: (0, 0))],
        out_specs=[pl.BlockSpec((tm, tn), lambda i, j: (i, j)),
                   pl.BlockSpec((tm, LANES), lambda i, j: (i, 0))],
        out_shape=[jax.ShapeDtypeStruct((t, n), BF16),
                   jax.ShapeDtypeStruct((t, LANES), F32)],
        scratch_shapes=[pltpu.VMEM((tm, d), BF16)],
        compiler_params=_cparams(("arbitrary", "arbitrary")),
        name="ln_inproj",
    )(x2, sh, sc, w_main, w_dt)


def _conv_kernel(tiles_per_seq, cur_ref, prev_ref, next_ref, w_ref, b_ref, o_ref, ext_ref):
    i = pl.program_id(0)
    tm = cur_ref.shape[0]
    first = (i % tiles_per_seq) == 0
    last = (i % tiles_per_seq) == tiles_per_seq - 1
    ext_ref[0:HALO, :] = jnp.where(first, jnp.zeros_like(prev_ref), prev_ref[...])
    ext_ref[HALO:HALO + tm, :] = cur_ref[...]
    ext_ref[HALO + tm:, :] = jnp.where(last, jnp.zeros_like(next_ref), next_ref[...])
    pad = SSD_CONV // 2
    sub = CONV_SUB
    win = sub + 2 * HALO
    ri = lax.broadcasted_iota(jnp.int32, (sub, win), 0)
    ci = lax.broadcasted_iota(jnp.int32, (sub, win), 1)
    picks = [jnp.where(ci == ri + HALO + k - pad, 1.0, 0.0).astype(BF16) for k in range(SSD_CONV)]
    for m in range(tm // sub):
        window = ext_ref[m * sub:m * sub + win, :]
        acc = b_ref[...] + w_ref[pad:pad + 1, :] * ext_ref[HALO + m * sub:HALO + (m + 1) * sub, :].astype(F32)
        for k in range(SSD_CONV):
            if k != pad:
                acc = acc + w_ref[k:k + 1, :] * _dot(picks[k], window)
        o_ref[m * sub:(m + 1) * sub, :] = _silu(acc).astype(BF16)


def _conv(big, col_off, conv_w, conv_b, seq):
    t = big.shape[0]
    cdim = conv_w.shape[1]
    tm, tc = TM_CONV, 1024
    tiles_per_seq = seq // tm
    cb0 = col_off // tc
    hb = tm // HALO
    nhb = t // HALO
    return pl.pallas_call(
        functools.partial(_conv_kernel, tiles_per_seq),
        grid=(t // tm, cdim // tc),
        in_specs=[pl.BlockSpec((tm, tc), lambda i, j: (i, cb0 + j)),
                  pl.BlockSpec((HALO, tc), lambda i, j: (jnp.maximum(i * hb - 1, 0), cb0 + j)),
                  pl.BlockSpec((HALO, tc), lambda i, j: (jnp.minimum((i + 1) * hb, nhb - 1), cb0 + j)),
                  pl.BlockSpec((SSD_CONV, tc), lambda i, j: (0, j)),
                  pl.BlockSpec((1, tc), lambda i, j: (0, j))],
        out_specs=pl.BlockSpec((tm, tc), lambda i, j: (i, j)),
        out_shape=jax.ShapeDtypeStruct((t, cdim), BF16),
        scratch_shapes=[pltpu.VMEM((tm + 2 * HALO, tc), BF16)],
        compiler_params=_cparams(("arbitrary", "arbitrary")),
        name="conv_silu",
    )(big, big, big, conv_w, conv_b)


def _tri_cumsum(tri, v):
    hi, mid, lo = _split3(v)
    return _dot(tri, hi) + _dot(tri, mid) + _dot(tri, lo)


def _expand_heads(v, e_ref):
    hi, lo = _split_hi_lo(v)
    return _dot(jnp.concatenate([hi, lo], axis=1), e_ref[...])


def _state_step(g, x_g, b_ref, c_ref, st_ref, w_exp, dec_exp, end_row):
    gw = x_g.shape[1]
    cols = slice(g * gw, (g + 1) * gw)
    bg = b_ref[:, g * SSD_STATE:(g + 1) * SSD_STATE]
    cg = c_ref[:, g * SSD_STATE:(g + 1) * SSD_STATE]
    h_in = st_ref[g]
    y_off = _dot(cg, h_in.astype(BF16)) * dec_exp[:, cols]
    xw = (x_g * w_exp[:, cols]).astype(BF16)
    s_new = lax.dot_general(bg, xw, (((0,), (0,)), ((), ())), preferred_element_type=F32)
    st_ref[g] = h_in * dec_exp[end_row:end_row + 1, cols] + s_new
    return y_off


def _ssd_bwd_kernel(x_ref, b_ref, c_ref, dt_ref, alog_ref, bias_ref, e64_ref, o_ref, st_ref):
    q = x_ref.shape[0]
    gw = SSD_HEADS_PER_GROUP * SSD_HEAD_DIM

    @pl.when(pl.program_id(1) == 0)
    def _():
        st_ref[...] = jnp.zeros_like(st_ref)

    dt = jax.nn.softplus(dt_ref[...] + bias_ref[...])
    da = dt * (-jnp.exp(alog_ref[...]))
    li = lax.broadcasted_iota(jnp.int32, (q, q), 0)
    si = lax.broadcasted_iota(jnp.int32, (q, q), 1)
    cs = _tri_cumsum(jnp.where(si >= li, 1.0, 0.0).astype(BF16), da)
    w_exp = _expand_heads(dt * jnp.exp(cs[0:1, :] - cs), e64_ref)
    dec_exp = _expand_heads(jnp.exp(cs), e64_ref)
    for g in range(SSD_GROUPS):
        cols = slice(g * gw, (g + 1) * gw)
        x_g = x_ref[:, cols].astype(F32)
        o_ref[:, cols] = _state_step(g, x_g, b_ref, c_ref, st_ref, w_exp, dec_exp, 0).astype(BF16)


def _ssd_fwd_kernel(x_ref, b_ref, c_ref, dt_ref, alog_ref, bias_ref, e64_ref, ecs_ref, z_ref,
                    yb_ref, dskip_ref, ng_ref, o_ref, st_ref):
    q = x_ref.shape[0]
    p = SSD_HEAD_DIM
    hpg = SSD_HEADS_PER_GROUP
    gw = hpg * p

    @pl.when(pl.program_id(1) == 0)
    def _():
        st_ref[...] = jnp.zeros_like(st_ref)

    dt = jax.nn.softplus(dt_ref[...] + bias_ref[...])
    da = dt * (-jnp.exp(alog_ref[...]))
    li = lax.broadcasted_iota(jnp.int32, (q, q), 0)
    si = lax.broadcasted_iota(jnp.int32, (q, q), 1)
    before = si <= li
    after = si >= li
    lane = lax.broadcasted_iota(jnp.int32, (q, LANES), 1)
    tri2 = jnp.concatenate([jnp.where(before, 1.0, 0.0), jnp.where(after, 1.0, 0.0)], axis=1).astype(BF16)
    da2 = jnp.concatenate([jnp.where(lane < SSD_HEADS, da, 0.0),
                           jnp.where((lane >= SSD_HEADS) & (lane < 2 * SSD_HEADS), da, 0.0)], axis=0)
    cs = _tri_cumsum(tri2, da2)
    csp_t = (cs - jnp.log(dt)).T
    hi, mid, lo = _split3(cs)
    nhd = 2 * SSD_HEADS
    v_cs = jnp.concatenate([hi[:, :nhd].astype(F32), mid[:, :nhd].astype(F32), lo[:, :nhd].astype(F32),
                            jnp.zeros((q, nhd), F32)], axis=1).astype(BF16)
    to_end = jnp.where(lane < SSD_HEADS, cs[q - 1:q, :] - cs, 0.0)
    w_exp = _expand_heads(dt * jnp.exp(to_end), e64_ref)
    dec_exp = _expand_heads(jnp.exp(cs), e64_ref)
    half =lax.broadcasted_iota(jnp.int32, (q, 2 * p), 1) < p

    for g in range(SSD_GROUPS):
        cols = slice(g * gw, (g + 1) * gw)
        bg = b_ref[:, g * SSD_STATE:(g + 1) * SSD_STATE]
        cg = c_ref[:, g * SSD_STATE:(g + 1) * SSD_STATE]
        cb = lax.dot_general(cg, bg, (((1,), (1,)), ((), ())), preferred_element_type=F32)
        colb = _dot(v_cs, ecs_ref[:, g * 2 * hpg * q:(g + 1) * 2 * hpg * q])
        x_g = x_ref[:, cols].astype(F32)
        ms = []
        for r in range(hpg):
            hh = g * hpg + r
            seg_f = colb[:, r * q:(r + 1) * q] - csp_t[hh:hh + 1, :]
            seg_b = colb[:, (hpg + r) * q:(hpg + r + 1) * q] - csp_t[SSD_HEADS + hh:SSD_HEADS + hh + 1, :]
            decay = jnp.exp(jnp.where(before, seg_f, -jnp.inf)) + jnp.exp(jnp.where(after, seg_b, -jnp.inf))
            ms.append((cb * decay).astype(BF16))
        ys = []
        for pr in range(hpg // 2):
            xp = x_g[:, 2 * pr * p:2 * (pr + 1) * p]
            rhs = jnp.concatenate([jnp.where(half, xp, 0.0), jnp.where(half, 0.0, xp)], axis=0).astype(BF16)
            ys.append(_dot(jnp.concatenate([ms[2 * pr], ms[2 * pr + 1]], axis=1), rhs))
        y_g = jnp.concatenate(ys, axis=1)
        y_g = y_g + _state_step(g, x_g, b_ref, c_ref, st_ref, w_exp, dec_exp, q - 1)
        y_g = y_g + yb_ref[:, cols].astype(F32) + x_g * dskip_ref[:, cols]
        yg = y_g * _silu(z_ref[:, cols].astype(F32))
        ms_g = jnp.mean(yg * yg, axis=-1, keepdims=True)
        o_ref[:, cols] = (yg * lax.rsqrt(ms_g + SSD_NORM_EPS) * ng_ref[:, cols]).astype(BF16)


def _head_expansion(col0):
    j = jnp.arange(2 * LANES)[:, None]
    c = jnp.arange(SSD_HEADS * SSD_HEAD_DIM)[None, :]
    return ((j % LANES - col0) == c // SSD_HEAD_DIM).astype(BF16)


def _score_expansion():
    nhd = 2 * SSD_HEADS
    row = jnp.arange(4 * nhd)[:, None]
    col = jnp.arange(nhd * SSD_CHUNK)[None, :]
    j = row % nhd
    head = j % SSD_HEADS
    blk = ((head // SSD_HEADS_PER_GROUP) * 2 + j // SSD_HEADS) * SSD_HEADS_PER_GROUP + head % SSD_HEADS_PER_GROUP
    return ((row < 3 * nhd) & (blk == col // SSD_CHUNK)).astype(BF16)


def _ssd(xc, dt_raw, alog_row, bias_row, batch, reverse, final_inputs=None):
    t = xc.shape[0]
    q = SSD_CHUNK
    nc = t // batch // q
    d_inner = SSD_HEADS * SSD_HEAD_DIM
    bcw = SSD_GROUPS * SSD_STATE

    def row(b, c):
        return b * nc + ((nc - 1 - c) if reverse else c)

    full = lambda a: pl.BlockSpec(a.shape, lambda b, c: (0,) * a.ndim)
    e64 = _head_expansion(SSD_HEADS if reverse else 0)
    in_specs = [pl.BlockSpec((q, d_inner), lambda b, c: (row(b, c), 0)),
                pl.BlockSpec((q, bcw), lambda b, c: (row(b, c), d_inner // bcw)),
                pl.BlockSpec((q, bcw), lambda b, c: (row(b, c), d_inner // bcw + 1)),
                pl.BlockSpec((q, LANES), lambda b, c: (row(b, c), 0)),
                full(alog_row), full(bias_row), full(e64)]
    args = [xc, xc, xc, dt_raw, alog_row, bias_row, e64]
    if reverse:
        body = _ssd_bwd_kernel
    else:
        body = _ssd_fwd_kernel
        big, y_bwd, dskip_row, ng_row = final_inputs
        ecs = _score_expansion()
        in_specs += [full(ecs),
                     pl.BlockSpec((q, d_inner), lambda b, c: (row(b, c), 0)),
                     pl.BlockSpec((q, d_inner), lambda b, c: (row(b, c), 0)),
                     full(dskip_row), full(ng_row)]
        args += [ecs, big, y_bwd, dskip_row, ng_row]
    return pl.pallas_call(
        body,
        grid=(batch, nc),
        in_specs=in_specs,
        out_specs=pl.BlockSpec((q, d_inner), lambda b, c: (row(b, c), 0)),
        out_shape=jax.ShapeDtypeStruct((t, d_inner), BF16),
        scratch_shapes=[pltpu.VMEM((SSD_GROUPS, SSD_STATE, SSD_HEADS_PER_GROUP * SSD_HEAD_DIM), F32)],
        compiler_params=_cparams(("arbitrary", "arbitrary")),
        name="ssd_bwd" if reverse else "ssd_fwd",
    )(*args)


def _mix_kernel(tiles_per_seq, seq, yn_ref, u_ref, up_ref, un_ref, ga_ref, gb_ref, x_ref,
                wssd_ref, pw_ref, wpo_ref, wo_ref, wr_ref,
                gateb_ref, pscale_ref, ln1g_ref, ln1b_ref, rb_ref, g1_ref, sh2_ref, sc2_ref,
                x1_ref, h2_ref, lg_ref, ext_ref):
    i = pl.program_id(0)
    tm, width = u_ref.shape
    first = (i % tiles_per_seq) == 0
    last = (i % tiles_per_seq) == tiles_per_seq - 1
    u = u_ref[...].astype(F32)
    ext_ref[0:HALO, :] = jnp.where(first, jnp.zeros_like(up_ref), up_ref[...])
    ext_ref[HALO:HALO + tm, :] = u_ref[...]
    ext_ref[HALO + tm:, :] = jnp.where(last, jnp.zeros_like(un_ref), un_ref[...])
    tpos = (i % tiles_per_seq) * tm + lax.broadcasted_iota(jnp.int32, (tm, 1), 0)
    gd = width // len(POOL_WINDOWS)
    sub = POOL_SUB
    win = sub + 2 * HALO
    ri = lax.broadcasted_iota(jnp.int32, (sub, win), 0) + HALO
    ci = lax.broadcasted_iota(jnp.int32, (sub, win), 1)
    mixed = []
    for gi, w in enumerate(POOL_WINDOWS):
        cols = slice(gi * gd, (gi + 1) * gd)
        band = jnp.where((ci >= ri - w // 2) & (ci < ri + w // 2), 1.0, 0.0).astype(BF16)
        s = jnp.concatenate([_dot(band, ext_ref[m * sub:m * sub + win, cols]) for m in range(tm // sub)], axis=0)
        cnt = (jnp.minimum(tpos + w // 2, seq) - jnp.maximum(tpos - w // 2, 0)).astype(F32)
        diff = s / cnt - u[:, cols]
        mixed.append(_dot(diff.astype(BF16), pw_ref[gi]))
    mixed = jnp.concatenate(mixed, axis=1) * pscale_ref[...]
    y_pool = _dot(mixed.astype(BF16), wpo_ref[...])
    y_ssd = _dot(yn_ref[...], wssd_ref[...])
    d = y_ssd.shape[1]
    g_ssd = jax.nn.sigmoid(ga_ref[...].astype(F32) + gateb_ref[:, 0:d])
    g_pool = jax.nn.sigmoid(gb_ref[...].astype(F32) + gateb_ref[:, d:2 * d])
    mix = _dot((g_ssd * y_ssd + g_pool * y_pool).astype(BF16), wo_ref[...])
    x1 = _layernorm(DEEPNORM_ALPHA * x_ref[...] + g1_ref[...] * mix) * ln1g_ref[...] + ln1b_ref[...]
    x1_ref[...] = x1
    h2 = _layernorm(x1) * (1.0 + sc2_ref[...]) + sh2_ref[...]
    h2_ref[...] = _pack_bf16_pairs(h2)
    lg_ref[...] = _dot3(h2, wr_ref[...]) + rb_ref[...]


def _mix(yn, big, x2, w_ssd_out, pool_w, w_pool_out, w_o, w_router, gate_b, pool_scale, ln1_g, ln1_b,
         r_bias, g1, sh2, sc2, seq, pool_col, gate_col):
    t, d = x2.shape
    tm = TM_MIX
    tiles_per_seq = seq // tm
    hb = tm // HALO
    nhb = t // HALO
    pcb = pool_col // d
    gcb = gate_col // d
    full = lambda a: pl.BlockSpec(a.shape, lambda i: (0,) * a.ndim)
    per_batch = pl.BlockSpec((None, 1, d), lambda i: (i // tiles_per_seq, 0, 0))
    return pl.pallas_call(
        functools.partial(_mix_kernel, tiles_per_seq, seq),
        grid=(t // tm,),
        in_specs=[pl.BlockSpec((tm, yn.shape[1]), lambda i: (i, 0)),
                  pl.BlockSpec((tm, d), lambda i: (i, pcb)),
                  pl.BlockSpec((HALO, d), lambda i: (jnp.maximum(i * hb - 1, 0), pcb)),
                  pl.BlockSpec((HALO, d), lambda i: (jnp.minimum((i + 1) * hb, nhb - 1), pcb)),
                  pl.BlockSpec((tm, d), lambda i: (i, gcb)),
                  pl.BlockSpec((tm, d), lambda i: (i, gcb + 1)),
                  pl.BlockSpec((tm, d), lambda i: (i, 0)),
                  full(w_ssd_out), full(pool_w), full(w_pool_out), full(w_o), full(w_router),
                  full(gate_b), full(pool_scale), full(ln1_g), full(ln1_b), full(r_bias),
                  per_batch, per_batch, per_batch],
        out_specs=[pl.BlockSpec((tm, d), lambda i: (i, 0)),
                   pl.BlockSpec((tm, d // 2), lambda i: (i, 0)),
                   pl.BlockSpec((tm, LANES), lambda i: (i, 0))],
        out_shape=[jax.ShapeDtypeStruct((t, d), F32),
                   jax.ShapeDtypeStruct((t, d // 2), jnp.int32),
                   jax.ShapeDtypeStruct((t, LANES), F32)],
        scratch_shapes=[pltpu.VMEM((tm + 2 * HALO, d), BF16)],
        compiler_params=_cparams(("arbitrary",)),
        name="mix_postln",
    )(yn, big, big, big, big, big, x2, w_ssd_out, pool_w, w_pool_out, w_o, w_router,
      gate_b, pool_scale, ln1_g, ln1_b, r_bias, g1, sh2, sc2)


def _route_kernel(lg_ref, rt_ref, cnt_ref, carry_ref):
    @pl.when(pl.program_id(0) == 0)
    def _():
        carry_ref[...] = jnp.zeros_like(carry_ref)

    lg = lg_ref[...]
    tm = lg.shape[0]
    lane = lax.broadcasted_iota(jnp.int32, lg.shape, 1).astype(F32)
    neg = -jnp.inf
    big_lane = float(LANES)
    gl = jnp.where(lane < MOE_GROUPS, lg, neg)
    gmax = jnp.max(gl, axis=-1, keepdims=True)
    g_w = 1.0 / jnp.sum(jnp.exp(gl - gmax), axis=-1, keepdims=True)
    g_idx = jnp.min(jnp.where(gl == gmax, lane, big_lane), axis=-1, keepdims=True)
    lo = MOE_GROUPS + MOE_EXPERTS_PER_GROUP * g_idx
    el = jnp.where((lane >= lo) & (lane < lo + MOE_EXPERTS_PER_GROUP), lg, neg)
    m1 = jnp.max(el, axis=-1, keepdims=True)
    i1 = jnp.min(jnp.where(el == m1, lane, big_lane), axis=-1, keepdims=True)
    el2 = jnp.where(lane == i1, neg, el)
    m2 = jnp.max(el2, axis=-1, keepdims=True)
    i2 = jnp.min(jnp.where(el2 == m2, lane, big_lane), axis=-1, keepdims=True)
    e = jnp.exp(m2 - m1)
    w1 = g_w / (1.0 + e)
    w2 = g_w * e / (1.0 + e)
    onehot = jnp.where((lane == i1) | (lane == i2), 1.0, 0.0)
    ri = lax.broadcasted_iota(jnp.int32, (tm, tm), 0)
    ci = lax.broadcasted_iota(jnp.int32, (tm, tm), 1)
    earlier = jnp.where(ci < ri, 1.0, 0.0).astype(BF16)
    rank = _dot(earlier, onehot.astype(BF16)) + carry_ref[...]
    r1 = jnp.sum(jnp.where(lane == i1, rank, 0.0), axis=-1, keepdims=True)
    r2 = jnp.sum(jnp.where(lane == i2, rank, 0.0), axis=-1, keepdims=True)
    carry_ref[...] = carry_ref[...] + jnp.sum(onehot, axis=0, keepdims=True)
    cnt_ref[...] = carry_ref[...]
    out = jnp.where(lane == 0, i1 - MOE_GROUPS, 0.0)
    out = jnp.where(lane == 1, i2 - MOE_GROUPS, out)
    out = jnp.where(lane == 2, r1, out)
    out = jnp.where(lane == 3, r2, out)
    out = jnp.where(lane == 4, w1, out)
    out = jnp.where(lane == 5, w2, out)
    rt_ref[...] = out


def _route(logits):
    t = logits.shape[0]
    tm = TM_ROUTE
    return pl.pallas_call(
        _route_kernel,
        grid=(t // tm,),
        in_specs=[pl.BlockSpec((tm, LANES), lambda i: (i, 0))],
        out_specs=[pl.BlockSpec((tm, LANES), lambda i: (i, 0)),
                   pl.BlockSpec((1, LANES), lambda i: (0, 0))],
        out_shape=[jax.ShapeDtypeStruct((t, LANES), F32),
                   jax.ShapeDtypeStruct((1, LANES), F32)],
        scratch_shapes=[pltpu.VMEM((1, LANES), F32)],
        compiler_params=_cparams(("arbitrary",)),
        name="route",
    )(logits)


def _slotmap_kernel(pos_ref, init_ref, inv_ref, sem):
    i = pl.program_id(0)
    tm = pos_ref.shape[1] // 2

    @pl.when(i == 0)
    def _():
        cp = pltpu.make_async_copy(init_ref, inv_ref, sem)
        cp.start()
        cp.wait()

    def body(r, carry):
        tok = i * tm + r
        inv_ref[pos_ref[0, 2 * r]] = tok
        inv_ref[pos_ref[0, 2 * r + 1]] = tok
        return carry

    lax.fori_loop(0, tm, body, 0, unroll=SLOTMAP_UNROLL)


def _slotmap(pos, init):
    t = pos.shape[0]
    tm = TM_SLOTMAP
    pos3 = pos.reshape(t // tm, 1, 2 * tm)
    return pl.pallas_call(
        _slotmap_kernel,
        grid=(t // tm,),
        in_specs=[pl.BlockSpec((None, 1, 2 * tm), lambda i: (i, 0, 0), memory_space=pltpu.SMEM),
                  pl.BlockSpec(memory_space=pl.ANY)],
        out_specs=pl.BlockSpec(memory_space=pltpu.SMEM),
        out_shape=jax.ShapeDtypeStruct(init.shape, jnp.int32),
        scratch_shapes=[pltpu.SemaphoreType.DMA(())],
        compiler_params=_cparams(("arbitrary",)),
        name="slotmap",
    )(pos3, init)


def _row_copy(src_ref, src_row, dst_ref, dst_row, sem):
    return pltpu.make_async_copy(src_ref.at[pl.ds(src_row, 1)], dst_ref.at[pl.ds(dst_row, 1)], sem)


def _expert_kernel(te_ref, nu_ref, inv_cur_ref, inv_nxt_ref, h2_ref, wg_ref, wu_ref, wd_ref,
                   y_ref, xbuf_ref, gsem):
    del te_ref
    j = pl.program_id(0)
    n_used = nu_ref[0]
    tm = y_ref.shape[0]
    slot = j % 2

    def gather_start(inv_ref, s):
        for r in range(tm):
            cp = _row_copy(h2_ref, inv_ref[0, r], xbuf_ref.at[s], r, gsem.at[s])
            cp.start(priority=r % DMA_PRIORITIES)

    def gather_wait(s):
        pltpu.make_async_copy(h2_ref.at[pl.ds(0, tm)], xbuf_ref.at[s], gsem.at[s]).wait()

    @pl.when(j == 0)
    def _():
        gather_start(inv_cur_ref, 0)

    @pl.when(j < n_used)
    def _():
        gather_wait(slot)
        xl, xr = [v.astype(BF16) for v in _unpack_bf16_pairs(xbuf_ref[slot])]
        gather_start(inv_nxt_ref, 1 - slot)
        half = xl.shape[1]
        gate = _dot(xl, wg_ref[0:half, :].astype(BF16)) + _dot(xr, wg_ref[half:, :].astype(BF16))
        up = _dot(xl, wu_ref[0:half, :].astype(BF16)) + _dot(xr, wu_ref[half:, :].astype(BF16))
        y = _dot((_silu(gate) * up).astype(BF16), wd_ref[...].astype(BF16))
        y_ref[...] = _pack_bf16_pairs(y)

    @pl.when(j >= n_used)
    def _():
        y_ref[...] = jnp.zeros_like(y_ref)

    @pl.when(j == n_used)
    def _():
        gather_wait(slot)


def _experts(tile_expert, n_used, inv, h2, w_gate, w_up, w_down):
    _, d, hdim = w_gate.shape
    assert h2.shape[1] == d // 2 and h2.dtype == jnp.int32, "rows arrive as packed bf16 column pairs"
    tm = TM_EXPERT
    n_tiles = inv.shape[0] // tm
    inv3 = inv.reshape(n_tiles, 1, tm)
    last_used = lambda j, nu: jnp.minimum(j, nu[0] - 1)
    grid_spec = pltpu.PrefetchScalarGridSpec(
        num_scalar_prefetch=2,
        grid=(n_tiles + 1,),
        in_specs=[pl.BlockSpec((None, 1, tm), lambda j, te, nu: (jnp.minimum(j, n_tiles - 1), 0, 0),
                               memory_space=pltpu.SMEM),
                  pl.BlockSpec((None, 1, tm), lambda j, te, nu: (jnp.minimum(j + 1, n_tiles - 1), 0, 0),
                               memory_space=pltpu.SMEM),
                  pl.BlockSpec(memory_space=pl.ANY),
                  pl.BlockSpec((None, d, hdim), lambda j, te, nu: (te[last_used(j, nu)], 0, 0)),
                  pl.BlockSpec((None, d, hdim), lambda j, te, nu: (te[last_used(j, nu)], 0, 0)),
                  pl.BlockSpec((None, hdim, d), lambda j, te, nu: (te[last_used(j, nu)], 0, 0))],
        out_specs=pl.BlockSpec((tm, d // 2), lambda j, te, nu: (j, 0)),
        scratch_shapes=[pltpu.VMEM((2, tm, d // 2), jnp.int32), pltpu.SemaphoreType.DMA((2,))],
    )
    return pl.pallas_call(
        _expert_kernel,
        grid_spec=grid_spec,
        out_shape=jax.ShapeDtypeStruct(((n_tiles + 1) * tm, d // 2), jnp.int32),
        compiler_params=_cparams(("arbitrary",)),
        name="experts",
    )(tile_expert, n_used, inv3, inv3, h2, w_gate, w_up, w_down)


def _combine_kernel(pos_cur_ref, pos_nxt_ref, y_ref, rt_ref, x1_ref, g2_ref, lng_ref, lnb_ref, o_ref,
                    buf_ref, sem):
    i = pl.program_id(0)
    tm = x1_ref.shape[0]
    slot = i % 2

    def gather_start(pos_ref, s):
        for r in range(tm):
            for k in range(2):
                cp = _row_copy(y_ref, pos_ref[0, 2 * r + k], buf_ref.at[s, k], r, sem.at[s])
                cp.start(priority=k % DMA_PRIORITIES)

    def gather_wait(s):
        for k in range(2):
            pltpu.make_async_copy(y_ref.at[pl.ds(0, tm)], buf_ref.at[s, k], sem.at[s]).wait()

    @pl.when(i == 0)
    def _():
        gather_start(pos_cur_ref, 0)

    gather_wait(slot)
    l0, r0 = _unpack_bf16_pairs(buf_ref[slot, 0])
    l1, r1 = _unpack_bf16_pairs(buf_ref[slot, 1])
    gather_start(pos_nxt_ref, 1 - slot)
    rt = rt_ref[...]
    w0, w1 = rt[:, 4:5], rt[:, 5:6]
    y_moe = jnp.concatenate([w0 * l0 + w1 * l1, w0 * r0 + w1 * r1], axis=1)
    v = DEEPNORM_ALPHA * x1_ref[...] + g2_ref[...] * y_moe
    o_ref[...] = _layernorm(v) * lng_ref[...] + lnb_ref[...]

    @pl.when(i == pl.num_programs(0) - 1)
    def _():
        gather_wait(1 - slot)


def _combine(pos, y_sorted, rt, x1, g2, ln_g, ln_b, seq):
    t, d = x1.shape
    tm = TM_COMBINE
    tiles_per_seq = seq // tm
    nt = t // tm
    pos3 = pos.reshape(nt, 1, 2 * tm)
    return pl.pallas_call(
        _combine_kernel,
        grid=(nt,),
        in_specs=[pl.BlockSpec((None, 1, 2 * tm), lambda i: (i, 0, 0), memory_space=pltpu.SMEM),
                  pl.BlockSpec((None, 1, 2 * tm), lambda i: (jnp.minimum(i + 1, nt - 1), 0, 0),
                               memory_space=pltpu.SMEM),
                  pl.BlockSpec(memory_space=pl.ANY),
                  pl.BlockSpec((tm, LANES), lambda i: (i, 0)),
                  pl.BlockSpec((tm, d), lambda i: (i, 0)),
                  pl.BlockSpec((None, 1, d), lambda i: (i // tiles_per_seq, 0, 0)),
                  pl.BlockSpec((1, d), lambda i: (0, 0)),
                  pl.BlockSpec((1, d), lambda i: (0, 0))],
        out_specs=pl.BlockSpec((tm, d), lambda i: (i, 0)),
        out_shape=jax.ShapeDtypeStruct((t, d), F32),
        scratch_shapes=[pltpu.VMEM((2, 2, tm, d // 2), jnp.int32), pltpu.SemaphoreType.DMA((2,))],
        compiler_params=_cparams(("arbitrary",)),
        name="combine_postln",
    )(pos3, pos3, y_sorted, rt, x1, g2, ln_g, ln_b)


def _layer(x2, c_pad, batch, seq, w_ada, b_ada, w_in, conv_w, conv_b, a_log_f, a_log_b, dt_bias_f,
           dt_bias_b, d_skip, ssd_norm_g, w_ssd_out, pool_w, pool_scale, w_pool_out, gate_b, w_o,
           ln1_g, ln1_b, router_wg, router_bg, router_we, router_be, exp_w_gate, exp_w_up,
           exp_w_down, ln2_g, ln2_b):
    t, d = x2.shape
    d_inner = SSD_HEADS * SSD_HEAD_DIM
    conv_dim = d_inner + 2 * SSD_GROUPS * SSD_STATE
    pool_width = pool_scale.shape[0]

    mod = _ada(c_pad, w_ada, b_ada[None, :])[:batch]
    sh1, sc1, g1, sh2, sc2, g2 = [m[:, None, :] for m in jnp.split(mod, 6, axis=-1)]

    o_xbc = d_inner
    o_dt = o_xbc + conv_dim
    o_pool = o_dt + 2 * SSD_HEADS
    w_in_b = w_in.astype(BF16)
    w_main = jnp.concatenate([w_in_b[:, :o_dt], w_in_b[:, o_pool:]], axis=1)
    w_dt = jnp.pad(w_in[:, o_dt:o_pool], ((0, 0), (0, LANES - 2 * SSD_HEADS)))
    pool_col = o_dt
    gate_col = o_dt + pool_width
    big, dt_raw = _inproj(x2, sh1, sc1, w_main, w_dt, seq)

    xc = _conv(big, o_xbc, conv_w, conv_b[None, :], seq)

    pad_row = lambda f, b: jnp.pad(jnp.concatenate([f, b]), (0, LANES - 2 * SSD_HEADS))[None, :]
    alog_row = pad_row(a_log_f, a_log_b)
    bias_row = pad_row(dt_bias_f, dt_bias_b)
    dskip_row = jnp.repeat(d_skip, SSD_HEAD_DIM)[None, :]
    y_bwd = _ssd(xc, dt_raw, alog_row, bias_row, batch, reverse=True)
    yn = _ssd(xc, dt_raw, alog_row, bias_row, batch, reverse=False,
              final_inputs=(big, y_bwd, dskip_row, ssd_norm_g[None, :]))

    w_router = jnp.pad(jnp.concatenate([router_wg, router_we], axis=1),
                       ((0, 0), (0, LANES - MOE_GROUPS - MOE_EXPERTS)))
    r_bias = jnp.pad(jnp.concatenate([router_bg, router_be]), (0, LANES - MOE_GROUPS - MOE_EXPERTS))[None, :]
    x1, h2, logits = _mix(yn, big, x2, w_ssd_out.astype(BF16), pool_w.astype(BF16),
                          w_pool_out.astype(BF16), w_o.astype(BF16), w_router, gate_b[None, :],
                          pool_scale[None, :], ln1_g[None, :], ln1_b[None, :], r_bias, g1, sh2, sc2,
                          seq, pool_col, gate_col)

    rt, counts = _route(logits)

    tme = TM_EXPERT
    cnt = counts[0, MOE_GROUPS:MOE_GROUPS + MOE_EXPERTS].astype(jnp.int32)
    padded = ((cnt + tme - 1) // tme) * tme
    ends = jnp.cumsum(padded)
    off = ends - padded
    eid = rt[:, 0:2].astype(jnp.int32)
    pos = rt[:, 2:4].astype(jnp.int32) + jnp.sum(
        jnp.where(eid[:, :, None] == jnp.arange(MOE_EXPERTS, dtype=jnp.int32), off, 0), axis=-1)
    p_rows = 2 * t + MOE_EXPERTS * tme
    n_tiles = p_rows // tme
    tile_ends = ends // tme
    tile_expert = jnp.minimum(
        jnp.sum(jnp.arange(n_tiles + 1, dtype=jnp.int32)[:, None] >= tile_ends[None, :], axis=1),
        MOE_EXPERTS - 1).astype(jnp.int32)
    n_used = tile_ends[-1:].astype(jnp.int32)

    inv = _slotmap(pos, jnp.zeros((p_rows,), jnp.int32))
    y_sorted = _experts(tile_expert, n_used, inv, h2, exp_w_gate, exp_w_up, exp_w_down)
    return _combine(pos, y_sorted, rt, x1, g2, ln2_g[None, :], ln2_b[None, :], seq)


def kernel(x, c, w_ada, b_ada, w_in, conv_w, conv_b, a_log_f, a_log_b, dt_bias_f, dt_bias_b, d_skip,
           ssd_norm_g, w_ssd_out, pool_w, pool_scale, w_pool_out, gate_b, w_o, ln1_g, ln1_b,
           router_wg, router_bg, router_we, router_be, exp_w_gate, exp_w_up, exp_w_down, ln2_g, ln2_b):
    batch, seq, d = x.shape
    x2 = x.reshape(batch * seq, d)
    c_pad = jnp.pad(c, ((0, 8 - batch), (0, 0)))
    params = (w_ada, b_ada, w_in, conv_w, conv_b, a_log_f, a_log_b, dt_bias_f, dt_bias_b, d_skip,
              ssd_norm_g, w_ssd_out, pool_w, pool_scale, w_pool_out, gate_b, w_o, ln1_g, ln1_b,
              router_wg, router_bg, router_we, router_be, exp_w_gate, exp_w_up, exp_w_down, ln2_g, ln2_b)
    for l in range(w_ada.shape[0]):
        x2 = _layer(x2, c_pad, batch, seq, *[p[l] for p in params])
    return x2.reshape(batch, seq, d)
```

```python
import functools

import jax
import jax.numpy as jnp
from jax import lax
from jax.experimental import pallas as pl
from jax.experimental.pallas import tpu as pltpu

F32 = jnp.float32
BF16 = jnp.bfloat16
HIGHEST = lax.Precision.HIGHEST

SSD_HEAD_DIM = 64
SSD_GROUPS = 8
SSD_HEADS_PER_GROUP = 4
SSD_HEADS = SSD_GROUPS * SSD_HEADS_PER_GROUP
SSD_STATE = 128
SSD_CONV = 5
SSD_CHUNK = 128
SSD_NORM_EPS = 1e-5
POOL_WINDOWS = (2, 4, 8, 16)
MOE_GROUPS = 4
MOE_EXPERTS_PER_GROUP = 8
MOE_EXPERTS = MOE_GROUPS * MOE_EXPERTS_PER_GROUP
DEPTH = 1
DEEPNORM_ALPHA = (2.0 * DEPTH) ** 0.25
LN_EPS = 1e-5

LANES = 128
HALO = 16
VMEM_LIMIT = 48 * 1024 * 1024
SLOTMAP_UNROLL = 8
DMA_PRIORITIES = 2

TM_INPROJ = 1024
TN_INPROJ = 2304
TM_CONV = 1024
CONV_SUB = 128
SSD_STEP_CHUNKS = 2
TM_MIX = 256
POOL_SUB = 128
TM_ROUTE = 512
TM_SLOTMAP = 1024
TM_EXPERT = 256
TM_COMBINE = 512


def _dot(a, b):
    return jnp.dot(a, b, preferred_element_type=F32)


def _split_hi_lo(v):
    hi = v.astype(BF16)
    lo = (v - hi.astype(F32)).astype(BF16)
    return hi, lo


def _split3(v):
    hi = v.astype(BF16)
    r = v - hi.astype(F32)
    mid = r.astype(BF16)
    lo = (r - mid.astype(F32)).astype(BF16)
    return hi, mid, lo


def _dot3(a, b):
    a_hi, a_lo = _split_hi_lo(a)
    b_hi, b_lo = _split_hi_lo(b)
    return _dot(a_hi, b_hi) + _dot(a_lo, b_hi) + _dot(a_hi, b_lo)


def _layernorm(v):
    mu = jnp.mean(v, axis=-1, keepdims=True)
    vc = v - mu
    var = jnp.mean(vc * vc, axis=-1, keepdims=True)
    return vc * lax.rsqrt(var + LN_EPS)


def _silu(v):
    return v * jax.nn.sigmoid(v)


def _pack_bf16_pairs(v):
    m = v.shape[1] // 2
    hi = lax.bitcast_convert_type(v[:, :m].astype(BF16).astype(F32), jnp.int32)
    lo = lax.bitcast_convert_type(v[:, m:].astype(BF16).astype(F32), jnp.int32)
    return hi | lax.shift_right_logical(lo, 16)


def _unpack_bf16_pairs(p):
    hi = lax.bitcast_convert_type(p & jnp.int32(-65536), F32)
    lo = lax.bitcast_convert_type(lax.shift_left(p, 16), F32)
    return hi, lo


def _cparams(sem):
    return pltpu.CompilerParams(dimension_semantics=sem, vmem_limit_bytes=VMEM_LIMIT)


def _ada_kernel(c_ref, w_ref, b_ref, o_ref):
    o_ref[...] = jnp.dot(_silu(c_ref[...]), w_ref[...], precision=HIGHEST,
                         preferred_element_type=F32) + b_ref[...]


def _ada(c_pad, w, b):
    d, n = w.shape
    tn = 1024
    return pl.pallas_call(
        _ada_kernel,
        grid=(n // tn,),
        in_specs=[pl.BlockSpec((c_pad.shape[0], d), lambda j: (0, 0)),
                  pl.BlockSpec((d, tn), lambda j: (0, j)),
                  pl.BlockSpec((1, tn), lambda j: (0, j))],
        out_specs=pl.BlockSpec((c_pad.shape[0], tn), lambda j: (0, j)),
        out_shape=jax.ShapeDtypeStruct((c_pad.shape[0], n), F32),
        compiler_params=_cparams(("arbitrary",)),
        name="ada_mod",
    )(c_pad, w, b)


def _inproj_kernel(x_ref, sh_ref, sc_ref, w_ref, wdt_ref, o_ref, dt_ref, h_ref):
    @pl.when(pl.program_id(1) == 0)
    def _():
        h = _layernorm(x_ref[...]) * (1.0 + sc_ref[...]) + sh_ref[...]
        h_ref[...] = h.astype(BF16)
        dt_ref[...] = _dot3(h, wdt_ref[...])

    o_ref[...] = _dot(h_ref[...], w_ref[...]).astype(BF16)


def _inproj(x2, sh, sc, w_main, w_dt, seq):
    t, d = x2.shape
    n = w_main.shape[1]
    tm, tn = TM_INPROJ, TN_INPROJ
    tiles_per_seq = seq // tm
    return pl.pallas_call(
        _inproj_kernel,
        grid=(t // tm, n // tn),
        in_specs=[pl.BlockSpec((tm, d), lambda i, j: (i, 0)),
                  pl.BlockSpec((None, 1, d), lambda i, j: (i // tiles_per_seq, 0, 0)),
                  pl.BlockSpec((None, 1, d), lambda i, j: (i // tiles_per_seq, 0, 0)),
                  pl.BlockSpec((d, tn), lambda i, j: (0, j)),
                  pl.BlockSpec((d, LANES), lambda i, j: (0, 0))],
        out_specs=[pl.BlockSpec((tm, tn), lambda i, j: (i, j)),
                   pl.BlockSpec((tm, LANES), lambda i, j: (i, 0))],
        out_shape=[jax.ShapeDtypeStruct((t, n), BF16),
                   jax.ShapeDtypeStruct((t, LANES), F32)],
        scratch_shapes=[pltpu.VMEM((tm, d), BF16)],
        compiler_params=_cparams(("arbitrary", "arbitrary")),
        name="ln_inproj",
    )(x2, sh, sc, w_main, w_dt)


def _conv_kernel(tiles_per_seq, cur_ref, prev_ref, next_ref, w_ref, b_ref, o_ref, ext_ref):
    i = pl.program_id(0)
    tm = cur_ref.shape[0]
    first = (i % tiles_per_seq) == 0
    last = (i % tiles_per_seq) == tiles_per_seq - 1
    ext_ref[0:HALO, :] = jnp.where(first, jnp.zeros_like(prev_ref), prev_ref[...])
    ext_ref[HALO:HALO + tm, :] = cur_ref[...]
    ext_ref[HALO + tm:, :] = jnp.where(last, jnp.zeros_like(next_ref), next_ref[...])
    pad = SSD_CONV // 2
    sub = CONV_SUB
    win = sub + 2 * HALO
    ri = lax.broadcasted_iota(jnp.int32, (sub, win), 0)
    ci = lax.broadcasted_iota(jnp.int32, (sub, win), 1)
    picks = [jnp.where(ci == ri + HALO + k - pad, 1.0, 0.0).astype(BF16) for k in range(SSD_CONV)]
    for m in range(tm // sub):
        window = ext_ref[m * sub:m * sub + win, :]
        acc = b_ref[...] + w_ref[pad:pad + 1, :] * ext_ref[HALO + m * sub:HALO + (m + 1) * sub, :].astype(F32)
        for k in range(SSD_CONV):
            if k != pad:
                acc = acc + w_ref[k:k + 1, :] * _dot(picks[k], window)
        o_ref[m * sub:(m + 1) * sub, :] = _silu(acc).astype(BF16)


def _conv(big, col_off, conv_w, conv_b, seq):
    t = big.shape[0]
    cdim = conv_w.shape[1]
    tm, tc = TM_CONV, 1024
    tiles_per_seq = seq // tm
    cb0 = col_off // tc
    hb = tm // HALO
    nhb = t // HALO
    return pl.pallas_call(
        functools.partial(_conv_kernel, tiles_per_seq),
        grid=(t // tm, cdim // tc),
        in_specs=[pl.BlockSpec((tm, tc), lambda i, j: (i, cb0 + j)),
                  pl.BlockSpec((HALO, tc), lambda i, j: (jnp.maximum(i * hb - 1, 0), cb0 + j)),
                  pl.BlockSpec((HALO, tc), lambda i, j: (jnp.minimum((i + 1) * hb, nhb - 1), cb0 + j)),
                  pl.BlockSpec((SSD_CONV, tc), lambda i, j: (0, j)),
                  pl.BlockSpec((1, tc), lambda i, j: (0, j))],
        out_specs=pl.BlockSpec((tm, tc), lambda i, j: (i, j)),
        out_shape=jax.ShapeDtypeStruct((t, cdim), BF16),
        scratch_shapes=[pltpu.VMEM((tm + 2 * HALO, tc), BF16)],
        compiler_params=_cparams(("arbitrary", "arbitrary")),
        name="conv_silu",
    )(big, big, big, conv_w, conv_b)


def _tri_cumsum(tri, v):
    hi, mid, lo = _split3(v)
    return _dot(tri, hi) + _dot(tri, mid) + _dot(tri, lo)


def _expand_heads(v, e_ref):
    hi, lo = _split_hi_lo(v)
    return _dot(jnp.concatenate([hi, lo], axis=1), e_ref[...])


def _state_step(g, x_g, b_ref, c_ref, st_ref, w_exp, dec_exp, end_row):
    gw = x_g.shape[1]
    cols = slice(g * gw, (g + 1) * gw)
    bg = b_ref[:, g * SSD_STATE:(g + 1) * SSD_STATE]
    cg = c_ref[:, g * SSD_STATE:(g + 1) * SSD_STATE]
    h_in = st_ref[g]
    y_off = _dot(cg, h_in.astype(BF16)) * dec_exp[:, cols]
    xw = (x_g * w_exp[:, cols]).astype(BF16)
    s_new = lax.dot_general(bg, xw, (((0,), (0,)), ((), ())), preferred_element_type=F32)
    st_ref[g] = h_in * dec_exp[end_row:end_row + 1, cols] + s_new
    return y_off


def _chunk_views(refs, ch):
    return [r.at[pl.ds(ch * SSD_CHUNK, SSD_CHUNK)] for r in refs]


def _ssd_bwd_kernel(x_ref, b_ref, c_ref, dt_ref, alog_ref, bias_ref, e64_ref, o_ref, st_ref):
    @pl.when(pl.program_id(1) == 0)
    def _():
        st_ref[...] = jnp.zeros_like(st_ref)

    for ch in reversed(range(x_ref.shape[0] // SSD_CHUNK)):
        x_v, b_v, c_v, dt_v, o_v = _chunk_views((x_ref, b_ref, c_ref, dt_ref, o_ref), ch)
        _ssd_bwd_chunk(x_v, b_v, c_v, dt_v, alog_ref, bias_ref, e64_ref, o_v, st_ref)


def _ssd_bwd_chunk(x_ref, b_ref, c_ref, dt_ref, alog_ref, bias_ref, e64_ref, o_ref, st_ref):
    q = x_ref.shape[0]
    gw = SSD_HEADS_PER_GROUP * SSD_HEAD_DIM
    dt = jax.nn.softplus(dt_ref[...] + bias_ref[...])
    da = dt * (-jnp.exp(alog_ref[...]))
    li = lax.broadcasted_iota(jnp.int32, (q, q), 0)
    si = lax.broadcasted_iota(jnp.int32, (q, q), 1)
    cs = _tri_cumsum(jnp.where(si >= li, 1.0, 0.0).astype(BF16), da)
    w_exp = _expand_heads(dt * jnp.exp(cs[0:1, :] - cs), e64_ref)
    dec_exp = _expand_heads(jnp.exp(cs), e64_ref)
    for g in range(SSD_GROUPS):
        cols = slice(g * gw, (g + 1) * gw)
        x_g = x_ref[:, cols].astype(F32)
        o_ref[:, cols] = _state_step(g, x_g, b_ref, c_ref, st_ref, w_exp, dec_exp, 0).astype(BF16)


def _ssd_fwd_kernel(x_ref, b_ref, c_ref, dt_ref, alog_ref, bias_ref, e64_ref, ecs_ref, z_ref,
                    yb_ref, dskip_ref, ng_ref, o_ref, st_ref):
    @pl.when(pl.program_id(1) == 0)
    def _():
        st_ref[...] = jnp.zeros_like(st_ref)

    for ch in range(x_ref.shape[0] // SSD_CHUNK):
        x_v, b_v, c_v, dt_v, z_v, yb_v, o_v = _chunk_views(
            (x_ref, b_ref, c_ref, dt_ref, z_ref, yb_ref, o_ref), ch)
        _ssd_fwd_chunk(x_v, b_v, c_v, dt_v, alog_ref, bias_ref, e64_ref, ecs_ref, z_v, yb_v, dskip_ref,
                       ng_ref, o_v, st_ref)


def _ssd_fwd_chunk(x_ref, b_ref, c_ref, dt_ref, alog_ref, bias_ref, e64_ref, ecs_ref, z_ref,
                   yb_ref, dskip_ref, ng_ref, o_ref, st_ref):
    q = x_ref.shape[0]
    p = SSD_HEAD_DIM
    hpg = SSD_HEADS_PER_GROUP
    gw = hpg * p
    dt = jax.nn.softplus(dt_ref[...] + bias_ref[...])
    da = dt * (-jnp.exp(alog_ref[...]))
    li = lax.broadcasted_iota(jnp.int32, (q, q), 0)
    si = lax.broadcasted_iota(jnp.int32, (q, q), 1)
    before = si <= li
    after = si >= li
    lane = lax.broadcasted_iota(jnp.int32, (q, LANES), 1)
    tri2 = jnp.concatenate([jnp.where(before, 1.0, 0.0), jnp.where(after, 1.0, 0.0)], axis=1).astype(BF16)
    da2 = jnp.concatenate([jnp.where(lane < SSD_HEADS, da, 0.0),
                           jnp.where((lane >= SSD_HEADS) & (lane < 2 * SSD_HEADS), da, 0.0)], axis=0)
    cs = _tri_cumsum(tri2, da2)
    csp_t = (cs - jnp.log(dt)).T
    hi, mid, lo = _split3(cs)
    nhd = 2 * SSD_HEADS
    v_cs = jnp.concatenate([hi[:, :nhd].astype(F32), mid[:, :nhd].astype(F32), lo[:, :nhd].astype(F32),
                            jnp.zeros((q, nhd), F32)], axis=1).astype(BF16)
    to_end = jnp.where(lane < SSD_HEADS, cs[q - 1:q, :] - cs, 0.0)
    w_exp = _expand_heads(dt * jnp.exp(to_end), e64_ref)
    dec_exp = _expand_heads(jnp.exp(cs), e64_ref)
    half =lax.broadcasted_iota(jnp.int32, (q, 2 * p), 1) < p

    for g in range(SSD_GROUPS):
        cols = slice(g * gw, (g + 1) * gw)
        bg = b_ref[:, g * SSD_STATE:(g + 1) * SSD_STATE]
        cg = c_ref[:, g * SSD_STATE:(g + 1) * SSD_STATE]
        cb = lax.dot_general(cg, bg, (((1,), (1,)), ((), ())), preferred_element_type=F32)
        colb = _dot(v_cs, ecs_ref[:, g * 2 * hpg * q:(g + 1) * 2 * hpg * q])
        x_g = x_ref[:, cols].astype(F32)
        ms = []
        for r in range(hpg):
            hh = g * hpg + r
            seg_f = colb[:, r * q:(r + 1) * q] - csp_t[hh:hh + 1, :]
            seg_b = colb[:, (hpg + r) * q:(hpg + r + 1) * q] - csp_t[SSD_HEADS + hh:SSD_HEADS + hh + 1, :]
            decay = jnp.exp(jnp.where(before, seg_f, -jnp.inf)) + jnp.exp(jnp.where(after, seg_b, -jnp.inf))
            ms.append((cb * decay).astype(BF16))
        ys = []
        for pr in range(hpg // 2):
            xp = x_g[:, 2 * pr * p:2 * (pr + 1) * p]
            rhs = jnp.concatenate([jnp.where(half, xp, 0.0), jnp.where(half, 0.0, xp)], axis=0).astype(BF16)
            ys.append(_dot(jnp.concatenate([ms[2 * pr], ms[2 * pr + 1]], axis=1), rhs))
        y_g = jnp.concatenate(ys, axis=1)
        y_g = y_g + _state_step(g, x_g, b_ref, c_ref, st_ref, w_exp, dec_exp, q - 1)
        y_g = y_g + yb_ref[:, cols].astype(F32) + x_g * dskip_ref[:, cols]
        yg = y_g * _silu(z_ref[:, cols].astype(F32))
        ms_g = jnp.mean(yg * yg, axis=-1, keepdims=True)
        o_ref[:, cols] = (yg * lax.rsqrt(ms_g + SSD_NORM_EPS) * ng_ref[:, cols]).astype(BF16)


def _head_expansion(col0):
    j = jnp.arange(2 * LANES)[:, None]
    c = jnp.arange(SSD_HEADS * SSD_HEAD_DIM)[None, :]
    return ((j % LANES - col0) == c // SSD_HEAD_DIM).astype(BF16)


def _score_expansion():
    nhd = 2 * SSD_HEADS
    row = jnp.arange(4 * nhd)[:, None]
    col = jnp.arange(nhd * SSD_CHUNK)[None, :]
    j = row % nhd
    head = j % SSD_HEADS
    blk = ((head // SSD_HEADS_PER_GROUP) * 2 + j // SSD_HEADS) * SSD_HEADS_PER_GROUP + head % SSD_HEADS_PER_GROUP
    return ((row < 3 * nhd) & (blk == col // SSD_CHUNK)).astype(BF16)


def _ssd(xc, dt_raw, alog_row, bias_row, batch, reverse, final_inputs=None):
    t = xc.shape[0]
    q = SSD_STEP_CHUNKS * SSD_CHUNK
    nc = t // batch // q
    d_inner = SSD_HEADS * SSD_HEAD_DIM
    bcw = SSD_GROUPS * SSD_STATE

    def row(b, c):
        return b * nc + ((nc - 1 - c) if reverse else c)

    full = lambda a: pl.BlockSpec(a.shape, lambda b, c: (0,) * a.ndim)
    e64 = _head_expansion(SSD_HEADS if reverse else 0)
    in_specs = [pl.BlockSpec((q, d_inner), lambda b, c: (row(b, c), 0)),
                pl.BlockSpec((q, bcw), lambda b, c: (row(b, c), d_inner // bcw)),
                pl.BlockSpec((q, bcw), lambda b, c: (row(b, c), d_inner // bcw + 1)),
                pl.BlockSpec((q, LANES), lambda b, c: (row(b, c), 0)),
                full(alog_row), full(bias_row), full(e64)]
    args = [xc, xc, xc, dt_raw, alog_row, bias_row, e64]
    if reverse:
        body = _ssd_bwd_kernel
    else:
        body = _ssd_fwd_kernel
        big, y_bwd, dskip_row, ng_row = final_inputs
        ecs = _score_expansion()
        in_specs += [full(ecs),
                     pl.BlockSpec((q, d_inner), lambda b, c: (row(b, c), 0)),
                     pl.BlockSpec((q, d_inner), lambda b, c: (row(b, c), 0)),
                     full(dskip_row), full(ng_row)]
        args += [ecs, big, y_bwd, dskip_row, ng_row]
    return pl.pallas_call(
        body,
        grid=(batch, nc),
        in_specs=in_specs,
        out_specs=pl.BlockSpec((q, d_inner), lambda b, c: (row(b, c), 0)),
        out_shape=jax.ShapeDtypeStruct((t, d_inner), BF16),
        scratch_shapes=[pltpu.VMEM((SSD_GROUPS, SSD_STATE, SSD_HEADS_PER_GROUP * SSD_HEAD_DIM), F32)],
        compiler_params=_cparams(("arbitrary", "arbitrary")),
        name="ssd_bwd" if reverse else "ssd_fwd",
    )(*args)


def _mix_kernel(tiles_per_seq, seq, yn_ref, u_ref, up_ref, un_ref, ga_ref, gb_ref, x_ref,
                wssd_ref, pw_ref, wpo_ref, wo_ref, wr_ref,
                gateb_ref, pscale_ref, ln1g_ref, ln1b_ref, rb_ref, g1_ref, sh2_ref, sc2_ref,
                x1_ref, h2_ref, lg_ref, ext_ref):
    i = pl.program_id(0)
    tm, width = u_ref.shape
    first = (i % tiles_per_seq) == 0
    last = (i % tiles_per_seq) == tiles_per_seq - 1
    u = u_ref[...].astype(F32)
    ext_ref[0:HALO, :] = jnp.where(first, jnp.zeros_like(up_ref), up_ref[...])
    ext_ref[HALO:HALO + tm, :] = u_ref[...]
    ext_ref[HALO + tm:, :] = jnp.where(last, jnp.zeros_like(un_ref), un_ref[...])
    tpos = (i % tiles_per_seq) * tm + lax.broadcasted_iota(jnp.int32, (tm, 1), 0)
    gd = width // len(POOL_WINDOWS)
    sub = POOL_SUB
    win = sub + 2 * HALO
    ri = lax.broadcasted_iota(jnp.int32, (sub, win), 0) + HALO
    ci = lax.broadcasted_iota(jnp.int32, (sub, win), 1)
    mixed = []
    for gi, w in enumerate(POOL_WINDOWS):
        cols = slice(gi * gd, (gi + 1) * gd)
        band = jnp.where((ci >= ri - w // 2) & (ci < ri + w // 2), 1.0, 0.0).astype(BF16)
        s = jnp.concatenate([_dot(band, ext_ref[m * sub:m * sub + win, cols]) for m in range(tm // sub)], axis=0)
        cnt = (jnp.minimum(tpos + w // 2, seq) - jnp.maximum(tpos - w // 2, 0)).astype(F32)
        diff = s / cnt - u[:, cols]
        mixed.append(_dot(diff.astype(BF16), pw_ref[gi]))
    mixed = jnp.concatenate(mixed, axis=1) * pscale_ref[...]
    y_pool = _dot(mixed.astype(BF16), wpo_ref[...])
    y_ssd = _dot(yn_ref[...], wssd_ref[...])
    d = y_ssd.shape[1]
    g_ssd = jax.nn.sigmoid(ga_ref[...].astype(F32) + gateb_ref[:, 0:d])
    g_pool = jax.nn.sigmoid(gb_ref[...].astype(F32) + gateb_ref[:, d:2 * d])
    mix = _dot((g_ssd * y_ssd + g_pool * y_pool).astype(BF16), wo_ref[...])
    x1 = _layernorm(DEEPNORM_ALPHA * x_ref[...] + g1_ref[...] * mix) * ln1g_ref[...] + ln1b_ref[...]
    x1_ref[...] = x1
    h2 = _layernorm(x1) * (1.0 + sc2_ref[...]) + sh2_ref[...]
    h2_ref[...] = _pack_bf16_pairs(h2)
    lg_ref[...] = _dot3(h2, wr_ref[...]) + rb_ref[...]


def _mix(yn, big, x2, w_ssd_out, pool_w, w_pool_out, w_o, w_router, gate_b, pool_scale, ln1_g, ln1_b,
         r_bias, g1, sh2, sc2, seq, pool_col, gate_col):
    t, d = x2.shape
    tm = TM_MIX
    tiles_per_seq = seq // tm
    hb = tm // HALO
    nhb = t // HALO
    pcb = pool_col // d
    gcb = gate_col // d
    full = lambda a: pl.BlockSpec(a.shape, lambda i: (0,) * a.ndim)
    per_batch = pl.BlockSpec((None, 1, d), lambda i: (i // tiles_per_seq, 0, 0))
    return pl.pallas_call(
        functools.partial(_mix_kernel, tiles_per_seq, seq),
        grid=(t // tm,),
        in_specs=[pl.BlockSpec((tm, yn.shape[1]), lambda i: (i, 0)),
                  pl.BlockSpec((tm, d), lambda i: (i, pcb)),
                  pl.BlockSpec((HALO, d), lambda i: (jnp.maximum(i * hb - 1, 0), pcb)),
                  pl.BlockSpec((HALO, d), lambda i: (jnp.minimum((i + 1) * hb, nhb - 1), pcb)),
                  pl.BlockSpec((tm, d), lambda i: (i, gcb)),
                  pl.BlockSpec((tm, d), lambda i: (i, gcb + 1)),
                  pl.BlockSpec((tm, d), lambda i: (i, 0)),
                  full(w_ssd_out), full(pool_w), full(w_pool_out), full(w_o), full(w_router),
                  full(gate_b), full(pool_scale), full(ln1_g), full(ln1_b), full(r_bias),
                  per_batch, per_batch, per_batch],
        out_specs=[pl.BlockSpec((tm, d), lambda i: (i, 0)),
                   pl.BlockSpec((tm, d // 2), lambda i: (i, 0)),
                   pl.BlockSpec((tm, LANES), lambda i: (i, 0))],
        out_shape=[jax.ShapeDtypeStruct((t, d), F32),
                   jax.ShapeDtypeStruct((t, d // 2), jnp.int32),
                   jax.ShapeDtypeStruct((t, LANES), F32)],
        scratch_shapes=[pltpu.VMEM((tm + 2 * HALO, d), BF16)],
        compiler_params=_cparams(("arbitrary",)),
        name="mix_postln",
    )(yn, big, big, big, big, big, x2, w_ssd_out, pool_w, w_pool_out, w_o, w_router,
      gate_b, pool_scale, ln1_g, ln1_b, r_bias, g1, sh2, sc2)


def _route_kernel(lg_ref, rt_ref, cnt_ref, carry_ref):
    @pl.when(pl.program_id(0) == 0)
    def _():
        carry_ref[...] = jnp.zeros_like(carry_ref)

    lg = lg_ref[...]
    tm = lg.shape[0]
    lane = lax.broadcasted_iota(jnp.int32, lg.shape, 1).astype(F32)
    neg = -jnp.inf
    big_lane = float(LANES)
    gl = jnp.where(lane < MOE_GROUPS, lg, neg)
    gmax = jnp.max(gl, axis=-1, keepdims=True)
    g_w = 1.0 / jnp.sum(jnp.exp(gl - gmax), axis=-1, keepdims=True)
    g_idx = jnp.min(jnp.where(gl == gmax, lane, big_lane), axis=-1, keepdims=True)
    lo = MOE_GROUPS + MOE_EXPERTS_PER_GROUP * g_idx
    el = jnp.where((lane >= lo) & (lane < lo + MOE_EXPERTS_PER_GROUP), lg, neg)
    m1 = jnp.max(el, axis=-1, keepdims=True)
    i1 = jnp.min(jnp.where(el == m1, lane, big_lane), axis=-1, keepdims=True)
    el2 = jnp.where(lane == i1, neg, el)
    m2 = jnp.max(el2, axis=-1, keepdims=True)
    i2 = jnp.min(jnp.where(el2 == m2, lane, big_lane), axis=-1, keepdims=True)
    e = jnp.exp(m2 - m1)
    w1 = g_w / (1.0 + e)
    w2 = g_w * e / (1.0 + e)
    onehot = jnp.where((lane == i1) | (lane == i2), 1.0, 0.0)
    ri = lax.broadcasted_iota(jnp.int32, (tm, tm), 0)
    ci = lax.broadcasted_iota(jnp.int32, (tm, tm), 1)
    earlier = jnp.where(ci < ri, 1.0, 0.0).astype(BF16)
    rank = _dot(earlier, onehot.astype(BF16)) + carry_ref[...]
    r1 = jnp.sum(jnp.where(lane == i1, rank, 0.0), axis=-1, keepdims=True)
    r2 = jnp.sum(jnp.where(lane == i2, rank, 0.0), axis=-1, keepdims=True)
    carry_ref[...] = carry_ref[...] + jnp.sum(onehot, axis=0, keepdims=True)
    cnt_ref[...] = carry_ref[...]
    out = jnp.where(lane == 0, i1 - MOE_GROUPS, 0.0)
    out = jnp.where(lane == 1, i2 - MOE_GROUPS, out)
    out = jnp.where(lane == 2, r1, out)
    out = jnp.where(lane == 3, r2, out)
    out = jnp.where(lane == 4, w1, out)
    out = jnp.where(lane == 5, w2, out)
    rt_ref[...] = out


def _route(logits):
    t = logits.shape[0]
    tm = TM_ROUTE
    return pl.pallas_call(
        _route_kernel,
        grid=(t // tm,),
        in_specs=[pl.BlockSpec((tm, LANES), lambda i: (i, 0))],
        out_specs=[pl.BlockSpec((tm, LANES), lambda i: (i, 0)),
                   pl.BlockSpec((1, LANES), lambda i: (0, 0))],
        out_shape=[jax.ShapeDtypeStruct((t, LANES), F32),
                   jax.ShapeDtypeStruct((1, LANES), F32)],
        scratch_shapes=[pltpu.VMEM((1, LANES), F32)],
        compiler_params=_cparams(("arbitrary",)),
        name="route",
    )(logits)


def _slotmap_kernel(pos_ref, init_ref, inv_ref, sem):
    i = pl.program_id(0)
    tm = pos_ref.shape[1] // 2

    @pl.when(i == 0)
    def _():
        cp = pltpu.make_async_copy(init_ref, inv_ref, sem)
        cp.start()
        cp.wait()

    def body(r, carry):
        tok = i * tm + r
        inv_ref[pos_ref[0, 2 * r]] = tok
        inv_ref[pos_ref[0, 2 * r + 1]] = tok
        return carry

    lax.fori_loop(0, tm, body, 0, unroll=SLOTMAP_UNROLL)


def _slotmap(pos, init):
    t = pos.shape[0]
    tm = TM_SLOTMAP
    pos3 = pos.reshape(t // tm, 1, 2 * tm)
    return pl.pallas_call(
        _slotmap_kernel,
        grid=(t // tm,),
        in_specs=[pl.BlockSpec((None, 1, 2 * tm), lambda i: (i, 0, 0), memory_space=pltpu.SMEM),
                  pl.BlockSpec(memory_space=pl.ANY)],
        out_specs=pl.BlockSpec(memory_space=pltpu.SMEM),
        out_shape=jax.ShapeDtypeStruct(init.shape, jnp.int32),
        scratch_shapes=[pltpu.SemaphoreType.DMA(())],
        compiler_params=_cparams(("arbitrary",)),
        name="slotmap",
    )(pos3, init)


def _row_copy(src_ref, src_row, dst_ref, dst_row, sem):
    return pltpu.make_async_copy(src_ref.at[pl.ds(src_row, 1)], dst_ref.at[pl.ds(dst_row, 1)], sem)


def _expert_kernel(te_ref, nu_ref, inv_cur_ref, inv_nxt_ref, h2_ref, wg_ref, wu_ref, wd_ref,
                   y_ref, xbuf_ref, wgb_ref, wub_ref, wdb_ref, gsem):
    j = pl.program_id(0)
    n_used = nu_ref[0]
    tm = y_ref.shape[0]
    slot = j % 2

    def gather_start(inv_ref, s):
        for r in range(tm):
            cp = _row_copy(h2_ref, inv_ref[0, r], xbuf_ref.at[s], r, gsem.at[s])
            cp.start(priority=r % DMA_PRIORITIES)

    def gather_wait(s):
        pltpu.make_async_copy(h2_ref.at[pl.ds(0, tm)], xbuf_ref.at[s], gsem.at[s]).wait()

    @pl.when(j == 0)
    def _():
        gather_start(inv_cur_ref, 0)

    new_expert = jnp.logical_or(j == 0, te_ref[j] != te_ref[jnp.maximum(j - 1, 0)])

    @pl.when(jnp.logical_and(j < n_used, new_expert))
    def _():
        wgb_ref[...] = wg_ref[...].astype(BF16)
        wub_ref[...] = wu_ref[...].astype(BF16)
        wdb_ref[...] = wd_ref[...].astype(BF16)

    @pl.when(j < n_used)
    def _():
        gather_wait(slot)
        xl, xr = [v.astype(BF16) for v in _unpack_bf16_pairs(xbuf_ref[slot])]
        gather_start(inv_nxt_ref, 1 - slot)
        half = xl.shape[1]
        gate = _dot(xl, wgb_ref[0:half, :]) + _dot(xr, wgb_ref[half:, :])
        up = _dot(xl, wub_ref[0:half, :]) + _dot(xr, wub_ref[half:, :])
        y = _dot((_silu(gate) * up).astype(BF16), wdb_ref[...])
        y_ref[...] = _pack_bf16_pairs(y)

    @pl.when(j >= n_used)
    def _():
        y_ref[...] = jnp.zeros_like(y_ref)

    @pl.when(j == n_used)
    def _():
        gather_wait(slot)


def _experts(tile_expert, n_used, inv, h2, w_gate, w_up, w_down):
    _, d, hdim = w_gate.shape
    assert h2.shape[1] == d // 2 and h2.dtype == jnp.int32, "rows arrive as packed bf16 column pairs"
    tm = TM_EXPERT
    n_tiles = inv.shape[0] // tm
    inv3 = inv.reshape(n_tiles, 1, tm)
    last_used = lambda j, nu: jnp.minimum(j, nu[0] - 1)
    grid_spec = pltpu.PrefetchScalarGridSpec(
        num_scalar_prefetch=2,
        grid=(n_tiles + 1,),
        in_specs=[pl.BlockSpec((None, 1, tm), lambda j, te, nu: (jnp.minimum(j, n_tiles - 1), 0, 0),
                               memory_space=pltpu.SMEM),
                  pl.BlockSpec((None, 1, tm), lambda j, te, nu: (jnp.minimum(j + 1, n_tiles - 1), 0, 0),
                               memory_space=pltpu.SMEM),
                  pl.BlockSpec(memory_space=pl.ANY),
                  pl.BlockSpec((None, d, hdim), lambda j, te, nu: (te[last_used(j, nu)], 0, 0)),
                  pl.BlockSpec((None, d, hdim), lambda j, te, nu: (te[last_used(j, nu)], 0, 0)),
                  pl.BlockSpec((None, hdim, d), lambda j, te, nu: (te[last_used(j, nu)], 0, 0))],
        out_specs=pl.BlockSpec((tm, d // 2), lambda j, te, nu: (j, 0)),
        scratch_shapes=[pltpu.VMEM((2, tm, d // 2), jnp.int32),
                        pltpu.VMEM((d, hdim), BF16), pltpu.VMEM((d, hdim), BF16),
                        pltpu.VMEM((hdim, d), BF16), pltpu.SemaphoreType.DMA((2,))],
    )
    return pl.pallas_call(
        _expert_kernel,
        grid_spec=grid_spec,
        out_shape=jax.ShapeDtypeStruct(((n_tiles + 1) * tm, d // 2), jnp.int32),
        compiler_params=_cparams(("arbitrary",)),
        name="experts",
    )(tile_expert, n_used, inv3, inv3, h2, w_gate, w_up, w_down)


def _combine_kernel(pos_cur_ref, pos_nxt_ref, y_ref, rt_ref, x1_ref, g2_ref, lng_ref, lnb_ref, o_ref,
                    buf_ref, sem):
    i = pl.program_id(0)
    tm = x1_ref.shape[0]
    slot = i % 2

    def gather_start(pos_ref, s):
        for r in range(tm):
            for k in range(2):
                cp = _row_copy(y_ref, pos_ref[0, 2 * r + k], buf_ref.at[s, k], r, sem.at[s])
                cp.start(priority=k % DMA_PRIORITIES)

    def gather_wait(s):
        for k in range(2):
            pltpu.make_async_copy(y_ref.at[pl.ds(0, tm)], buf_ref.at[s, k], sem.at[s]).wait()

    @pl.when(i == 0)
    def _():
        gather_start(pos_cur_ref, 0)

    gather_wait(slot)
    l0, r0 = _unpack_bf16_pairs(buf_ref[slot, 0])
    l1, r1 = _unpack_bf16_pairs(buf_ref[slot, 1])
    gather_start(pos_nxt_ref, 1 - slot)
    rt = rt_ref[...]
    w0, w1 = rt[:, 4:5], rt[:, 5:6]
    y_moe = jnp.concatenate([w0 * l0 + w1 * l1, w0 * r0 + w1 * r1], axis=1)
    v = DEEPNORM_ALPHA * x1_ref[...] + g2_ref[...] * y_moe
    o_ref[...] = _layernorm(v) * lng_ref[...] + lnb_ref[...]

    @pl.when(i == pl.num_programs(0) - 1)
    def _():
        gather_wait(1 - slot)


def _combine(pos, y_sorted, rt, x1, g2, ln_g, ln_b, seq):
    t, d = x1.shape
    tm = TM_COMBINE
    tiles_per_seq = seq // tm
    nt = t // tm
    pos3 = pos.reshape(nt, 1, 2 * tm)
    return pl.pallas_call(
        _combine_kernel,
        grid=(nt,),
        in_specs=[pl.BlockSpec((None, 1, 2 * tm), lambda i: (i, 0, 0), memory_space=pltpu.SMEM),
                  pl.BlockSpec((None, 1, 2 * tm), lambda i: (jnp.minimum(i + 1, nt - 1), 0, 0),
                               memory_space=pltpu.SMEM),
                  pl.BlockSpec(memory_space=pl.ANY),
                  pl.BlockSpec((tm, LANES), lambda i: (i, 0)),
                  pl.BlockSpec((tm, d), lambda i: (i, 0)),
                  pl.BlockSpec((None, 1, d), lambda i: (i // tiles_per_seq, 0, 0)),
                  pl.BlockSpec((1, d), lambda i: (0, 0)),
                  pl.BlockSpec((1, d), lambda i: (0, 0))],
        out_specs=pl.BlockSpec((tm, d), lambda i: (i, 0)),
        out_shape=jax.ShapeDtypeStruct((t, d), F32),
        scratch_shapes=[pltpu.VMEM((2, 2, tm, d // 2), jnp.int32), pltpu.SemaphoreType.DMA((2,))],
        compiler_params=_cparams(("arbitrary",)),
        name="combine_postln",
    )(pos3, pos3, y_sorted, rt, x1, g2, ln_g, ln_b)


def _layer(x2, c_pad, batch, seq, w_ada, b_ada, w_in, conv_w, conv_b, a_log_f, a_log_b, dt_bias_f,
           dt_bias_b, d_skip, ssd_norm_g, w_ssd_out, pool_w, pool_scale, w_pool_out, gate_b, w_o,
           ln1_g, ln1_b, router_wg, router_bg, router_we, router_be, exp_w_gate, exp_w_up,
           exp_w_down, ln2_g, ln2_b):
    t, d = x2.shape
    d_inner = SSD_HEADS * SSD_HEAD_DIM
    conv_dim = d_inner + 2 * SSD_GROUPS * SSD_STATE
    pool_width = pool_scale.shape[0]

    mod = _ada(c_pad, w_ada, b_ada[None, :])[:batch]
    sh1, sc1, g1, sh2, sc2, g2 = [m[:, None, :] for m in jnp.split(mod, 6, axis=-1)]

    o_xbc = d_inner
    o_dt = o_xbc + conv_dim
    o_pool = o_dt + 2 * SSD_HEADS
    w_in_b = w_in.astype(BF16)
    w_main = jnp.concatenate([w_in_b[:, :o_dt], w_in_b[:, o_pool:]], axis=1)
    w_dt = jnp.pad(w_in[:, o_dt:o_pool], ((0, 0), (0, LANES - 2 * SSD_HEADS)))
    pool_col = o_dt
    gate_col = o_dt + pool_width
    big, dt_raw = _inproj(x2, sh1, sc1, w_main, w_dt, seq)

    xc = _conv(big, o_xbc, conv_w, conv_b[None, :], seq)

    pad_row = lambda f, b: jnp.pad(jnp.concatenate([f, b]), (0, LANES - 2 * SSD_HEADS))[None, :]
    alog_row = pad_row(a_log_f, a_log_b)
    bias_row = pad_row(dt_bias_f, dt_bias_b)
    dskip_row = jnp.repeat(d_skip, SSD_HEAD_DIM)[None, :]
    y_bwd = _ssd(xc, dt_raw, alog_row, bias_row, batch, reverse=True)
    yn = _ssd(xc, dt_raw, alog_row, bias_row, batch, reverse=False,
              final_inputs=(big, y_bwd, dskip_row, ssd_norm_g[None, :]))

    w_router = jnp.pad(jnp.concatenate([router_wg, router_we], axis=1),
                       ((0, 0), (0, LANES - MOE_GROUPS - MOE_EXPERTS)))
    r_bias = jnp.pad(jnp.concatenate([router_bg, router_be]), (0, LANES - MOE_GROUPS - MOE_EXPERTS))[None, :]
    x1, h2, logits = _mix(yn, big, x2, w_ssd_out.astype(BF16), pool_w.astype(BF16),
                          w_pool_out.astype(BF16), w_o.astype(BF16), w_router, gate_b[None, :],
                          pool_scale[None, :], ln1_g[None, :], ln1_b[None, :], r_bias, g1, sh2, sc2,
                          seq, pool_col, gate_col)

    rt, counts = _route(logits)

    tme = TM_EXPERT
    cnt = counts[0, MOE_GROUPS:MOE_GROUPS + MOE_EXPERTS].astype(jnp.int32)
    padded = ((cnt + tme - 1) // tme) * tme
    ends = jnp.cumsum(padded)
    off = ends - padded
    eid = rt[:, 0:2].astype(jnp.int32)
    pos = rt[:, 2:4].astype(jnp.int32) + jnp.sum(
        jnp.where(eid[:, :, None] == jnp.arange(MOE_EXPERTS, dtype=jnp.int32), off, 0), axis=-1)
    p_rows = 2 * t + MOE_EXPERTS * tme
    n_tiles = p_rows // tme
    tile_ends = ends // tme
    tile_expert = jnp.minimum(
        jnp.sum(jnp.arange(n_tiles + 1, dtype=jnp.int32)[:, None] >= tile_ends[None, :], axis=1),
        MOE_EXPERTS - 1).astype(jnp.int32)
    n_used = tile_ends[-1:].astype(jnp.int32)

    inv = _slotmap(pos, jnp.zeros((p_rows,), jnp.int32))
    y_sorted = _experts(tile_expert, n_used, inv, h2, exp_w_gate, exp_w_up, exp_w_down)
    return _combine(pos, y_sorted, rt, x1, g2, ln2_g[None, :], ln2_b[None, :], seq)


def kernel(x, c, w_ada, b_ada, w_in, conv_w, conv_b, a_log_f, a_log_b, dt_bias_f, dt_bias_b, d_skip,
           ssd_norm_g, w_ssd_out, pool_w, pool_scale, w_pool_out, gate_b, w_o, ln1_g, ln1_b,
           router_wg, router_bg, router_we, router_be, exp_w_gate, exp_w_up, exp_w_down, ln2_g, ln2_b):
    batch, seq, d = x.shape
    x2 = x.reshape(batch * seq, d)
    c_pad = jnp.pad(c, ((0, 8 - batch), (0, 0)))
    params = (w_ada, b_ada, w_in, conv_w, conv_b, a_log_f, a_log_b, dt_bias_f, dt_bias_b, d_skip,
              ssd_norm_g, w_ssd_out, pool_w, pool_scale, w_pool_out, gate_b, w_o, ln1_g, ln1_b,
              router_wg, router_bg, router_we, router_be, exp_w_gate, exp_w_up, exp_w_down, ln2_g, ln2_b)
    for l in range(w_ada.shape[0]):
        x2 = _layer(x2, c_pad, batch, seq, *[p[l] for p in params])
    return x2.reshape(batch, seq, d)
```

```python
import functools

import jax
import jax.numpy as jnp
from jax import lax
from jax.experimental import pallas as pl
from jax.experimental.pallas import tpu as pltpu

F32 = jnp.float32
BF16 = jnp.bfloat16
HIGHEST = lax.Precision.HIGHEST

SSD_HEAD_DIM = 64
SSD_GROUPS = 8
SSD_HEADS_PER_GROUP = 4
SSD_HEADS = SSD_GROUPS * SSD_HEADS_PER_GROUP
SSD_STATE = 128
SSD_CONV = 5
SSD_CHUNK = 128
SSD_NORM_EPS = 1e-5
POOL_WINDOWS = (2, 4, 8, 16)
MOE_GROUPS = 4
MOE_EXPERTS_PER_GROUP = 8
MOE_EXPERTS = MOE_GROUPS * MOE_EXPERTS_PER_GROUP
DEPTH = 1
DEEPNORM_ALPHA = (2.0 * DEPTH) ** 0.25
LN_EPS = 1e-5

LANES = 128
HALO = 16
VMEM_LIMIT = 48 * 1024 * 1024
SLOTMAP_UNROLL = 8
DMA_PRIORITIES = 2
GATHER_RING = 3

TM_INPROJ = 1024
TN_INPROJ = 2304
TM_CONV = 1024
CONV_SUB = 128
SSD_STEP_CHUNKS = 2
TM_MIX = 256
POOL_SUB = 128
TM_ROUTE = 512
TM_SLOTMAP = 1024
TM_EXPERT = 256
TM_COMBINE = 512


def _dot(a, b):
    return jnp.dot(a, b, preferred_element_type=F32)


def _split_hi_lo(v):
    hi = v.astype(BF16)
    lo = (v - hi.astype(F32)).astype(BF16)
    return hi, lo


def _split3(v):
    hi = v.astype(BF16)
    r = v - hi.astype(F32)
    mid = r.astype(BF16)
    lo = (r - mid.astype(F32)).astype(BF16)
    return hi, mid, lo


def _dot3(a, b):
    a_hi, a_lo = _split_hi_lo(a)
    b_hi, b_lo = _split_hi_lo(b)
    return _dot(a_hi, b_hi) + _dot(a_lo, b_hi) + _dot(a_hi, b_lo)


def _layernorm(v):
    mu = jnp.mean(v, axis=-1, keepdims=True)
    vc = v - mu
    var = jnp.mean(vc * vc, axis=-1, keepdims=True)
    return vc * lax.rsqrt(var + LN_EPS)


def _silu(v):
    return v * jax.nn.sigmoid(v)


def _pack_bf16_pairs(v):
    m = v.shape[1] // 2
    hi = lax.bitcast_convert_type(v[:, :m].astype(BF16).astype(F32), jnp.int32)
    lo = lax.bitcast_convert_type(v[:, m:].astype(BF16).astype(F32), jnp.int32)
    return hi | lax.shift_right_logical(lo, 16)


def _unpack_bf16_pairs(p):
    hi = lax.bitcast_convert_type(p & jnp.int32(-65536), F32)
    lo = lax.bitcast_convert_type(lax.shift_left(p, 16), F32)
    return hi, lo


def _cparams(sem):
    return pltpu.CompilerParams(dimension_semantics=sem, vmem_limit_bytes=VMEM_LIMIT)


def _ada_kernel(c_ref, w_ref, b_ref, o_ref):
    o_ref[...] = jnp.dot(_silu(c_ref[...]), w_ref[...], precision=HIGHEST,
                         preferred_element_type=F32) + b_ref[...]


def _ada(c_pad, w, b):
    d, n = w.shape
    tn = 1024
    return pl.pallas_call(
        _ada_kernel,
        grid=(n // tn,),
        in_specs=[pl.BlockSpec((c_pad.shape[0], d), lambda j: (0, 0)),
                  pl.BlockSpec((d, tn), lambda j: (0, j)),
                  pl.BlockSpec((1, tn), lambda j: (0, j))],
        out_specs=pl.BlockSpec((c_pad.shape[0], tn), lambda j: (0, j)),
        out_shape=jax.ShapeDtypeStruct((c_pad.shape[0], n), F32),
        compiler_params=_cparams(("arbitrary",)),
        name="ada_mod",
    )(c_pad, w, b)


def _inproj_kernel(x_ref, sh_ref, sc_ref, w_ref, wdt_ref, o_ref, dt_ref, h_ref):
    @pl.when(pl.program_id(1) == 0)
    def _():
        h = _layernorm(x_ref[...]) * (1.0 + sc_ref[...]) + sh_ref[...]
        h_ref[...] = h.astype(BF16)
        dt_ref[...] = _dot3(h, wdt_ref[...])

    o_ref[...] = _dot(h_ref[...], w_ref[...]).astype(BF16)


def _inproj(x2, sh, sc, w_main, w_dt, seq):
    t, d = x2.shape
    n = w_main.shape[1]
    tm, tn = TM_INPROJ, TN_INPROJ
    tiles_per_seq = seq // tm
    return pl.pallas_call(
        _inproj_kernel,
        grid=(t // tm, n // tn),
        in_specs=[pl.BlockSpec((tm, d), lambda i, j: (i, 0)),
                  pl.BlockSpec((None, 1, d), lambda i, j: (i // tiles_per_seq, 0, 0)),
                  pl.BlockSpec((None, 1, d), lambda i, j: (i // tiles_per_seq, 0, 0)),
                  pl.BlockSpec((d, tn), lambda i, j: (0, j)),
                  pl.BlockSpec((d, LANES), lambda i, j: (0, 0))],
        out_specs=[pl.BlockSpec((tm, tn), lambda i, j: (i, j)),
                   pl.BlockSpec((tm, LANES), lambda i, j: (i, 0))],
        out_shape=[jax.ShapeDtypeStruct((t, n), BF16),
                   jax.ShapeDtypeStruct((t, LANES), F32)],
        scratch_shapes=[pltpu.VMEM((tm, d), BF16)],
        compiler_params=_cparams(("arbitrary", "arbitrary")),
        name="ln_inproj",
    )(x2, sh, sc, w_main, w_dt)


def _conv_kernel(tiles_per_seq, cur_ref, prev_ref, next_ref, w_ref, b_ref, o_ref, ext_ref):
    i = pl.program_id(0)
    tm = cur_ref.shape[0]
    first = (i % tiles_per_seq) == 0
    last = (i % tiles_per_seq) == tiles_per_seq - 1
    ext_ref[0:HALO, :] = jnp.where(first, jnp.zeros_like(prev_ref), prev_ref[...])
    ext_ref[HALO:HALO + tm, :] = cur_ref[...]
    ext_ref[HALO + tm:, :] = jnp.where(last, jnp.zeros_like(next_ref), next_ref[...])
    pad = SSD_CONV // 2
    sub = CONV_SUB
    win = sub + 2 * HALO
    ri = lax.broadcasted_iota(jnp.int32, (sub, win), 0)
    ci = lax.broadcasted_iota(jnp.int32, (sub, win), 1)
    picks = [jnp.where(ci == ri + HALO + k - pad, 1.0, 0.0).astype(BF16) for k in range(SSD_CONV)]
    for m in range(tm // sub):
        window = ext_ref[m * sub:m * sub + win, :]
        acc = b_ref[...] + w_ref[pad:pad + 1, :] * ext_ref[HALO + m * sub:HALO + (m + 1) * sub, :].astype(F32)
        for k in range(SSD_CONV):
            if k != pad:
                acc = acc + w_ref[k:k + 1, :] * _dot(picks[k], window)
        o_ref[m * sub:(m + 1) * sub, :] = _silu(acc).astype(BF16)


def _conv(big, col_off, conv_w, conv_b, seq):
    t = big.shape[0]
    cdim = conv_w.shape[1]
    tm, tc = TM_CONV, 1024
    tiles_per_seq = seq // tm
    cb0 = col_off // tc
    hb = tm // HALO
    nhb = t // HALO
    return pl.pallas_call(
        functools.partial(_conv_kernel, tiles_per_seq),
        grid=(t // tm, cdim // tc),
        in_specs=[pl.BlockSpec((tm, tc), lambda i, j: (i, cb0 + j)),
                  pl.BlockSpec((HALO, tc), lambda i, j: (jnp.maximum(i * hb - 1, 0), cb0 + j)),
                  pl.BlockSpec((HALO, tc), lambda i, j: (jnp.minimum((i + 1) * hb, nhb - 1), cb0 + j)),
                  pl.BlockSpec((SSD_CONV, tc), lambda i, j: (0, j)),
                  pl.BlockSpec((1, tc), lambda i, j: (0, j))],
        out_specs=pl.BlockSpec((tm, tc), lambda i, j: (i, j)),
        out_shape=jax.ShapeDtypeStruct((t, cdim), BF16),
        scratch_shapes=[pltpu.VMEM((tm + 2 * HALO, tc), BF16)],
        compiler_params=_cparams(("arbitrary", "arbitrary")),
        name="conv_silu",
    )(big, big, big, conv_w, conv_b)


def _tri_cumsum(tri, v):
    hi, mid, lo = _split3(v)
    return _dot(tri, hi) + _dot(tri, mid) + _dot(tri, lo)


def _expand_heads(v, e_ref):
    hi, lo = _split_hi_lo(v)
    return _dot(jnp.concatenate([hi, lo], axis=1), e_ref[...])


def _state_step(g, x_g, b_ref, c_ref, st_ref, w_exp, dec_exp, end_row):
    gw = x_g.shape[1]
    cols = slice(g * gw, (g + 1) * gw)
    bg = b_ref[:, g * SSD_STATE:(g + 1) * SSD_STATE]
    cg = c_ref[:, g * SSD_STATE:(g + 1) * SSD_STATE]
    h_in = st_ref[g]
    y_off = _dot(cg, h_in.astype(BF16)) * dec_exp[:, cols]
    xw = (x_g * w_exp[:, cols]).astype(BF16)
    s_new = lax.dot_general(bg, xw, (((0,), (0,)), ((), ())), preferred_element_type=F32)
    st_ref[g] = h_in * dec_exp[end_row:end_row + 1, cols] + s_new
    return y_off


def _chunk_views(refs, ch):
    return [r.at[pl.ds(ch * SSD_CHUNK, SSD_CHUNK)] for r in refs]


def _ssd_bwd_kernel(x_ref, b_ref, c_ref, dt_ref, alog_ref, bias_ref, e64_ref, o_ref, st_ref):
    @pl.when(pl.program_id(1) == 0)
    def _():
        st_ref[...] = jnp.zeros_like(st_ref)

    for ch in reversed(range(x_ref.shape[0] // SSD_CHUNK)):
        x_v, b_v, c_v, dt_v, o_v = _chunk_views((x_ref, b_ref, c_ref, dt_ref, o_ref), ch)
        _ssd_bwd_chunk(x_v, b_v, c_v, dt_v, alog_ref, bias_ref, e64_ref, o_v, st_ref)


def _ssd_bwd_chunk(x_ref, b_ref, c_ref, dt_ref, alog_ref, bias_ref, e64_ref, o_ref, st_ref):
    q = x_ref.shape[0]
    gw = SSD_HEADS_PER_GROUP * SSD_HEAD_DIM
    dt = jax.nn.softplus(dt_ref[...] + bias_ref[...])
    da = dt * (-jnp.exp(alog_ref[...]))
    li = lax.broadcasted_iota(jnp.int32, (q, q), 0)
    si = lax.broadcasted_iota(jnp.int32, (q, q), 1)
    cs = _tri_cumsum(jnp.where(si >= li, 1.0, 0.0).astype(BF16), da)
    w_exp = _expand_heads(dt * jnp.exp(cs[0:1, :] - cs), e64_ref)
    dec_exp = _expand_heads(jnp.exp(cs), e64_ref)
    for g in range(SSD_GROUPS):
        cols = slice(g * gw, (g + 1) * gw)
        x_g = x_ref[:, cols].astype(F32)
        o_ref[:, cols] = _state_step(g, x_g, b_ref, c_ref, st_ref, w_exp, dec_exp, 0).astype(BF16)


def _ssd_fwd_kernel(x_ref, b_ref, c_ref, dt_ref, alog_ref, bias_ref, e64_ref, ecs_ref, z_ref,
                    yb_ref, dskip_ref, ng_ref, o_ref, st_ref):
    @pl.when(pl.program_id(1) == 0)
    def _():
        st_ref[...] = jnp.zeros_like(st_ref)

    for ch in range(x_ref.shape[0] // SSD_CHUNK):
        x_v, b_v, c_v, dt_v, z_v, yb_v, o_v = _chunk_views(
            (x_ref, b_ref, c_ref, dt_ref, z_ref, yb_ref, o_ref), ch)
        _ssd_fwd_chunk(x_v, b_v, c_v, dt_v, alog_ref, bias_ref, e64_ref, ecs_ref, z_v, yb_v, dskip_ref,
                       ng_ref, o_v, st_ref)


def _ssd_fwd_chunk(x_ref, b_ref, c_ref, dt_ref, alog_ref, bias_ref, e64_ref, ecs_ref, z_ref,
                   yb_ref, dskip_ref, ng_ref, o_ref, st_ref):
    q = x_ref.shape[0]
    p = SSD_HEAD_DIM
    hpg = SSD_HEADS_PER_GROUP
    gw = hpg * p
    dt = jax.nn.softplus(dt_ref[...] + bias_ref[...])
    da = dt * (-jnp.exp(alog_ref[...]))
    li = lax.broadcasted_iota(jnp.int32, (q, q), 0)
    si = lax.broadcasted_iota(jnp.int32, (q, q), 1)
    before = si <= li
    after = si >= li
    lane = lax.broadcasted_iota(jnp.int32, (q, LANES), 1)
    tri2 = jnp.concatenate([jnp.where(before, 1.0, 0.0), jnp.where(after, 1.0, 0.0)], axis=1).astype(BF16)
    da2 = jnp.concatenate([jnp.where(lane < SSD_HEADS, da, 0.0),
                           jnp.where((lane >= SSD_HEADS) & (lane < 2 * SSD_HEADS), da, 0.0)], axis=0)
    cs = _tri_cumsum(tri2, da2)
    csp_t = (cs - jnp.log(dt)).T
    hi, mid, lo = _split3(cs)
    nhd = 2 * SSD_HEADS
    v_cs = jnp.concatenate([hi[:, :nhd].astype(F32), mid[:, :nhd].astype(F32), lo[:, :nhd].astype(F32),
                            jnp.zeros((q, nhd), F32)], axis=1).astype(BF16)
    to_end = jnp.where(lane < SSD_HEADS, cs[q - 1:q, :] - cs, 0.0)
    w_exp = _expand_heads(dt * jnp.exp(to_end), e64_ref)
    dec_exp = _expand_heads(jnp.exp(cs), e64_ref)
    half =lax.broadcasted_iota(jnp.int32, (q, 2 * p), 1) < p

    for g in range(SSD_GROUPS):
        cols = slice(g * gw, (g + 1) * gw)
        bg = b_ref[:, g * SSD_STATE:(g + 1) * SSD_STATE]
        cg = c_ref[:, g * SSD_STATE:(g + 1) * SSD_STATE]
        cb = lax.dot_general(cg, bg, (((1,), (1,)), ((), ())), preferred_element_type=F32)
        colb = _dot(v_cs, ecs_ref[:, g * 2 * hpg * q:(g + 1) * 2 * hpg * q])
        x_g = x_ref[:, cols].astype(F32)
        ms = []
        for r in range(hpg):
            hh = g * hpg + r
            seg_f = colb[:, r * q:(r + 1) * q] - csp_t[hh:hh + 1, :]
            seg_b = colb[:, (hpg + r) * q:(hpg + r + 1) * q] - csp_t[SSD_HEADS + hh:SSD_HEADS + hh + 1, :]
            decay = jnp.exp(jnp.where(before, seg_f, -jnp.inf)) + jnp.exp(jnp.where(after, seg_b, -jnp.inf))
            ms.append((cb * decay).astype(BF16))
        ys = []
        for pr in range(hpg // 2):
            xp = x_g[:, 2 * pr * p:2 * (pr + 1) * p]
            rhs = jnp.concatenate([jnp.where(half, xp, 0.0), jnp.where(half, 0.0, xp)], axis=0).astype(BF16)
            ys.append(_dot(jnp.concatenate([ms[2 * pr], ms[2 * pr + 1]], axis=1), rhs))
        y_g = jnp.concatenate(ys, axis=1)
        y_g = y_g + _state_step(g, x_g, b_ref, c_ref, st_ref, w_exp, dec_exp, q - 1)
        y_g = y_g + yb_ref[:, cols].astype(F32) + x_g * dskip_ref[:, cols]
        yg = y_g * _silu(z_ref[:, cols].astype(F32))
        ms_g = jnp.mean(yg * yg, axis=-1, keepdims=True)
        o_ref[:, cols] = (yg * lax.rsqrt(ms_g + SSD_NORM_EPS) * ng_ref[:, cols]).astype(BF16)


def _head_expansion(col0):
    j = jnp.arange(2 * LANES)[:, None]
    c = jnp.arange(SSD_HEADS * SSD_HEAD_DIM)[None, :]
    return ((j % LANES - col0) == c // SSD_HEAD_DIM).astype(BF16)


def _score_expansion():
    nhd = 2 * SSD_HEADS
    row = jnp.arange(4 * nhd)[:, None]
    col = jnp.arange(nhd * SSD_CHUNK)[None, :]
    j = row % nhd
    head = j % SSD_HEADS
    blk = ((head // SSD_HEADS_PER_GROUP) * 2 + j // SSD_HEADS) * SSD_HEADS_PER_GROUP + head % SSD_HEADS_PER_GROUP
    return ((row < 3 * nhd) & (blk == col // SSD_CHUNK)).astype(BF16)


def _ssd(xc, dt_raw, alog_row, bias_row, batch, reverse, final_inputs=None):
    t = xc.shape[0]
    q = SSD_STEP_CHUNKS * SSD_CHUNK
    nc = t // batch // q
    d_inner = SSD_HEADS * SSD_HEAD_DIM
    bcw = SSD_GROUPS * SSD_STATE

    def row(b, c):
        return b * nc + ((nc - 1 - c) if reverse else c)

    full = lambda a: pl.BlockSpec(a.shape, lambda b, c: (0,) * a.ndim)
    e64 = _head_expansion(SSD_HEADS if reverse else 0)
    in_specs = [pl.BlockSpec((q, d_inner), lambda b, c: (row(b, c), 0)),
                pl.BlockSpec((q, bcw), lambda b, c: (row(b, c), d_inner // bcw)),
                pl.BlockSpec((q, bcw), lambda b, c: (row(b, c), d_inner // bcw + 1)),
                pl.BlockSpec((q, LANES), lambda b, c: (row(b, c), 0)),
                full(alog_row), full(bias_row), full(e64)]
    args = [xc, xc, xc, dt_raw, alog_row, bias_row, e64]
    if reverse:
        body = _ssd_bwd_kernel
    else:
        body = _ssd_fwd_kernel
        big, y_bwd, dskip_row, ng_row = final_inputs
        ecs = _score_expansion()
        in_specs += [full(ecs),
                     pl.BlockSpec((q, d_inner), lambda b, c: (row(b, c), 0)),
                     pl.BlockSpec((q, d_inner), lambda b, c: (row(b, c), 0)),
                     full(dskip_row), full(ng_row)]
        args += [ecs, big, y_bwd, dskip_row, ng_row]
    return pl.pallas_call(
        body,
        grid=(batch, nc),
        in_specs=in_specs,
        out_specs=pl.BlockSpec((q, d_inner), lambda b, c: (row(b, c), 0)),
        out_shape=jax.ShapeDtypeStruct((t, d_inner), BF16),
        scratch_shapes=[pltpu.VMEM((SSD_GROUPS, SSD_STATE, SSD_HEADS_PER_GROUP * SSD_HEAD_DIM), F32)],
        compiler_params=_cparams(("arbitrary", "arbitrary")),
        name="ssd_bwd" if reverse else "ssd_fwd",
    )(*args)


def _mix_kernel(tiles_per_seq, seq, yn_ref, u_ref, up_ref, un_ref, ga_ref, gb_ref, x_ref,
                wssd_ref, pw_ref, wpo_ref, wo_ref, wr_ref,
                gateb_ref, pscale_ref, ln1g_ref, ln1b_ref, rb_ref, g1_ref, sh2_ref, sc2_ref,
                x1_ref, h2_ref, lg_ref, ext_ref):
    i = pl.program_id(0)
    tm, width = u_ref.shape
    first = (i % tiles_per_seq) == 0
    last = (i % tiles_per_seq) == tiles_per_seq - 1
    u = u_ref[...].astype(F32)
    ext_ref[0:HALO, :] = jnp.where(first, jnp.zeros_like(up_ref), up_ref[...])
    ext_ref[HALO:HALO + tm, :] = u_ref[...]
    ext_ref[HALO + tm:, :] = jnp.where(last, jnp.zeros_like(un_ref), un_ref[...])
    tpos = (i % tiles_per_seq) * tm + lax.broadcasted_iota(jnp.int32, (tm, 1), 0)
    gd = width // len(POOL_WINDOWS)
    sub = POOL_SUB
    win = sub + 2 * HALO
    ri = lax.broadcasted_iota(jnp.int32, (sub, win), 0) + HALO
    ci = lax.broadcasted_iota(jnp.int32, (sub, win), 1)
    mixed = []
    for gi, w in enumerate(POOL_WINDOWS):
        cols = slice(gi * gd, (gi + 1) * gd)
        band = jnp.where((ci >= ri - w // 2) & (ci < ri + w // 2), 1.0, 0.0).astype(BF16)
        s = jnp.concatenate([_dot(band, ext_ref[m * sub:m * sub + win, cols]) for m in range(tm // sub)], axis=0)
        cnt = (jnp.minimum(tpos + w // 2, seq) - jnp.maximum(tpos - w // 2, 0)).astype(F32)
        diff = s / cnt - u[:, cols]
        mixed.append(_dot(diff.astype(BF16), pw_ref[gi]))
    mixed = jnp.concatenate(mixed, axis=1) * pscale_ref[...]
    y_pool = _dot(mixed.astype(BF16), wpo_ref[...])
    y_ssd = _dot(yn_ref[...], wssd_ref[...])
    d = y_ssd.shape[1]
    g_ssd = jax.nn.sigmoid(ga_ref[...].astype(F32) + gateb_ref[:, 0:d])
    g_pool = jax.nn.sigmoid(gb_ref[...].astype(F32) + gateb_ref[:, d:2 * d])
    mix = _dot((g_ssd * y_ssd + g_pool * y_pool).astype(BF16), wo_ref[...])
    x1 = _layernorm(DEEPNORM_ALPHA * x_ref[...] + g1_ref[...] * mix) * ln1g_ref[...] + ln1b_ref[...]
    x1_ref[...] = x1
    h2 = _layernorm(x1) * (1.0 + sc2_ref[...]) + sh2_ref[...]
    h2_ref[...] = _pack_bf16_pairs(h2)
    lg_ref[...] = _dot3(h2, wr_ref[...]) + rb_ref[...]


def _mix(yn, big, x2, w_ssd_out, pool_w, w_pool_out, w_o, w_router, gate_b, pool_scale, ln1_g, ln1_b,
         r_bias, g1, sh2, sc2, seq, pool_col, gate_col):
    t, d = x2.shape
    tm = TM_MIX
    tiles_per_seq = seq // tm
    hb = tm // HALO
    nhb = t // HALO
    pcb = pool_col // d
    gcb = gate_col // d
    full = lambda a: pl.BlockSpec(a.shape, lambda i: (0,) * a.ndim)
    per_batch = pl.BlockSpec((None, 1, d), lambda i: (i // tiles_per_seq, 0, 0))
    return pl.pallas_call(
        functools.partial(_mix_kernel, tiles_per_seq, seq),
        grid=(t // tm,),
        in_specs=[pl.BlockSpec((tm, yn.shape[1]), lambda i: (i, 0)),
                  pl.BlockSpec((tm, d), lambda i: (i, pcb)),
                  pl.BlockSpec((HALO, d), lambda i: (jnp.maximum(i * hb - 1, 0), pcb)),
                  pl.BlockSpec((HALO, d), lambda i: (jnp.minimum((i + 1) * hb, nhb - 1), pcb)),
                  pl.BlockSpec((tm, d), lambda i: (i, gcb)),
                  pl.BlockSpec((tm, d), lambda i: (i, gcb + 1)),
                  pl.BlockSpec((tm, d), lambda i: (i, 0)),
                  full(w_ssd_out), full(pool_w), full(w_pool_out), full(w_o), full(w_router),
                  full(gate_b), full(pool_scale), full(ln1_g), full(ln1_b), full(r_bias),
                  per_batch, per_batch, per_batch],
        out_specs=[pl.BlockSpec((tm, d), lambda i: (i, 0)),
                   pl.BlockSpec((tm, d // 2), lambda i: (i, 0)),
                   pl.BlockSpec((tm, LANES), lambda i: (i, 0))],
        out_shape=[jax.ShapeDtypeStruct((t, d), F32),
                   jax.ShapeDtypeStruct((t, d // 2), jnp.int32),
                   jax.ShapeDtypeStruct((t, LANES), F32)],
        scratch_shapes=[pltpu.VMEM((tm + 2 * HALO, d), BF16)],
        compiler_params=_cparams(("arbitrary",)),
        name="mix_postln",
    )(yn, big, big, big, big, big, x2, w_ssd_out, pool_w, w_pool_out, w_o, w_router,
      gate_b, pool_scale, ln1_g, ln1_b, r_bias, g1, sh2, sc2)


def _route_kernel(lg_ref, rt_ref, cnt_ref, carry_ref):
    @pl.when(pl.program_id(0) == 0)
    def _():
        carry_ref[...] = jnp.zeros_like(carry_ref)

    lg = lg_ref[...]
    tm = lg.shape[0]
    lane = lax.broadcasted_iota(jnp.int32, lg.shape, 1).astype(F32)
    neg = -jnp.inf
    big_lane = float(LANES)
    gl = jnp.where(lane < MOE_GROUPS, lg, neg)
    gmax = jnp.max(gl, axis=-1, keepdims=True)
    g_w = 1.0 / jnp.sum(jnp.exp(gl - gmax), axis=-1, keepdims=True)
    g_idx = jnp.min(jnp.where(gl == gmax, lane, big_lane), axis=-1, keepdims=True)
    lo = MOE_GROUPS + MOE_EXPERTS_PER_GROUP * g_idx
    el = jnp.where((lane >= lo) & (lane < lo + MOE_EXPERTS_PER_GROUP), lg, neg)
    m1 = jnp.max(el, axis=-1, keepdims=True)
    i1 = jnp.min(jnp.where(el == m1, lane, big_lane), axis=-1, keepdims=True)
    el2 = jnp.where(lane == i1, neg, el)
    m2 = jnp.max(el2, axis=-1, keepdims=True)
    i2 = jnp.min(jnp.where(el2 == m2, lane, big_lane), axis=-1, keepdims=True)
    e = jnp.exp(m2 - m1)
    w1 = g_w / (1.0 + e)
    w2 = g_w * e / (1.0 + e)
    onehot = jnp.where((lane == i1) | (lane == i2), 1.0, 0.0)
    ri = lax.broadcasted_iota(jnp.int32, (tm, tm), 0)
    ci = lax.broadcasted_iota(jnp.int32, (tm, tm), 1)
    earlier = jnp.where(ci < ri, 1.0, 0.0).astype(BF16)
    rank = _dot(earlier, onehot.astype(BF16)) + carry_ref[...]
    r1 = jnp.sum(jnp.where(lane == i1, rank, 0.0), axis=-1, keepdims=True)
    r2 = jnp.sum(jnp.where(lane == i2, rank, 0.0), axis=-1, keepdims=True)
    carry_ref[...] = carry_ref[...] + jnp.sum(onehot, axis=0, keepdims=True)
    cnt_ref[...] = carry_ref[...]
    out = jnp.where(lane == 0, i1 - MOE_GROUPS, 0.0)
    out = jnp.where(lane == 1, i2 - MOE_GROUPS, out)
    out = jnp.where(lane == 2, r1, out)
    out = jnp.where(lane == 3, r2, out)
    out = jnp.where(lane == 4, w1, out)
    out = jnp.where(lane == 5, w2, out)
    rt_ref[...] = out


def _route(logits):
    t = logits.shape[0]
    tm = TM_ROUTE
    return pl.pallas_call(
        _route_kernel,
        grid=(t // tm,),
        in_specs=[pl.BlockSpec((tm, LANES), lambda i: (i, 0))],
        out_specs=[pl.BlockSpec((tm, LANES), lambda i: (i, 0)),
                   pl.BlockSpec((1, LANES), lambda i: (0, 0))],
        out_shape=[jax.ShapeDtypeStruct((t, LANES), F32),
                   jax.ShapeDtypeStruct((1, LANES), F32)],
        scratch_shapes=[pltpu.VMEM((1, LANES), F32)],
        compiler_params=_cparams(("arbitrary",)),
        name="route",
    )(logits)


def _slotmap_kernel(pos_ref, init_ref, inv_ref, sem):
    i = pl.program_id(0)
    tm = pos_ref.shape[1] // 2

    @pl.when(i == 0)
    def _():
        cp = pltpu.make_async_copy(init_ref, inv_ref, sem)
        cp.start()
        cp.wait()

    def body(r, carry):
        tok = i * tm + r
        inv_ref[pos_ref[0, 2 * r]] = tok
        inv_ref[pos_ref[0, 2 * r + 1]] = tok
        return carry

    lax.fori_loop(0, tm, body, 0, unroll=SLOTMAP_UNROLL)


def _slotmap(pos, init):
    t = pos.shape[0]
    tm = TM_SLOTMAP
    pos3 = pos.reshape(t // tm, 1, 2 * tm)
    return pl.pallas_call(
        _slotmap_kernel,
        grid=(t // tm,),
        in_specs=[pl.BlockSpec((None, 1, 2 * tm), lambda i: (i, 0, 0), memory_space=pltpu.SMEM),
                  pl.BlockSpec(memory_space=pl.ANY)],
        out_specs=pl.BlockSpec(memory_space=pltpu.SMEM),
        out_shape=jax.ShapeDtypeStruct(init.shape, jnp.int32),
        scratch_shapes=[pltpu.SemaphoreType.DMA(())],
        compiler_params=_cparams(("arbitrary",)),
        name="slotmap",
    )(pos3, init)


def _row_copy(src_ref, src_row, dst_ref, dst_row, sem):
    return pltpu.make_async_copy(src_ref.at[pl.ds(src_row, 1)], dst_ref.at[pl.ds(dst_row, 1)], sem)


def _expert_kernel(te_ref, nu_ref, inv0_ref, inv1_ref, inv2_ref, h2_ref, wg_ref, wu_ref, wd_ref,
                   y_ref, xbuf_ref, wgb_ref, wub_ref, wdb_ref, gsem):
    j = pl.program_id(0)
    n_used = nu_ref[0]
    tm = y_ref.shape[0]
    ring = xbuf_ref.shape[0]
    slot = j % ring

    def gather_start(inv_ref, s):
        for r in range(tm):
            cp = _row_copy(h2_ref, inv_ref[0, r], xbuf_ref.at[s], r, gsem.at[s])
            cp.start(priority=r % DMA_PRIORITIES)

    def gather_wait(s):
        pltpu.make_async_copy(h2_ref.at[pl.ds(0, tm)], xbuf_ref.at[s], gsem.at[s]).wait()

    @pl.when(j == 0)
    def _():
        gather_start(inv0_ref, 0)
        gather_start(inv1_ref, 1)

    new_expert = jnp.logical_or(j == 0, te_ref[j] != te_ref[jnp.maximum(j - 1, 0)])

    @pl.when(jnp.logical_and(j < n_used, new_expert))
    def _():
        wgb_ref[...] = wg_ref[...].astype(BF16)
        wub_ref[...] = wu_ref[...].astype(BF16)
        wdb_ref[...] = wd_ref[...].astype(BF16)

    @pl.when(j < n_used)
    def _():
        gather_wait(slot)
        xl, xr = [v.astype(BF16) for v in _unpack_bf16_pairs(xbuf_ref[slot])]
        gather_start(inv2_ref, (j + 2) % ring)
        half = xl.shape[1]
        gate = _dot(xl, wgb_ref[0:half, :]) + _dot(xr, wgb_ref[half:, :])
        up = _dot(xl, wub_ref[0:half, :]) + _dot(xr, wub_ref[half:, :])
        y = _dot((_silu(gate) * up).astype(BF16), wdb_ref[...])
        y_ref[...] = _pack_bf16_pairs(y)

    @pl.when(j >= n_used)
    def _():
        y_ref[...] = jnp.zeros_like(y_ref)

    @pl.when(j == n_used)
    def _():
        gather_wait(slot)
        gather_wait((j + 1) % ring)


def _experts(tile_expert, n_used, inv, h2, w_gate, w_up, w_down):
    _, d, hdim = w_gate.shape
    assert h2.shape[1] == d // 2 and h2.dtype == jnp.int32, "rows arrive as packed bf16 column pairs"
    tm = TM_EXPERT
    n_tiles = inv.shape[0] // tm
    inv3 = inv.reshape(n_tiles, 1, tm)
    last_used = lambda j, nu: jnp.minimum(j, nu[0] - 1)
    grid_spec = pltpu.PrefetchScalarGridSpec(
        num_scalar_prefetch=2,
        grid=(n_tiles + 1,),
        in_specs=[pl.BlockSpec((None, 1, tm), lambda j, te, nu: (jnp.minimum(j, n_tiles - 1), 0, 0),
                               memory_space=pltpu.SMEM),
                  pl.BlockSpec((None, 1, tm), lambda j, te, nu: (jnp.minimum(j + 1, n_tiles - 1), 0, 0),
                               memory_space=pltpu.SMEM),
                  pl.BlockSpec((None, 1, tm), lambda j, te, nu: (jnp.minimum(j + 2, n_tiles - 1), 0, 0),
                               memory_space=pltpu.SMEM),
                  pl.BlockSpec(memory_space=pl.ANY),
                  pl.BlockSpec((None, d, hdim), lambda j, te, nu: (te[last_used(j, nu)], 0, 0)),
                  pl.BlockSpec((None, d, hdim), lambda j, te, nu: (te[last_used(j, nu)], 0, 0)),
                  pl.BlockSpec((None, hdim, d), lambda j, te, nu: (te[last_used(j, nu)], 0, 0))],
        out_specs=pl.BlockSpec((tm, d // 2), lambda j, te, nu: (j, 0)),
        scratch_shapes=[pltpu.VMEM((GATHER_RING, tm, d // 2), jnp.int32),
                        pltpu.VMEM((d, hdim), BF16), pltpu.VMEM((d, hdim), BF16),
                        pltpu.VMEM((hdim, d), BF16), pltpu.SemaphoreType.DMA((GATHER_RING,))],
    )
    return pl.pallas_call(
        _expert_kernel,
        grid_spec=grid_spec,
        out_shape=jax.ShapeDtypeStruct(((n_tiles + 1) * tm, d // 2), jnp.int32),
        compiler_params=_cparams(("arbitrary",)),
        name="experts",
    )(tile_expert, n_used, inv3, inv3, inv3, h2, w_gate, w_up, w_down)


def _combine_kernel(pos0_ref, pos1_ref, pos2_ref, y_ref, rt_ref, x1_ref, g2_ref, lng_ref, lnb_ref, o_ref,
                    buf_ref, sem):
    i = pl.program_id(0)
    tm = x1_ref.shape[0]
    ring = buf_ref.shape[0]
    slot = i % ring

    def gather_start(pos_ref, s):
        for r in range(tm):
            for k in range(2):
                cp = _row_copy(y_ref, pos_ref[0, 2 * r + k], buf_ref.at[s, k], r, sem.at[s])
                cp.start(priority=k % DMA_PRIORITIES)

    def gather_wait(s):
        for k in range(2):
            pltpu.make_async_copy(y_ref.at[pl.ds(0, tm)], buf_ref.at[s, k], sem.at[s]).wait()

    @pl.when(i == 0)
    def _():
        gather_start(pos0_ref, 0)
        gather_start(pos1_ref, 1)

    gather_wait(slot)
    l0, r0 = _unpack_bf16_pairs(buf_ref[slot, 0])
    l1, r1 = _unpack_bf16_pairs(buf_ref[slot, 1])
    gather_start(pos2_ref, (i + 2) % ring)
    rt = rt_ref[...]
    w0, w1 = rt[:, 4:5], rt[:, 5:6]
    y_moe = jnp.concatenate([w0 * l0 + w1 * l1, w0 * r0 + w1 * r1], axis=1)
    v = DEEPNORM_ALPHA * x1_ref[...] + g2_ref[...] * y_moe
    o_ref[...] = _layernorm(v) * lng_ref[...] + lnb_ref[...]

    @pl.when(i == pl.num_programs(0) - 1)
    def _():
        gather_wait((i + 1) % ring)
        gather_wait((i + 2) % ring)


def _combine(pos, y_sorted, rt, x1, g2, ln_g, ln_b, seq):
    t, d = x1.shape
    tm = TM_COMBINE
    tiles_per_seq = seq // tm
    nt = t // tm
    pos3 = pos.reshape(nt, 1, 2 * tm)
    return pl.pallas_call(
        _combine_kernel,
        grid=(nt,),
        in_specs=[pl.BlockSpec((None, 1, 2 * tm), lambda i: (i, 0, 0), memory_space=pltpu.SMEM),
                  pl.BlockSpec((None, 1, 2 * tm), lambda i: (jnp.minimum(i + 1, nt - 1), 0, 0),
                               memory_space=pltpu.SMEM),
                  pl.BlockSpec((None, 1, 2 * tm), lambda i: (jnp.minimum(i + 2, nt - 1), 0, 0),
                               memory_space=pltpu.SMEM),
                  pl.BlockSpec(memory_space=pl.ANY),
                  pl.BlockSpec((tm, LANES), lambda i: (i, 0)),
                  pl.BlockSpec((tm, d), lambda i: (i, 0)),
                  pl.BlockSpec((None, 1, d), lambda i: (i // tiles_per_seq, 0, 0)),
                  pl.BlockSpec((1, d), lambda i: (0, 0)),
                  pl.BlockSpec((1, d), lambda i: (0, 0))],
        out_specs=pl.BlockSpec((tm, d), lambda i: (i, 0)),
        out_shape=jax.ShapeDtypeStruct((t, d), F32),
        scratch_shapes=[pltpu.VMEM((GATHER_RING, 2, tm, d // 2), jnp.int32),
                        pltpu.SemaphoreType.DMA((GATHER_RING,))],
        compiler_params=_cparams(("arbitrary",)),
        name="combine_postln",
    )(pos3, pos3, pos3, y_sorted, rt, x1, g2, ln_g, ln_b)


def _layer(x2, c_pad, batch, seq, w_ada, b_ada, w_in, conv_w, conv_b, a_log_f, a_log_b, dt_bias_f,
           dt_bias_b, d_skip, ssd_norm_g, w_ssd_out, pool_w, pool_scale, w_pool_out, gate_b, w_o,
           ln1_g, ln1_b, router_wg, router_bg, router_we, router_be, exp_w_gate, exp_w_up,
           exp_w_down, ln2_g, ln2_b):
    t, d = x2.shape
    d_inner = SSD_HEADS * SSD_HEAD_DIM
    conv_dim = d_inner + 2 * SSD_GROUPS * SSD_STATE
    pool_width = pool_scale.shape[0]

    mod = _ada(c_pad, w_ada, b_ada[None, :])[:batch]
    sh1, sc1, g1, sh2, sc2, g2 = [m[:, None, :] for m in jnp.split(mod, 6, axis=-1)]

    o_xbc = d_inner
    o_dt = o_xbc + conv_dim
    o_pool = o_dt + 2 * SSD_HEADS
    w_in_b = w_in.astype(BF16)
    w_main = jnp.concatenate([w_in_b[:, :o_dt], w_in_b[:, o_pool:]], axis=1)
    w_dt = jnp.pad(w_in[:, o_dt:o_pool], ((0, 0), (0, LANES - 2 * SSD_HEADS)))
    pool_col = o_dt
    gate_col = o_dt + pool_width
    big, dt_raw = _inproj(x2, sh1, sc1, w_main, w_dt, seq)

    xc = _conv(big, o_xbc, conv_w, conv_b[None, :], seq)

    pad_row = lambda f, b: jnp.pad(jnp.concatenate([f, b]), (0, LANES - 2 * SSD_HEADS))[None, :]
    alog_row = pad_row(a_log_f, a_log_b)
    bias_row = pad_row(dt_bias_f, dt_bias_b)
    dskip_row = jnp.repeat(d_skip, SSD_HEAD_DIM)[None, :]
    y_bwd = _ssd(xc, dt_raw, alog_row, bias_row, batch, reverse=True)
    yn = _ssd(xc, dt_raw, alog_row, bias_row, batch, reverse=False,
              final_inputs=(big, y_bwd, dskip_row, ssd_norm_g[None, :]))

    w_router = jnp.pad(jnp.concatenate([router_wg, router_we], axis=1),
                       ((0, 0), (0, LANES - MOE_GROUPS - MOE_EXPERTS)))
    r_bias = jnp.pad(jnp.concatenate([router_bg, router_be]), (0, LANES - MOE_GROUPS - MOE_EXPERTS))[None, :]
    x1, h2, logits = _mix(yn, big, x2, w_ssd_out.astype(BF16), pool_w.astype(BF16),
                          w_pool_out.astype(BF16), w_o.astype(BF16), w_router, gate_b[None, :],
                          pool_scale[None, :], ln1_g[None, :], ln1_b[None, :], r_bias, g1, sh2, sc2,
                          seq, pool_col, gate_col)

    rt, counts = _route(logits)

    tme = TM_EXPERT
    cnt = counts[0, MOE_GROUPS:MOE_GROUPS + MOE_EXPERTS].astype(jnp.int32)
    padded = ((cnt + tme - 1) // tme) * tme
    ends = jnp.cumsum(padded)
    off = ends - padded
    eid = rt[:, 0:2].astype(jnp.int32)
    pos = rt[:, 2:4].astype(jnp.int32) + jnp.sum(
        jnp.where(eid[:, :, None] == jnp.arange(MOE_EXPERTS, dtype=jnp.int32), off, 0), axis=-1)
    p_rows = 2 * t + MOE_EXPERTS * tme
    n_tiles = p_rows // tme
    tile_ends = ends // tme
    tile_expert = jnp.minimum(
        jnp.sum(jnp.arange(n_tiles + 1, dtype=jnp.int32)[:, None] >= tile_ends[None, :], axis=1),
        MOE_EXPERTS - 1).astype(jnp.int32)
    n_used = tile_ends[-1:].astype(jnp.int32)

    inv = _slotmap(pos, jnp.zeros((p_rows,), jnp.int32))
    y_sorted = _experts(tile_expert, n_used, inv, h2, exp_w_gate, exp_w_up, exp_w_down)
    return _combine(pos, y_sorted, rt, x1, g2, ln2_g[None, :], ln2_b[None, :], seq)


def kernel(x, c, w_ada, b_ada, w_in, conv_w, conv_b, a_log_f, a_log_b, dt_bias_f, dt_bias_b, d_skip,
           ssd_norm_g, w_ssd_out, pool_w, pool_scale, w_pool_out, gate_b, w_o, ln1_g, ln1_b,
           router_wg, router_bg, router_we, router_be, exp_w_gate, exp_w_up, exp_w_down, ln2_g, ln2_b):
    batch, seq, d = x.shape
    x2 = x.reshape(batch * seq, d)
    c_pad = jnp.pad(c, ((0, 8 - batch), (0, 0)))
    params = (w_ada, b_ada, w_in, conv_w, conv_b, a_log_f, a_log_b, dt_bias_f, dt_bias_b, d_skip,
              ssd_norm_g, w_ssd_out, pool_w, pool_scale, w_pool_out, gate_b, w_o, ln1_g, ln1_b,
              router_wg, router_bg, router_we, router_be, exp_w_gate, exp_w_up, exp_w_down, ln2_g, ln2_b)
    for l in range(w_ada.shape[0]):
        x2 = _layer(x2, c_pad, batch, seq, *[p[l] for p in params])
    return x2.reshape(batch, seq, d)
```

```python
import functools

import jax
import jax.numpy as jnp
from jax import lax
from jax.experimental import pallas as pl
from jax.experimental.pallas import tpu as pltpu

F32 = jnp.float32
BF16 = jnp.bfloat16
HIGHEST = lax.Precision.HIGHEST

SSD_HEAD_DIM = 64
SSD_GROUPS = 8
SSD_HEADS_PER_GROUP = 4
SSD_HEADS = SSD_GROUPS * SSD_HEADS_PER_GROUP
SSD_STATE = 128
SSD_CONV = 5
SSD_CHUNK = 128
SSD_NORM_EPS = 1e-5
POOL_WINDOWS = (2, 4, 8, 16)
MOE_GROUPS = 4
MOE_EXPERTS_PER_GROUP = 8
MOE_EXPERTS = MOE_GROUPS * MOE_EXPERTS_PER_GROUP
DEPTH = 1
DEEPNORM_ALPHA = (2.0 * DEPTH) ** 0.25
LN_EPS = 1e-5

LANES = 128
HALO = 16
VMEM_LIMIT = 48 * 1024 * 1024
SLOTMAP_UNROLL = 8
DMA_PRIORITIES = 2
GATHER_RING = 3

TM_INPROJ = 1024
TN_INPROJ = 2304
TM_CONV = 1024
CONV_SUB = 128
SSD_STEP_CHUNKS = 2
TM_MIX = 256
POOL_SUB = 128
TM_ROUTE = 512
TM_SLOTMAP = 1024
TM_EXPERT = 256
TM_COMBINE = 512


def _dot(a, b):
    return jnp.dot(a, b, preferred_element_type=F32)


def _split_hi_lo(v):
    hi = v.astype(BF16)
    lo = (v - hi.astype(F32)).astype(BF16)
    return hi, lo


def _split3(v):
    hi = v.astype(BF16)
    r = v - hi.astype(F32)
    mid = r.astype(BF16)
    lo = (r - mid.astype(F32)).astype(BF16)
    return hi, mid, lo


def _dot3(a, b):
    a_hi, a_lo = _split_hi_lo(a)
    b_hi, b_lo = _split_hi_lo(b)
    return _dot(a_hi, b_hi) + _dot(a_lo, b_hi) + _dot(a_hi, b_lo)


def _layernorm(v):
    mu = jnp.mean(v, axis=-1, keepdims=True)
    vc = v - mu
    var = jnp.mean(vc * vc, axis=-1, keepdims=True)
    return vc * lax.rsqrt(var + LN_EPS)


def _silu(v):
    return v * jax.nn.sigmoid(v)


def _pack_bf16_pairs(v):
    m = v.shape[1] // 2
    hi = lax.bitcast_convert_type(v[:, :m].astype(BF16).astype(F32), jnp.int32)
    lo = lax.bitcast_convert_type(v[:, m:].astype(BF16).astype(F32), jnp.int32)
    return hi | lax.shift_right_logical(lo, 16)


def _unpack_bf16_pairs(p):
    hi = lax.bitcast_convert_type(p & jnp.int32(-65536), F32)
    lo = lax.bitcast_convert_type(lax.shift_left(p, 16), F32)
    return hi, lo


def _cparams(sem):
    return pltpu.CompilerParams(dimension_semantics=sem, vmem_limit_bytes=VMEM_LIMIT)


def _ada_kernel(c_ref, w_ref, b_ref, o_ref):
    o_ref[...] = jnp.dot(_silu(c_ref[...]), w_ref[...], precision=HIGHEST,
                         preferred_element_type=F32) + b_ref[...]


def _ada(c_pad, w, b):
    d, n = w.shape
    tn = 1024
    return pl.pallas_call(
        _ada_kernel,
        grid=(n // tn,),
        in_specs=[pl.BlockSpec((c_pad.shape[0], d), lambda j: (0, 0)),
                  pl.BlockSpec((d, tn), lambda j: (0, j)),
                  pl.BlockSpec((1, tn), lambda j: (0, j))],
        out_specs=pl.BlockSpec((c_pad.shape[0], tn), lambda j: (0, j)),
        out_shape=jax.ShapeDtypeStruct((c_pad.shape[0], n), F32),
        compiler_params=_cparams(("arbitrary",)),
        name="ada_mod",
    )(c_pad, w, b)


def _inproj_kernel(x_ref, sh_ref, sc_ref, w_ref, wdt_ref, o_ref, dt_ref, h_ref):
    @pl.when(pl.program_id(1) == 0)
    def _():
        h = _layernorm(x_ref[...]) * (1.0 + sc_ref[...]) + sh_ref[...]
        h_ref[...] = h.astype(BF16)
        dt_ref[...] = _dot3(h, wdt_ref[...])

    o_ref[...] = _dot(h_ref[...], w_ref[...]).astype(BF16)


def _inproj(x2, sh, sc, w_main, w_dt, seq):
    t, d = x2.shape
    n = w_main.shape[1]
    tm, tn = TM_INPROJ, TN_INPROJ
    tiles_per_seq = seq // tm
    return pl.pallas_call(
        _inproj_kernel,
        grid=(t // tm, n // tn),
        in_specs=[pl.BlockSpec((tm, d), lambda i, j: (i, 0)),
                  pl.BlockSpec((None, 1, d), lambda i, j: (i // tiles_per_seq, 0, 0)),
                  pl.BlockSpec((None, 1, d), lambda i, j: (i // tiles_per_seq, 0, 0)),
                  pl.BlockSpec((d, tn), lambda i, j: (0, j)),
                  pl.BlockSpec((d, LANES), lambda i, j: (0, 0))],
        out_specs=[pl.BlockSpec((tm, tn), lambda i, j: (i, j)),
                   pl.BlockSpec((tm, LANES), lambda i, j: (i, 0))],
        out_shape=[jax.ShapeDtypeStruct((t, n), BF16),
                   jax.ShapeDtypeStruct((t, LANES), F32)],
        scratch_shapes=[pltpu.VMEM((tm, d), BF16)],
        compiler_params=_cparams(("arbitrary", "arbitrary")),
        name="ln_inproj",
    )(x2, sh, sc, w_main, w_dt)


def _conv_kernel(tiles_per_seq, cur_ref, prev_ref, next_ref, w_ref, b_ref, o_ref, ext_ref):
    i = pl.program_id(0)
    tm = cur_ref.shape[0]
    first = (i % tiles_per_seq) == 0
    last = (i % tiles_per_seq) == tiles_per_seq - 1
    ext_ref[0:HALO, :] = jnp.where(first, jnp.zeros_like(prev_ref), prev_ref[...])
    ext_ref[HALO:HALO + tm, :] = cur_ref[...]
    ext_ref[HALO + tm:, :] = jnp.where(last, jnp.zeros_like(next_ref), next_ref[...])
    pad = SSD_CONV // 2
    sub = CONV_SUB
    win = sub + 2 * HALO
    ri = lax.broadcasted_iota(jnp.int32, (sub, win), 0)
    ci = lax.broadcasted_iota(jnp.int32, (sub, win), 1)
    picks = [jnp.where(ci == ri + HALO + k - pad, 1.0, 0.0).astype(BF16) for k in range(SSD_CONV)]
    for m in range(tm // sub):
        window = ext_ref[m * sub:m * sub + win, :]
        acc = b_ref[...] + w_ref[pad:pad + 1, :] * ext_ref[HALO + m * sub:HALO + (m + 1) * sub, :].astype(F32)
        for k in range(SSD_CONV):
            if k != pad:
                acc = acc + w_ref[k:k + 1, :] * _dot(picks[k], window)
        o_ref[m * sub:(m + 1) * sub, :] = _silu(acc).astype(BF16)


def _conv(big, col_off, conv_w, conv_b, seq):
    t = big.shape[0]
    cdim = conv_w.shape[1]
    tm, tc = TM_CONV, 1024
    tiles_per_seq = seq // tm
    cb0 = col_off // tc
    hb = tm // HALO
    nhb = t // HALO
    return pl.pallas_call(
        functools.partial(_conv_kernel, tiles_per_seq),
        grid=(t // tm, cdim // tc),
        in_specs=[pl.BlockSpec((tm, tc), lambda i, j: (i, cb0 + j)),
                  pl.BlockSpec((HALO, tc), lambda i, j: (jnp.maximum(i * hb - 1, 0), cb0 + j)),
                  pl.BlockSpec((HALO, tc), lambda i, j: (jnp.minimum((i + 1) * hb, nhb - 1), cb0 + j)),
                  pl.BlockSpec((SSD_CONV, tc), lambda i, j: (0, j)),
                  pl.BlockSpec((1, tc), lambda i, j: (0, j))],
        out_specs=pl.BlockSpec((tm, tc), lambda i, j: (i, j)),
        out_shape=jax.ShapeDtypeStruct((t, cdim), BF16),
        scratch_shapes=[pltpu.VMEM((tm + 2 * HALO, tc), BF16)],
        compiler_params=_cparams(("arbitrary", "arbitrary")),
        name="conv_silu",
    )(big, big, big, conv_w, conv_b)


def _tri_cumsum(tri, v):
    hi, mid, lo = _split3(v)
    return _dot(tri, hi) + _dot(tri, mid) + _dot(tri, lo)


def _expand_heads(v, e_ref):
    hi, lo = _split_hi_lo(v)
    return _dot(jnp.concatenate([hi, lo], axis=1), e_ref[...])


def _state_step(g, x_g, b_ref, c_ref, st_ref, w_exp, dec_exp, end_row):
    gw = x_g.shape[1]
    cols = slice(g * gw, (g + 1) * gw)
    bg = b_ref[:, g * SSD_STATE:(g + 1) * SSD_STATE]
    cg = c_ref[:, g * SSD_STATE:(g + 1) * SSD_STATE]
    h_in = st_ref[g]
    y_off = _dot(cg, h_in.astype(BF16)) * dec_exp[:, cols]
    xw = (x_g * w_exp[:, cols]).astype(BF16)
    s_new = lax.dot_general(bg, xw, (((0,), (0,)), ((), ())), preferred_element_type=F32)
    st_ref[g] = h_in * dec_exp[end_row:end_row + 1, cols] + s_new
    return y_off


def _chunk_views(refs, ch):
    return [r.at[pl.ds(ch * SSD_CHUNK, SSD_CHUNK)] for r in refs]


def _ssd_bwd_kernel(x_ref, b_ref, c_ref, dt_ref, alog_ref, bias_ref, e64_ref, o_ref, st_ref):
    @pl.when(pl.program_id(1) == 0)
    def _():
        st_ref[...] = jnp.zeros_like(st_ref)

    for ch in reversed(range(x_ref.shape[0] // SSD_CHUNK)):
        x_v, b_v, c_v, dt_v, o_v = _chunk_views((x_ref, b_ref, c_ref, dt_ref, o_ref), ch)
        _ssd_bwd_chunk(x_v, b_v, c_v, dt_v, alog_ref, bias_ref, e64_ref, o_v, st_ref)


def _ssd_bwd_chunk(x_ref, b_ref, c_ref, dt_ref, alog_ref, bias_ref, e64_ref, o_ref, st_ref):
    q = x_ref.shape[0]
    gw = SSD_HEADS_PER_GROUP * SSD_HEAD_DIM
    dt = jax.nn.softplus(dt_ref[...] + bias_ref[...])
    da = dt * (-jnp.exp(alog_ref[...]))
    li = lax.broadcasted_iota(jnp.int32, (q, q), 0)
    si = lax.broadcasted_iota(jnp.int32, (q, q), 1)
    cs = _tri_cumsum(jnp.where(si >= li, 1.0, 0.0).astype(BF16), da)
    w_exp = _expand_heads(dt * jnp.exp(cs[0:1, :] - cs), e64_ref)
    dec_exp = _expand_heads(jnp.exp(cs), e64_ref)
    for g in range(SSD_GROUPS):
        cols = slice(g * gw, (g + 1) * gw)
        x_g = x_ref[:, cols].astype(F32)
        o_ref[:, cols] = _state_step(g, x_g, b_ref, c_ref, st_ref, w_exp, dec_exp, 0).astype(BF16)


def _ssd_fwd_kernel(x_ref, b_ref, c_ref, dt_ref, alog_ref, bias_ref, e64_ref, ecs_ref, z_ref,
                    yb_ref, dskip_ref, ng_ref, o_ref, st_ref):
    @pl.when(pl.program_id(1) == 0)
    def _():
        st_ref[...] = jnp.zeros_like(st_ref)

    for ch in range(x_ref.shape[0] // SSD_CHUNK):
        x_v, b_v, c_v, dt_v, z_v, yb_v, o_v = _chunk_views(
            (x_ref, b_ref, c_ref, dt_ref, z_ref, yb_ref, o_ref), ch)
        _ssd_fwd_chunk(x_v, b_v, c_v, dt_v, alog_ref, bias_ref, e64_ref, ecs_ref, z_v, yb_v, dskip_ref,
                       ng_ref, o_v, st_ref)


def _ssd_fwd_chunk(x_ref, b_ref, c_ref, dt_ref, alog_ref, bias_ref, e64_ref, ecs_ref, z_ref,
                   yb_ref, dskip_ref, ng_ref, o_ref, st_ref):
    q = x_ref.shape[0]
    p = SSD_HEAD_DIM
    hpg = SSD_HEADS_PER_GROUP
    gw = hpg * p
    dt = jax.nn.softplus(dt_ref[...] + bias_ref[...])
    da = dt * (-jnp.exp(alog_ref[...]))
    li = lax.broadcasted_iota(jnp.int32, (q, q), 0)
    si = lax.broadcasted_iota(jnp.int32, (q, q), 1)
    before = si <= li
    after = si >= li
    lane = lax.broadcasted_iota(jnp.int32, (q, LANES), 1)
    tri2 = jnp.concatenate([jnp.where(before, 1.0, 0.0), jnp.where(after, 1.0, 0.0)], axis=1).astype(BF16)
    da2 = jnp.concatenate([jnp.where(lane < SSD_HEADS, da, 0.0),
                           jnp.where((lane >= SSD_HEADS) & (lane < 2 * SSD_HEADS), da, 0.0)], axis=0)
    cs = _tri_cumsum(tri2, da2)
    csp_t = (cs - jnp.log(dt)).T
    hi, mid, lo = _split3(cs)
    nhd = 2 * SSD_HEADS
    v_cs = jnp.concatenate([hi[:, :nhd].astype(F32), mid[:, :nhd].astype(F32), lo[:, :nhd].astype(F32),
                            jnp.zeros((q, nhd), F32)], axis=1).astype(BF16)
    to_end = jnp.where(lane < SSD_HEADS, cs[q - 1:q, :] - cs, 0.0)
    w_exp = _expand_heads(dt * jnp.exp(to_end), e64_ref)
    dec_exp = _expand_heads(jnp.exp(cs), e64_ref)
    half =lax.broadcasted_iota(jnp.int32, (q, 2 * p), 1) < p

    for g in range(SSD_GROUPS):
        cols = slice(g * gw, (g + 1) * gw)
        bg = b_ref[:, g * SSD_STATE:(g + 1) * SSD_STATE]
        cg = c_ref[:, g * SSD_STATE:(g + 1) * SSD_STATE]
        cb = lax.dot_general(cg, bg, (((1,), (1,)), ((), ())), preferred_element_type=F32)
        colb = _dot(v_cs, ecs_ref[:, g * 2 * hpg * q:(g + 1) * 2 * hpg * q])
        x_g = x_ref[:, cols].astype(F32)
        ms = []
        for r in range(hpg):
            hh = g * hpg + r
            seg_f = colb[:, r * q:(r + 1) * q] - csp_t[hh:hh + 1, :]
            seg_b = colb[:, (hpg + r) * q:(hpg + r + 1) * q] - csp_t[SSD_HEADS + hh:SSD_HEADS + hh + 1, :]
            decay = jnp.exp(jnp.where(before, seg_f, -jnp.inf)) + jnp.exp(jnp.where(after, seg_b, -jnp.inf))
            ms.append((cb * decay).astype(BF16))
        ys = []
        for pr in range(hpg // 2):
            xp = x_g[:, 2 * pr * p:2 * (pr + 1) * p]
            rhs = jnp.concatenate([jnp.where(half, xp, 0.0), jnp.where(half, 0.0, xp)], axis=0).astype(BF16)
            ys.append(_dot(jnp.concatenate([ms[2 * pr], ms[2 * pr + 1]], axis=1), rhs))
        y_g = jnp.concatenate(ys, axis=1)
        y_g = y_g + _state_step(g, x_g, b_ref, c_ref, st_ref, w_exp, dec_exp, q - 1)
        y_g = y_g + yb_ref[:, cols].astype(F32) + x_g * dskip_ref[:, cols]
        yg = y_g * _silu(z_ref[:, cols].astype(F32))
        ms_g = jnp.mean(yg * yg, axis=-1, keepdims=True)
        o_ref[:, cols] = (yg * lax.rsqrt(ms_g + SSD_NORM_EPS) * ng_ref[:, cols]).astype(BF16)


def _head_expansion(col0):
    j = jnp.arange(2 * LANES)[:, None]
    c = jnp.arange(SSD_HEADS * SSD_HEAD_DIM)[None, :]
    return ((j % LANES - col0) == c // SSD_HEAD_DIM).astype(BF16)


def _score_expansion():
    nhd = 2 * SSD_HEADS
    row = jnp.arange(4 * nhd)[:, None]
    col = jnp.arange(nhd * SSD_CHUNK)[None, :]
    j = row % nhd
    head = j % SSD_HEADS
    blk = ((head // SSD_HEADS_PER_GROUP) * 2 + j // SSD_HEADS) * SSD_HEADS_PER_GROUP + head % SSD_HEADS_PER_GROUP
    return ((row < 3 * nhd) & (blk == col // SSD_CHUNK)).astype(BF16)


def _ssd(xc, dt_raw, alog_row, bias_row, batch, reverse, final_inputs=None):
    t = xc.shape[0]
    q = SSD_STEP_CHUNKS * SSD_CHUNK
    nc = t // batch // q
    d_inner = SSD_HEADS * SSD_HEAD_DIM
    bcw = SSD_GROUPS * SSD_STATE

    def row(b, c):
        return b * nc + ((nc - 1 - c) if reverse else c)

    full = lambda a: pl.BlockSpec(a.shape, lambda b, c: (0,) * a.ndim)
    e64 = _head_expansion(SSD_HEADS if reverse else 0)
    in_specs = [pl.BlockSpec((q, d_inner), lambda b, c: (row(b, c), 0)),
                pl.BlockSpec((q, bcw), lambda b, c: (row(b, c), d_inner // bcw)),
                pl.BlockSpec((q, bcw), lambda b, c: (row(b, c), d_inner // bcw + 1)),
                pl.BlockSpec((q, LANES), lambda b, c: (row(b, c), 0)),
                full(alog_row), full(bias_row), full(e64)]
    args = [xc, xc, xc, dt_raw, alog_row, bias_row, e64]
    if reverse:
        body = _ssd_bwd_kernel
    else:
        body = _ssd_fwd_kernel
        big, y_bwd, dskip_row, ng_row = final_inputs
        ecs = _score_expansion()
        in_specs += [full(ecs),
                     pl.BlockSpec((q, d_inner), lambda b, c: (row(b, c), 0)),
                     pl.BlockSpec((q, d_inner), lambda b, c: (row(b, c), 0)),
                     full(dskip_row), full(ng_row)]
        args += [ecs, big, y_bwd, dskip_row, ng_row]
    return pl.pallas_call(
        body,
        grid=(batch, nc),
        in_specs=in_specs,
        out_specs=pl.BlockSpec((q, d_inner), lambda b, c: (row(b, c), 0)),
        out_shape=jax.ShapeDtypeStruct((t, d_inner), BF16),
        scratch_shapes=[pltpu.VMEM((SSD_GROUPS, SSD_STATE, SSD_HEADS_PER_GROUP * SSD_HEAD_DIM), F32)],
        compiler_params=_cparams(("arbitrary", "arbitrary")),
        name="ssd_bwd" if reverse else "ssd_fwd",
    )(*args)


def _mix_kernel(tiles_per_seq, seq, yn_ref, u_ref, up_ref, un_ref, ga_ref, gb_ref, x_ref,
                wssd_ref, pw_ref, wpo_ref, wo_ref, wr_ref,
                gateb_ref, pscale_ref, ln1g_ref, ln1b_ref, rb_ref, g1_ref, sh2_ref, sc2_ref,
                x1_ref, h2_ref, lg_ref, ext_ref):
    i = pl.program_id(0)
    tm, width = u_ref.shape
    first = (i % tiles_per_seq) == 0
    last = (i % tiles_per_seq) == tiles_per_seq - 1
    u = u_ref[...].astype(F32)
    ext_ref[0:HALO, :] = jnp.where(first, jnp.zeros_like(up_ref), up_ref[...])
    ext_ref[HALO:HALO + tm, :] = u_ref[...]
    ext_ref[HALO + tm:, :] = jnp.where(last, jnp.zeros_like(un_ref), un_ref[...])
    tpos = (i % tiles_per_seq) * tm + lax.broadcasted_iota(jnp.int32, (tm, 1), 0)
    gd = width // len(POOL_WINDOWS)
    sub = POOL_SUB
    win = sub + 2 * HALO
    ri = lax.broadcasted_iota(jnp.int32, (sub, win), 0) + HALO
    ci = lax.broadcasted_iota(jnp.int32, (sub, win), 1)
    mixed = []
    for gi, w in enumerate(POOL_WINDOWS):
        cols = slice(gi * gd, (gi + 1) * gd)
        band = jnp.where((ci >= ri - w // 2) & (ci < ri + w // 2), 1.0, 0.0).astype(BF16)
        s = jnp.concatenate([_dot(band, ext_ref[m * sub:m * sub + win, cols]) for m in range(tm // sub)], axis=0)
        cnt = (jnp.minimum(tpos + w // 2, seq) - jnp.maximum(tpos - w // 2, 0)).astype(F32)
        diff = s / cnt - u[:, cols]
        mixed.append(_dot(diff.astype(BF16), pw_ref[gi]))
    mixed = jnp.concatenate(mixed, axis=1) * pscale_ref[...]
    y_pool = _dot(mixed.astype(BF16), wpo_ref[...])
    y_ssd = _dot(yn_ref[...], wssd_ref[...])
    d = y_ssd.shape[1]
    g_ssd = jax.nn.sigmoid(ga_ref[...].astype(F32) + gateb_ref[:, 0:d])
    g_pool = jax.nn.sigmoid(gb_ref[...].astype(F32) + gateb_ref[:, d:2 * d])
    mix = _dot((g_ssd * y_ssd + g_pool * y_pool).astype(BF16), wo_ref[...])
    x1 = _layernorm(DEEPNORM_ALPHA * x_ref[...] + g1_ref[...] * mix) * ln1g_ref[...] + ln1b_ref[...]
    x1_ref[...] = x1
    h2 = _layernorm(x1) * (1.0 + sc2_ref[...]) + sh2_ref[...]
    h2_ref[...] = _pack_bf16_pairs(h2)
    lg_ref[...] = _dot3(h2, wr_ref[...]) + rb_ref[...]


def _mix(yn, big, x2, w_ssd_out, pool_w, w_pool_out, w_o, w_router, gate_b, pool_scale, ln1_g, ln1_b,
         r_bias, g1, sh2, sc2, seq, pool_col, gate_col):
    t, d = x2.shape
    tm = TM_MIX
    tiles_per_seq = seq // tm
    hb = tm // HALO
    nhb = t // HALO
    pcb = pool_col // d
    gcb = gate_col // d
    full = lambda a: pl.BlockSpec(a.shape, lambda i: (0,) * a.ndim)
    per_batch = pl.BlockSpec((None, 1, d), lambda i: (i // tiles_per_seq, 0, 0))
    return pl.pallas_call(
        functools.partial(_mix_kernel, tiles_per_seq, seq),
        grid=(t // tm,),
        in_specs=[pl.BlockSpec((tm, yn.shape[1]), lambda i: (i, 0)),
                  pl.BlockSpec((tm, d), lambda i: (i, pcb)),
                  pl.BlockSpec((HALO, d), lambda i: (jnp.maximum(i * hb - 1, 0), pcb)),
                  pl.BlockSpec((HALO, d), lambda i: (jnp.minimum((i + 1) * hb, nhb - 1), pcb)),
                  pl.BlockSpec((tm, d), lambda i: (i, gcb)),
                  pl.BlockSpec((tm, d), lambda i: (i, gcb + 1)),
                  pl.BlockSpec((tm, d), lambda i: (i, 0)),
                  full(w_ssd_out), full(pool_w), full(w_pool_out), full(w_o), full(w_router),
                  full(gate_b), full(pool_scale), full(ln1_g), full(ln1_b), full(r_bias),
                  per_batch, per_batch, per_batch],
        out_specs=[pl.BlockSpec((tm, d), lambda i: (i, 0)),
                   pl.BlockSpec((tm, d // 2), lambda i: (i, 0)),
                   pl.BlockSpec((tm, LANES), lambda i: (i, 0))],
        out_shape=[jax.ShapeDtypeStruct((t, d), F32),
                   jax.ShapeDtypeStruct((t, d // 2), jnp.int32),
                   jax.ShapeDtypeStruct((t, LANES), F32)],
        scratch_shapes=[pltpu.VMEM((tm + 2 * HALO, d), BF16)],
        compiler_params=_cparams(("arbitrary",)),
        name="mix_postln",
    )(yn, big, big, big, big, big, x2, w_ssd_out, pool_w, w_pool_out, w_o, w_router,
      gate_b, pool_scale, ln1_g, ln1_b, r_bias, g1, sh2, sc2)


def _route_kernel(lg_ref, rt_ref, cnt_ref, carry_ref):
    @pl.when(pl.program_id(0) == 0)
    def _():
        carry_ref[...] = jnp.zeros_like(carry_ref)

    lg = lg_ref[...]
    tm = lg.shape[0]
    lane = lax.broadcasted_iota(jnp.int32, lg.shape, 1).astype(F32)
    neg = -jnp.inf
    big_lane = float(LANES)
    gl = jnp.where(lane < MOE_GROUPS, lg, neg)
    gmax = jnp.max(gl, axis=-1, keepdims=True)
    g_w = 1.0 / jnp.sum(jnp.exp(gl - gmax), axis=-1, keepdims=True)
    g_idx = jnp.min(jnp.where(gl == gmax, lane, big_lane), axis=-1, keepdims=True)
    lo = MOE_GROUPS + MOE_EXPERTS_PER_GROUP * g_idx
    el = jnp.where((lane >= lo) & (lane < lo + MOE_EXPERTS_PER_GROUP), lg, neg)
    m1 = jnp.max(el, axis=-1, keepdims=True)
    i1 = jnp.min(jnp.where(el == m1, lane, big_lane), axis=-1, keepdims=True)
    el2 = jnp.where(lane == i1, neg, el)
    m2 = jnp.max(el2, axis=-1, keepdims=True)
    i2 = jnp.min(jnp.where(el2 == m2, lane, big_lane), axis=-1, keepdims=True)
    e = jnp.exp(m2 - m1)
    w1 = g_w / (1.0 + e)
    w2 = g_w * e / (1.0 + e)
    onehot = jnp.where((lane == i1) | (lane == i2), 1.0, 0.0)
    ri = lax.broadcasted_iota(jnp.int32, (tm, tm), 0)
    ci = lax.broadcasted_iota(jnp.int32, (tm, tm), 1)
    earlier = jnp.where(ci < ri, 1.0, 0.0).astype(BF16)
    rank = _dot(earlier, onehot.astype(BF16)) + carry_ref[...]
    r1 = jnp.sum(jnp.where(lane == i1, rank, 0.0), axis=-1, keepdims=True)
    r2 = jnp.sum(jnp.where(lane == i2, rank, 0.0), axis=-1, keepdims=True)
    carry_ref[...] = carry_ref[...] + jnp.sum(onehot, axis=0, keepdims=True)
    cnt_ref[...] = carry_ref[...]
    out = jnp.where(lane == 0, i1 - MOE_GROUPS, 0.0)
    out = jnp.where(lane == 1, i2 - MOE_GROUPS, out)
    out = jnp.where(lane == 2, r1, out)
    out = jnp.where(lane == 3, r2, out)
    out = jnp.where(lane == 4, w1, out)
    out = jnp.where(lane == 5, w2, out)
    rt_ref[...] = out


def _route(logits):
    t = logits.shape[0]
    tm = TM_ROUTE
    return pl.pallas_call(
        _route_kernel,
        grid=(t // tm,),
        in_specs=[pl.BlockSpec((tm, LANES), lambda i: (i, 0))],
        out_specs=[pl.BlockSpec((tm, LANES), lambda i: (i, 0)),
                   pl.BlockSpec((1, LANES), lambda i: (0, 0))],
        out_shape=[jax.ShapeDtypeStruct((t, LANES), F32),
                   jax.ShapeDtypeStruct((1, LANES), F32)],
        scratch_shapes=[pltpu.VMEM((1, LANES), F32)],
        compiler_params=_cparams(("arbitrary",)),
        name="route",
    )(logits)


def _slotmap_kernel(pos_ref, init_ref, inv_ref, sem):
    i = pl.program_id(0)
    tm = pos_ref.shape[1] // 2

    @pl.when(i == 0)
    def _():
        cp = pltpu.make_async_copy(init_ref, inv_ref, sem)
        cp.start()
        cp.wait()

    def body(r, carry):
        tok = i * tm + r
        inv_ref[pos_ref[0, 2 * r]] = tok
        inv_ref[pos_ref[0, 2 * r + 1]] = tok
        return carry

    lax.fori_loop(0, tm, body, 0, unroll=SLOTMAP_UNROLL)


def _slotmap(pos, init):
    t = pos.shape[0]
    tm = TM_SLOTMAP
    pos3 = pos.reshape(t // tm, 1, 2 * tm)
    return pl.pallas_call(
        _slotmap_kernel,
        grid=(t // tm,),
        in_specs=[pl.BlockSpec((None, 1, 2 * tm), lambda i: (i, 0, 0), memory_space=pltpu.SMEM),
                  pl.BlockSpec(memory_space=pl.ANY)],
        out_specs=pl.BlockSpec(memory_space=pltpu.SMEM),
        out_shape=jax.ShapeDtypeStruct(init.shape, jnp.int32),
        scratch_shapes=[pltpu.SemaphoreType.DMA(())],
        compiler_params=_cparams(("arbitrary",)),
        name="slotmap",
    )(pos3, init)


def _row_copy(src_ref, src_row, dst_ref, dst_row, sem):
    return pltpu.make_async_copy(src_ref.at[pl.ds(src_row, 1)], dst_ref.at[pl.ds(dst_row, 1)], sem)


def _expert_kernel(te_ref, nu_ref, inv0_ref, inv1_ref, inv2_ref, h2_ref, wg_ref, wu_ref, wd_ref,
                   y_ref, xbuf_ref, wgb_ref, wub_ref, wdb_ref, gate_ref, act_ref, gsem):
    j = pl.program_id(0)
    n_used = nu_ref[0]
    tm = y_ref.shape[0]
    ring = xbuf_ref.shape[0]
    slot = j % ring

    def gather_start(inv_ref, s, lo, hi):
        for r in range(lo, hi):
            cp = _row_copy(h2_ref, inv_ref[0, r], xbuf_ref.at[s], r, gsem.at[s])
            cp.start(priority=r % DMA_PRIORITIES)

    def gather_wait(s):
        pltpu.make_async_copy(h2_ref.at[pl.ds(0, tm)], xbuf_ref.at[s], gsem.at[s]).wait()

    @pl.when(j == 0)
    def _():
        gather_start(inv0_ref, 0, 0, tm)
        gather_start(inv1_ref, 1, 0, tm)

    new_expert = jnp.logical_or(j == 0, te_ref[j] != te_ref[jnp.maximum(j - 1, 0)])

    @pl.when(jnp.logical_and(j < n_used, new_expert))
    def _():
        wgb_ref[...] = wg_ref[...].astype(BF16)
        wub_ref[...] = wu_ref[...].astype(BF16)
        wdb_ref[...] = wd_ref[...].astype(BF16)

    used = j < n_used
    nxt = (j + 2) % ring
    cuts = [round(k * tm / 3) for k in range(4)]

    def x_halves():
        return [v.astype(BF16) for v in _unpack_bf16_pairs(xbuf_ref[slot])]

    @pl.when(used)
    def _():
        gather_wait(slot)
        xl, xr = x_halves()
        gather_start(inv2_ref, nxt, cuts[0], cuts[1])
        half = xl.shape[1]
        gate_ref[...] = _dot(xl, wgb_ref[0:half, :]) + _dot(xr, wgb_ref[half:, :])

    @pl.when(used)
    def _():
        xl, xr = x_halves()
        gather_start(inv2_ref, nxt, cuts[1], cuts[2])
        half = xl.shape[1]
        up = _dot(xl, wub_ref[0:half, :]) + _dot(xr, wub_ref[half:, :])
        act_ref[...] = (_silu(gate_ref[...]) * up).astype(BF16)

    @pl.when(used)
    def _():
        gather_start(inv2_ref, nxt, cuts[2], cuts[3])
        y_ref[...] = _pack_bf16_pairs(_dot(act_ref[...], wdb_ref[...]))

    @pl.when(j >= n_used)
    def _():
        y_ref[...] = jnp.zeros_like(y_ref)

    @pl.when(j == n_used)
    def _():
        gather_wait(slot)
        gather_wait((j + 1) % ring)


def _experts(tile_expert, n_used, inv, h2, w_gate, w_up, w_down):
    _, d, hdim = w_gate.shape
    assert h2.shape[1] == d // 2 and h2.dtype == jnp.int32, "rows arrive as packed bf16 column pairs"
    tm = TM_EXPERT
    n_tiles = inv.shape[0] // tm
    inv3 = inv.reshape(n_tiles, 1, tm)
    last_used = lambda j, nu: jnp.minimum(j, nu[0] - 1)
    grid_spec = pltpu.PrefetchScalarGridSpec(
        num_scalar_prefetch=2,
        grid=(n_tiles + 1,),
        in_specs=[pl.BlockSpec((None, 1, tm), lambda j, te, nu: (jnp.minimum(j, n_tiles - 1), 0, 0),
                               memory_space=pltpu.SMEM),
                  pl.BlockSpec((None, 1, tm), lambda j, te, nu: (jnp.minimum(j + 1, n_tiles - 1), 0, 0),
                               memory_space=pltpu.SMEM),
                  pl.BlockSpec((None, 1, tm), lambda j, te, nu: (jnp.minimum(j + 2, n_tiles - 1), 0, 0),
                               memory_space=pltpu.SMEM),
                  pl.BlockSpec(memory_space=pl.ANY),
                  pl.BlockSpec((None, d, hdim), lambda j, te, nu: (te[last_used(j, nu)], 0, 0)),
                  pl.BlockSpec((None, d, hdim), lambda j, te, nu: (te[last_used(j, nu)], 0, 0)),
                  pl.BlockSpec((None, hdim, d), lambda j, te, nu: (te[last_used(j, nu)], 0, 0))],
        out_specs=pl.BlockSpec((tm, d // 2), lambda j, te, nu: (j, 0)),
        scratch_shapes=[pltpu.VMEM((GATHER_RING, tm, d // 2), jnp.int32),
                        pltpu.VMEM((d, hdim), BF16), pltpu.VMEM((d, hdim), BF16),
                        pltpu.VMEM((hdim, d), BF16),
                        pltpu.VMEM((tm, hdim), F32), pltpu.VMEM((tm, hdim), BF16),
                        pltpu.SemaphoreType.DMA((GATHER_RING,))],
    )
    return pl.pallas_call(
        _expert_kernel,
        grid_spec=grid_spec,
        out_shape=jax.ShapeDtypeStruct(((n_tiles + 1) * tm, d // 2), jnp.int32),
        compiler_params=_cparams(("arbitrary",)),
        name="experts",
    )(tile_expert, n_used, inv3, inv3, inv3, h2, w_gate, w_up, w_down)


def _combine_kernel(pos0_ref, pos1_ref, pos2_ref, y_ref, rt_ref, x1_ref, g2_ref, lng_ref, lnb_ref, o_ref,
                    buf_ref, sem):
    i = pl.program_id(0)
    tm = x1_ref.shape[0]
    ring = buf_ref.shape[0]
    slot = i % ring

    def gather_start(pos_ref, s):
        for r in range(tm):
            for k in range(2):
                cp = _row_copy(y_ref, pos_ref[0, 2 * r + k], buf_ref.at[s, k], r, sem.at[s])
                cp.start(priority=k % DMA_PRIORITIES)

    def gather_wait(s):
        for k in range(2):
            pltpu.make_async_copy(y_ref.at[pl.ds(0, tm)], buf_ref.at[s, k], sem.at[s]).wait()

    @pl.when(i == 0)
    def _():
        gather_start(pos0_ref, 0)
        gather_start(pos1_ref, 1)

    gather_wait(slot)
    l0, r0 = _unpack_bf16_pairs(buf_ref[slot, 0])
    l1, r1 = _unpack_bf16_pairs(buf_ref[slot, 1])
    gather_start(pos2_ref, (i + 2) % ring)
    rt = rt_ref[...]
    w0, w1 = rt[:, 4:5], rt[:, 5:6]
    y_moe = jnp.concatenate([w0 * l0 + w1 * l1, w0 * r0 + w1 * r1], axis=1)
    v = DEEPNORM_ALPHA * x1_ref[...] + g2_ref[...] * y_moe
    o_ref[...] = _layernorm(v) * lng_ref[...] + lnb_ref[...]

    @pl.when(i == pl.num_programs(0) - 1)
    def _():
        gather_wait((i + 1) % ring)
        gather_wait((i + 2) % ring)


def _combine(pos, y_sorted, rt, x1, g2, ln_g, ln_b, seq):
    t, d = x1.shape
    tm = TM_COMBINE
    tiles_per_seq = seq // tm
    nt = t // tm
    pos3 = pos.reshape(nt, 1, 2 * tm)
    return pl.pallas_call(
        _combine_kernel,
        grid=(nt,),
        in_specs=[pl.BlockSpec((None, 1, 2 * tm), lambda i: (i, 0, 0), memory_space=pltpu.SMEM),
                  pl.BlockSpec((None, 1, 2 * tm), lambda i: (jnp.minimum(i + 1, nt - 1), 0, 0),
                               memory_space=pltpu.SMEM),
                  pl.BlockSpec((None, 1, 2 * tm), lambda i: (jnp.minimum(i + 2, nt - 1), 0, 0),
                               memory_space=pltpu.SMEM),
                  pl.BlockSpec(memory_space=pl.ANY),
                  pl.BlockSpec((tm, LANES), lambda i: (i, 0)),
                  pl.BlockSpec((tm, d), lambda i: (i, 0)),
                  pl.BlockSpec((None, 1, d), lambda i: (i // tiles_per_seq, 0, 0)),
                  pl.BlockSpec((1, d), lambda i: (0, 0)),
                  pl.BlockSpec((1, d), lambda i: (0, 0))],
        out_specs=pl.BlockSpec((tm, d), lambda i: (i, 0)),
        out_shape=jax.ShapeDtypeStruct((t, d), F32),
        scratch_shapes=[pltpu.VMEM((GATHER_RING, 2, tm, d // 2), jnp.int32),
                        pltpu.SemaphoreType.DMA((GATHER_RING,))],
        compiler_params=_cparams(("arbitrary",)),
        name="combine_postln",
    )(pos3, pos3, pos3, y_sorted, rt, x1, g2, ln_g, ln_b)


def _layer(x2, c_pad, batch, seq, w_ada, b_ada, w_in, conv_w, conv_b, a_log_f, a_log_b, dt_bias_f,
           dt_bias_b, d_skip, ssd_norm_g, w_ssd_out, pool_w, pool_scale, w_pool_out, gate_b, w_o,
           ln1_g, ln1_b, router_wg, router_bg, router_we, router_be, exp_w_gate, exp_w_up,
           exp_w_down, ln2_g, ln2_b):
    t, d = x2.shape
    d_inner = SSD_HEADS * SSD_HEAD_DIM
    conv_dim = d_inner + 2 * SSD_GROUPS * SSD_STATE
    pool_width = pool_scale.shape[0]

    mod = _ada(c_pad, w_ada, b_ada[None, :])[:batch]
    sh1, sc1, g1, sh2, sc2, g2 = [m[:, None, :] for m in jnp.split(mod, 6, axis=-1)]

    o_xbc = d_inner
    o_dt = o_xbc + conv_dim
    o_pool = o_dt + 2 * SSD_HEADS
    w_in_b = w_in.astype(BF16)
    w_main = jnp.concatenate([w_in_b[:, :o_dt], w_in_b[:, o_pool:]], axis=1)
    w_dt = jnp.pad(w_in[:, o_dt:o_pool], ((0, 0), (0, LANES - 2 * SSD_HEADS)))
    pool_col = o_dt
    gate_col = o_dt + pool_width
    big, dt_raw = _inproj(x2, sh1, sc1, w_main, w_dt, seq)

    xc = _conv(big, o_xbc, conv_w, conv_b[None, :], seq)

    pad_row = lambda f, b: jnp.pad(jnp.concatenate([f, b]), (0, LANES - 2 * SSD_HEADS))[None, :]
    alog_row = pad_row(a_log_f, a_log_b)
    bias_row = pad_row(dt_bias_f, dt_bias_b)
    dskip_row = jnp.repeat(d_skip, SSD_HEAD_DIM)[None, :]
    y_bwd = _ssd(xc, dt_raw, alog_row, bias_row, batch, reverse=True)
    yn = _ssd(xc, dt_raw, alog_row, bias_row, batch, reverse=False,
              final_inputs=(big, y_bwd, dskip_row, ssd_norm_g[None, :]))

    w_router = jnp.pad(jnp.concatenate([router_wg, router_we], axis=1),
                       ((0, 0), (0, LANES - MOE_GROUPS - MOE_EXPERTS)))
    r_bias = jnp.pad(jnp.concatenate([router_bg, router_be]), (0, LANES - MOE_GROUPS - MOE_EXPERTS))[None, :]
    x1, h2, logits = _mix(yn, big, x2, w_ssd_out.astype(BF16), pool_w.astype(BF16),
                          w_pool_out.astype(BF16), w_o.astype(BF16), w_router, gate_b[None, :],
                          pool_scale[None, :], ln1_g[None, :], ln1_b[None, :], r_bias, g1, sh2, sc2,
                          seq, pool_col, gate_col)

    rt, counts = _route(logits)

    tme = TM_EXPERT
    cnt = counts[0, MOE_GROUPS:MOE_GROUPS + MOE_EXPERTS].astype(jnp.int32)
    padded = ((cnt + tme - 1) // tme) * tme
    ends = jnp.cumsum(padded)
    off = ends - padded
    eid = rt[:, 0:2].astype(jnp.int32)
    pos = rt[:, 2:4].astype(jnp.int32) + jnp.sum(
        jnp.where(eid[:, :, None] == jnp.arange(MOE_EXPERTS, dtype=jnp.int32), off, 0), axis=-1)
    p_rows = 2 * t + MOE_EXPERTS * tme
    n_tiles = p_rows // tme
    tile_ends = ends // tme
    tile_expert = jnp.minimum(
        jnp.sum(jnp.arange(n_tiles + 1, dtype=jnp.int32)[:, None] >= tile_ends[None, :], axis=1),
        MOE_EXPERTS - 1).astype(jnp.int32)
    n_used = tile_ends[-1:].astype(jnp.int32)

    inv = _slotmap(pos, jnp.zeros((p_rows,), jnp.int32))
    y_sorted = _experts(tile_expert, n_used, inv, h2, exp_w_gate, exp_w_up, exp_w_down)
    return _combine(pos, y_sorted, rt, x1, g2, ln2_g[None, :], ln2_b[None, :], seq)


def kernel(x, c, w_ada, b_ada, w_in, conv_w, conv_b, a_log_f, a_log_b, dt_bias_f, dt_bias_b, d_skip,
           ssd_norm_g, w_ssd_out, pool_w, pool_scale, w_pool_out, gate_b, w_o, ln1_g, ln1_b,
           router_wg, router_bg, router_we, router_be, exp_w_gate, exp_w_up, exp_w_down, ln2_g, ln2_b):
    batch, seq, d = x.shape
    x2 = x.reshape(batch * seq, d)
    c_pad = jnp.pad(c, ((0, 8 - batch), (0, 0)))
    params = (w_ada, b_ada, w_in, conv_w, conv_b, a_log_f, a_log_b, dt_bias_f, dt_bias_b, d_skip,
              ssd_norm_g, w_ssd_out, pool_w, pool_scale, w_pool_out, gate_b, w_o, ln1_g, ln1_b,
              router_wg, router_bg, router_we, router_be, exp_w_gate, exp_w_up, exp_w_down, ln2_g, ln2_b)
    for l in range(w_ada.shape[0]):
        x2 = _layer(x2, c_pad, batch, seq, *[p[l] for p in params])
    return x2.reshape(batch, seq, d)
```

```python
import functools

import jax
import jax.numpy as jnp
from jax import lax
from jax.experimental import pallas as pl
from jax.experimental.pallas import tpu as pltpu

F32 = jnp.float32
BF16 = jnp.bfloat16
HIGHEST = lax.Precision.HIGHEST

SSD_HEAD_DIM = 64
SSD_GROUPS = 8
SSD_HEADS_PER_GROUP = 4
SSD_HEADS = SSD_GROUPS * SSD_HEADS_PER_GROUP
SSD_STATE = 128
SSD_CONV = 5
SSD_CHUNK = 128
SSD_NORM_EPS = 1e-5
POOL_WINDOWS = (2, 4, 8, 16)
MOE_GROUPS = 4
MOE_EXPERTS_PER_GROUP = 8
MOE_EXPERTS = MOE_GROUPS * MOE_EXPERTS_PER_GROUP
DEPTH = 1
DEEPNORM_ALPHA = (2.0 * DEPTH) ** 0.25
LN_EPS = 1e-5

LANES = 128
HALO = 16
VMEM_LIMIT = 48 * 1024 * 1024
SLOTMAP_UNROLL = 8
DMA_PRIORITIES = 2
GATHER_RING = 3

TM_INPROJ = 1024
TN_INPROJ = 3072
TM_CONV = 1024
CONV_SUB = 128
SSD_STEP_CHUNKS = 4
TM_MIX = 256
POOL_SUB = 128
TM_ROUTE = 512
TM_SLOTMAP = 1024
TM_EXPERT = 256
TM_COMBINE = 512


def _dot(a, b):
    return jnp.dot(a, b, preferred_element_type=F32)


def _split_hi_lo(v):
    hi = v.astype(BF16)
    lo = (v - hi.astype(F32)).astype(BF16)
    return hi, lo


def _split3(v):
    hi = v.astype(BF16)
    r = v - hi.astype(F32)
    mid = r.astype(BF16)
    lo = (r - mid.astype(F32)).astype(BF16)
    return hi, mid, lo


def _dot3(a, b):
    a_hi, a_lo = _split_hi_lo(a)
    b_hi, b_lo = _split_hi_lo(b)
    return _dot(a_hi, b_hi) + _dot(a_lo, b_hi) + _dot(a_hi, b_lo)


def _layernorm(v):
    mu = jnp.mean(v, axis=-1, keepdims=True)
    vc = v - mu
    var = jnp.mean(vc * vc, axis=-1, keepdims=True)
    return vc * lax.rsqrt(var + LN_EPS)


def _silu(v):
    return v * jax.nn.sigmoid(v)


def _pack_bf16_pairs(v):
    m = v.shape[1] // 2
    hi = lax.bitcast_convert_type(v[:, :m].astype(BF16).astype(F32), jnp.int32)
    lo = lax.bitcast_convert_type(v[:, m:].astype(BF16).astype(F32), jnp.int32)
    return hi | lax.shift_right_logical(lo, 16)


def _unpack_bf16_pairs(p):
    hi = lax.bitcast_convert_type(p & jnp.int32(-65536), F32)
    lo = lax.bitcast_convert_type(lax.shift_left(p, 16), F32)
    return hi, lo


def _cparams(sem):
    return pltpu.CompilerParams(dimension_semantics=sem, vmem_limit_bytes=VMEM_LIMIT)


def _ada_kernel(c_ref, w_ref, b_ref, o_ref):
    o_ref[...] = jnp.dot(_silu(c_ref[...]), w_ref[...], precision=HIGHEST,
                         preferred_element_type=F32) + b_ref[...]


def _ada(c_pad, w, b):
    d, n = w.shape
    tn = 1024
    return pl.pallas_call(
        _ada_kernel,
        grid=(n // tn,),
        in_specs=[pl.BlockSpec((c_pad.shape[0], d), lambda j: (0, 0)),
                  pl.BlockSpec((d, tn), lambda j: (0, j)),
                  pl.BlockSpec((1, tn), lambda j: (0, j))],
        out_specs=pl.BlockSpec((c_pad.shape[0], tn), lambda j: (0, j)),
        out_shape=jax.ShapeDtypeStruct((c_pad.shape[0], n), F32),
        compiler_params=_cparams(("arbitrary",)),
        name="ada_mod",
    )(c_pad, w, b)


def _inproj_kernel(x_ref, sh_ref, sc_ref, w_ref, wdt_ref, o_ref, dt_ref, h_ref):
    @pl.when(pl.program_id(1) == 0)
    def _():
        h = _layernorm(x_ref[...]) * (1.0 + sc_ref[...]) + sh_ref[...]
        h_ref[...] = h.astype(BF16)
        dt_ref[...] = _dot3(h, wdt_ref[...])

    o_ref[...] = _dot(h_ref[...], w_ref[...]).astype(BF16)


def _inproj(x2, sh, sc, w_main, w_dt, seq):
    t, d = x2.shape
    n = w_main.shape[1]
    tm, tn = TM_INPROJ, TN_INPROJ
    tiles_per_seq = seq // tm
    return pl.pallas_call(
        _inproj_kernel,
        grid=(t // tm, n // tn),
        in_specs=[pl.BlockSpec((tm, d), lambda i, j: (i, 0)),
                  pl.BlockSpec((None, 1, d), lambda i, j: (i // tiles_per_seq, 0, 0)),
                  pl.BlockSpec((None, 1, d), lambda i, j: (i // tiles_per_seq, 0, 0)),
                  pl.BlockSpec((d, tn), lambda i, j: (0, j)),
                  pl.BlockSpec((d, LANES), lambda i, j: (0, 0))],
        out_specs=[pl.BlockSpec((tm, tn), lambda i, j: (i, j)),
                   pl.BlockSpec((tm, LANES), lambda i, j: (i, 0))],
        out_shape=[jax.ShapeDtypeStruct((t, n), BF16),
                   jax.ShapeDtypeStruct((t, LANES), F32)],
        scratch_shapes=[pltpu.VMEM((tm, d), BF16)],
        compiler_params=_cparams(("arbitrary", "arbitrary")),
        name="ln_inproj",
    )(x2, sh, sc, w_main, w_dt)


def _conv_kernel(tiles_per_seq, cur_ref, prev_ref, next_ref, w_ref, b_ref, o_ref, ext_ref):
    i = pl.program_id(0)
    tm = cur_ref.shape[0]
    first = (i % tiles_per_seq) == 0
    last = (i % tiles_per_seq) == tiles_per_seq - 1
    ext_ref[0:HALO, :] = jnp.where(first, jnp.zeros_like(prev_ref), prev_ref[...])
    ext_ref[HALO:HALO + tm, :] = cur_ref[...]
    ext_ref[HALO + tm:, :] = jnp.where(last, jnp.zeros_like(next_ref), next_ref[...])
    pad = SSD_CONV // 2
    sub = CONV_SUB
    win = sub + 2 * HALO
    ri = lax.broadcasted_iota(jnp.int32, (sub, win), 0)
    ci = lax.broadcasted_iota(jnp.int32, (sub, win), 1)
    picks = [jnp.where(ci == ri + HALO + k - pad, 1.0, 0.0).astype(BF16) for k in range(SSD_CONV)]
    for m in range(tm // sub):
        window = ext_ref[m * sub:m * sub + win, :]
        acc = b_ref[...] + w_ref[pad:pad + 1, :] * ext_ref[HALO + m * sub:HALO + (m + 1) * sub, :].astype(F32)
        for k in range(SSD_CONV):
            if k != pad:
                acc = acc + w_ref[k:k + 1, :] * _dot(picks[k], window)
        o_ref[m * sub:(m + 1) * sub, :] = _silu(acc).astype(BF16)


def _conv(big, col_off, conv_w, conv_b, seq):
    t = big.shape[0]
    cdim = conv_w.shape[1]
    tm, tc = TM_CONV, 1024
    tiles_per_seq = seq // tm
    cb0 = col_off // tc
    hb = tm // HALO
    nhb = t // HALO
    return pl.pallas_call(
        functools.partial(_conv_kernel, tiles_per_seq),
        grid=(t // tm, cdim // tc),
        in_specs=[pl.BlockSpec((tm, tc), lambda i, j: (i, cb0 + j)),
                  pl.BlockSpec((HALO, tc), lambda i, j: (jnp.maximum(i * hb - 1, 0), cb0 + j)),
                  pl.BlockSpec((HALO, tc), lambda i, j: (jnp.minimum((i + 1) * hb, nhb - 1), cb0 + j)),
                  pl.BlockSpec((SSD_CONV, tc), lambda i, j: (0, j)),
                  pl.BlockSpec((1, tc), lambda i, j: (0, j))],
        out_specs=pl.BlockSpec((tm, tc), lambda i, j: (i, j)),
        out_shape=jax.ShapeDtypeStruct((t, cdim), BF16),
        scratch_shapes=[pltpu.VMEM((tm + 2 * HALO, tc), BF16)],
        compiler_params=_cparams(("arbitrary", "arbitrary")),
        name="conv_silu",
    )(big, big, big, conv_w, conv_b)


def _tri_cumsum(tri, v):
    hi, mid, lo = _split3(v)
    return _dot(tri, hi) + _dot(tri, mid) + _dot(tri, lo)


def _expand_heads(v, e_ref):
    hi, lo = _split_hi_lo(v)
    return _dot(jnp.concatenate([hi, lo], axis=1), e_ref[...])


def _state_step(g, x_g, b_ref, c_ref, st_ref, w_exp, dec_exp, end_row):
    gw = x_g.shape[1]
    cols = slice(g * gw, (g + 1) * gw)
    bg = b_ref[:, g * SSD_STATE:(g + 1) * SSD_STATE]
    cg = c_ref[:, g * SSD_STATE:(g + 1) * SSD_STATE]
    h_in = st_ref[g]
    y_off = _dot(cg, h_in.astype(BF16)) * dec_exp[:, cols]
    xw = (x_g * w_exp[:, cols]).astype(BF16)
    s_new = lax.dot_general(bg, xw, (((0,), (0,)), ((), ())), preferred_element_type=F32)
    st_ref[g] = h_in * dec_exp[end_row:end_row + 1, cols] + s_new
    return y_off


def _chunk_views(refs, ch):
    return [r.at[pl.ds(ch * SSD_CHUNK, SSD_CHUNK)] for r in refs]


def _ssd_bwd_kernel(x_ref, b_ref, c_ref, dt_ref, alog_ref, bias_ref, e64_ref, o_ref, st_ref):
    @pl.when(pl.program_id(1) == 0)
    def _():
        st_ref[...] = jnp.zeros_like(st_ref)

    for ch in reversed(range(x_ref.shape[0] // SSD_CHUNK)):
        x_v, b_v, c_v, dt_v, o_v = _chunk_views((x_ref, b_ref, c_ref, dt_ref, o_ref), ch)
        _ssd_bwd_chunk(x_v, b_v, c_v, dt_v, alog_ref, bias_ref, e64_ref, o_v, st_ref)


def _ssd_bwd_chunk(x_ref, b_ref, c_ref, dt_ref, alog_ref, bias_ref, e64_ref, o_ref, st_ref):
    q = x_ref.shape[0]
    gw = SSD_HEADS_PER_GROUP * SSD_HEAD_DIM
    dt = jax.nn.softplus(dt_ref[...] + bias_ref[...])
    da = dt * (-jnp.exp(alog_ref[...]))
    li = lax.broadcasted_iota(jnp.int32, (q, q), 0)
    si = lax.broadcasted_iota(jnp.int32, (q, q), 1)
    cs = _tri_cumsum(jnp.where(si >= li, 1.0, 0.0).astype(BF16), da)
    w_exp = _expand_heads(dt * jnp.exp(cs[0:1, :] - cs), e64_ref)
    dec_exp = _expand_heads(jnp.exp(cs), e64_ref)
    for g in range(SSD_GROUPS):
        cols = slice(g * gw, (g + 1) * gw)
        x_g = x_ref[:, cols].astype(F32)
        o_ref[:, cols] = _state_step(g, x_g, b_ref, c_ref, st_ref, w_exp, dec_exp, 0).astype(BF16)


def _ssd_fwd_kernel(x_ref, b_ref, c_ref, dt_ref, alog_ref, bias_ref, e64_ref, ecs_ref, z_ref,
                    yb_ref, dskip_ref, ng_ref, o_ref, st_ref):
    @pl.when(pl.program_id(1) == 0)
    def _():
        st_ref[...] = jnp.zeros_like(st_ref)

    for ch in range(x_ref.shape[0] // SSD_CHUNK):
        x_v, b_v, c_v, dt_v, z_v, yb_v, o_v = _chunk_views(
            (x_ref, b_ref, c_ref, dt_ref, z_ref, yb_ref, o_ref), ch)
        _ssd_fwd_chunk(x_v, b_v, c_v, dt_v, alog_ref, bias_ref, e64_ref, ecs_ref, z_v, yb_v, dskip_ref,
                       ng_ref, o_v, st_ref)


def _ssd_fwd_chunk(x_ref, b_ref, c_ref, dt_ref, alog_ref, bias_ref, e64_ref, ecs_ref, z_ref,
                   yb_ref, dskip_ref, ng_ref, o_ref, st_ref):
    q = x_ref.shape[0]
    p = SSD_HEAD_DIM
    hpg = SSD_HEADS_PER_GROUP
    gw = hpg * p
    dt = jax.nn.softplus(dt_ref[...] + bias_ref[...])
    da = dt * (-jnp.exp(alog_ref[...]))
    li = lax.broadcasted_iota(jnp.int32, (q, q), 0)
    si = lax.broadcasted_iota(jnp.int32, (q, q), 1)
    before = si <= li
    after = si >= li
    lane = lax.broadcasted_iota(jnp.int32, (q, LANES), 1)
    tri2 = jnp.concatenate([jnp.where(before, 1.0, 0.0), jnp.where(after, 1.0, 0.0)], axis=1).astype(BF16)
    da2 = jnp.concatenate([jnp.where(lane < SSD_HEADS, da, 0.0),
                           jnp.where((lane >= SSD_HEADS) & (lane < 2 * SSD_HEADS), da, 0.0)], axis=0)
    cs = _tri_cumsum(tri2, da2)
    csp_t = (cs - jnp.log(dt)).T
    hi, mid, lo = _split3(cs)
    nhd = 2 * SSD_HEADS
    v_cs = jnp.concatenate([hi[:, :nhd].astype(F32), mid[:, :nhd].astype(F32), lo[:, :nhd].astype(F32),
                            jnp.zeros((q, nhd), F32)], axis=1).astype(BF16)
    to_end = jnp.where(lane < SSD_HEADS, cs[q - 1:q, :] - cs, 0.0)
    w_exp = _expand_heads(dt * jnp.exp(to_end), e64_ref)
    dec_exp = _expand_heads(jnp.exp(cs), e64_ref)
    half =lax.broadcasted_iota(jnp.int32, (q, 2 * p), 1) < p

    for g in range(SSD_GROUPS):
        cols = slice(g * gw, (g + 1) * gw)
        bg = b_ref[:, g * SSD_STATE:(g + 1) * SSD_STATE]
        cg = c_ref[:, g * SSD_STATE:(g + 1) * SSD_STATE]
        cb = lax.dot_general(cg, bg, (((1,), (1,)), ((), ())), preferred_element_type=F32)
        colb = _dot(v_cs, ecs_ref[:, g * 2 * hpg * q:(g + 1) * 2 * hpg * q])
        x_g = x_ref[:, cols].astype(F32)
        ms = []
        for r in range(hpg):
            hh = g * hpg + r
            seg_f = colb[:, r * q:(r + 1) * q] - csp_t[hh:hh + 1, :]
            seg_b = colb[:, (hpg + r) * q:(hpg + r + 1) * q] - csp_t[SSD_HEADS + hh:SSD_HEADS + hh + 1, :]
            decay = jnp.exp(jnp.where(before, seg_f, -jnp.inf)) + jnp.exp(jnp.where(after, seg_b, -jnp.inf))
            ms.append((cb * decay).astype(BF16))
        ys = []
        for pr in range(hpg // 2):
            xp = x_g[:, 2 * pr * p:2 * (pr + 1) * p]
            rhs = jnp.concatenate([jnp.where(half, xp, 0.0), jnp.where(half, 0.0, xp)], axis=0).astype(BF16)
            ys.append(_dot(jnp.concatenate([ms[2 * pr], ms[2 * pr + 1]], axis=1), rhs))
        y_g = jnp.concatenate(ys, axis=1)
        y_g = y_g + _state_step(g, x_g, b_ref, c_ref, st_ref, w_exp, dec_exp, q - 1)
        y_g = y_g + yb_ref[:, cols].astype(F32) + x_g * dskip_ref[:, cols]
        yg = y_g * _silu(z_ref[:, cols].astype(F32))
        ms_g = jnp.mean(yg * yg, axis=-1, keepdims=True)
        o_ref[:, cols] = (yg * lax.rsqrt(ms_g + SSD_NORM_EPS) * ng_ref[:, cols]).astype(BF16)


def _head_expansion(col0):
    j = jnp.arange(2 * LANES)[:, None]
    c = jnp.arange(SSD_HEADS * SSD_HEAD_DIM)[None, :]
    return ((j % LANES - col0) == c // SSD_HEAD_DIM).astype(BF16)


def _score_expansion():
    nhd = 2 * SSD_HEADS
    row = jnp.arange(4 * nhd)[:, None]
    col = jnp.arange(nhd * SSD_CHUNK)[None, :]
    j = row % nhd
    head = j % SSD_HEADS
    blk = ((head // SSD_HEADS_PER_GROUP) * 2 + j // SSD_HEADS) * SSD_HEADS_PER_GROUP + head % SSD_HEADS_PER_GROUP
    return ((row < 3 * nhd) & (blk == col // SSD_CHUNK)).astype(BF16)


def _ssd(xc, dt_raw, alog_row, bias_row, batch, reverse, final_inputs=None):
    t = xc.shape[0]
    q = SSD_STEP_CHUNKS * SSD_CHUNK
    nc = t // batch // q
    d_inner = SSD_HEADS * SSD_HEAD_DIM
    bcw = SSD_GROUPS * SSD_STATE

    def row(b, c):
        return b * nc + ((nc - 1 - c) if reverse else c)

    full = lambda a: pl.BlockSpec(a.shape, lambda b, c: (0,) * a.ndim)
    e64 = _head_expansion(SSD_HEADS if reverse else 0)
    in_specs = [pl.BlockSpec((q, d_inner), lambda b, c: (row(b, c), 0)),
                pl.BlockSpec((q, bcw), lambda b, c: (row(b, c), d_inner // bcw)),
                pl.BlockSpec((q, bcw), lambda b, c: (row(b, c), d_inner // bcw + 1)),
                pl.BlockSpec((q, LANES), lambda b, c: (row(b, c), 0)),
                full(alog_row), full(bias_row), full(e64)]
    args = [xc, xc, xc, dt_raw, alog_row, bias_row, e64]
    if reverse:
        body = _ssd_bwd_kernel
    else:
        body = _ssd_fwd_kernel
        big, y_bwd, dskip_row, ng_row = final_inputs
        ecs = _score_expansion()
        in_specs += [full(ecs),
                     pl.BlockSpec((q, d_inner), lambda b, c: (row(b, c), 0)),
                     pl.BlockSpec((q, d_inner), lambda b, c: (row(b, c), 0)),
                     full(dskip_row), full(ng_row)]
        args += [ecs, big, y_bwd, dskip_row, ng_row]
    return pl.pallas_call(
        body,
        grid=(batch, nc),
        in_specs=in_specs,
        out_specs=pl.BlockSpec((q, d_inner), lambda b, c: (row(b, c), 0)),
        out_shape=jax.ShapeDtypeStruct((t, d_inner), BF16),
        scratch_shapes=[pltpu.VMEM((SSD_GROUPS, SSD_STATE, SSD_HEADS_PER_GROUP * SSD_HEAD_DIM), F32)],
        compiler_params=_cparams(("arbitrary", "arbitrary")),
        name="ssd_bwd" if reverse else "ssd_fwd",
    )(*args)


def _mix_kernel(tiles_per_seq, seq, yn_ref, u_ref, up_ref, un_ref, ga_ref, gb_ref, x_ref,
                wssd_ref, pw_ref, wpo_ref, wo_ref, wr_ref,
                gateb_ref, pscale_ref, ln1g_ref, ln1b_ref, rb_ref, g1_ref, sh2_ref, sc2_ref,
                x1_ref, h2_ref, lg_ref, ext_ref):
    i = pl.program_id(0)
    tm, width = u_ref.shape
    first = (i % tiles_per_seq) == 0
    last = (i % tiles_per_seq) == tiles_per_seq - 1
    u = u_ref[...].astype(F32)
    ext_ref[0:HALO, :] = jnp.where(first, jnp.zeros_like(up_ref), up_ref[...])
    ext_ref[HALO:HALO + tm, :] = u_ref[...]
    ext_ref[HALO + tm:, :] = jnp.where(last, jnp.zeros_like(un_ref), un_ref[...])
    tpos = (i % tiles_per_seq) * tm + lax.broadcasted_iota(jnp.int32, (tm, 1), 0)
    gd = width // len(POOL_WINDOWS)
    sub = POOL_SUB
    win = sub + 2 * HALO
    ri = lax.broadcasted_iota(jnp.int32, (sub, win), 0) + HALO
    ci = lax.broadcasted_iota(jnp.int32, (sub, win), 1)
    mixed = []
    for gi, w in enumerate(POOL_WINDOWS):
        cols = slice(gi * gd, (gi + 1) * gd)
        band = jnp.where((ci >= ri - w // 2) & (ci < ri + w // 2), 1.0, 0.0).astype(BF16)
        s = jnp.concatenate([_dot(band, ext_ref[m * sub:m * sub + win, cols]) for m in range(tm // sub)], axis=0)
        cnt = (jnp.minimum(tpos + w // 2, seq) - jnp.maximum(tpos - w // 2, 0)).astype(F32)
        diff = s / cnt - u[:, cols]
        mixed.append(_dot(diff.astype(BF16), pw_ref[gi]))
    mixed = jnp.concatenate(mixed, axis=1) * pscale_ref[...]
    y_pool = _dot(mixed.astype(BF16), wpo_ref[...])
    y_ssd = _dot(yn_ref[...], wssd_ref[...])
    d = y_ssd.shape[1]
    g_ssd = jax.nn.sigmoid(ga_ref[...].astype(F32) + gateb_ref[:, 0:d])
    g_pool = jax.nn.sigmoid(gb_ref[...].astype(F32) + gateb_ref[:, d:2 * d])
    mix = _dot((g_ssd * y_ssd + g_pool * y_pool).astype(BF16), wo_ref[...])
    x1 = _layernorm(DEEPNORM_ALPHA * x_ref[...] + g1_ref[...] * mix) * ln1g_ref[...] + ln1b_ref[...]
    x1_ref[...] = x1
    h2 = _layernorm(x1) * (1.0 + sc2_ref[...]) + sh2_ref[...]
    h2_ref[...] = _pack_bf16_pairs(h2)
    lg_ref[...] = _dot3(h2, wr_ref[...]) + rb_ref[...]


def _mix(yn, big, x2, w_ssd_out, pool_w, w_pool_out, w_o, w_router, gate_b, pool_scale, ln1_g, ln1_b,
         r_bias, g1, sh2, sc2, seq, pool_col, gate_col):
    t, d = x2.shape
    tm = TM_MIX
    tiles_per_seq = seq // tm
    hb = tm // HALO
    nhb = t // HALO
    pcb = pool_col // d
    gcb = gate_col // d
    full = lambda a: pl.BlockSpec(a.shape, lambda i: (0,) * a.ndim)
    per_batch = pl.BlockSpec((None, 1, d), lambda i: (i // tiles_per_seq, 0, 0))
    return pl.pallas_call(
        functools.partial(_mix_kernel, tiles_per_seq, seq),
        grid=(t // tm,),
        in_specs=[pl.BlockSpec((tm, yn.shape[1]), lambda i: (i, 0)),
                  pl.BlockSpec((tm, d), lambda i: (i, pcb)),
                  pl.BlockSpec((HALO, d), lambda i: (jnp.maximum(i * hb - 1, 0), pcb)),
                  pl.BlockSpec((HALO, d), lambda i: (jnp.minimum((i + 1) * hb, nhb - 1), pcb)),
                  pl.BlockSpec((tm, d), lambda i: (i, gcb)),
                  pl.BlockSpec((tm, d), lambda i: (i, gcb + 1)),
                  pl.BlockSpec((tm, d), lambda i: (i, 0)),
                  full(w_ssd_out), full(pool_w), full(w_pool_out), full(w_o), full(w_router),
                  full(gate_b), full(pool_scale), full(ln1_g), full(ln1_b), full(r_bias),
                  per_batch, per_batch, per_batch],
        out_specs=[pl.BlockSpec((tm, d), lambda i: (i, 0)),
                   pl.BlockSpec((tm, d // 2), lambda i: (i, 0)),
                   pl.BlockSpec((tm, LANES), lambda i: (i, 0))],
        out_shape=[jax.ShapeDtypeStruct((t, d), F32),
                   jax.ShapeDtypeStruct((t, d // 2), jnp.int32),
                   jax.ShapeDtypeStruct((t, LANES), F32)],
        scratch_shapes=[pltpu.VMEM((tm + 2 * HALO, d), BF16)],
        compiler_params=_cparams(("arbitrary",)),
        name="mix_postln",
    )(yn, big, big, big, big, big, x2, w_ssd_out, pool_w, w_pool_out, w_o, w_router,
      gate_b, pool_scale, ln1_g, ln1_b, r_bias, g1, sh2, sc2)


def _route_kernel(lg_ref, rt_ref, cnt_ref, carry_ref):
    @pl.when(pl.program_id(0) == 0)
    def _():
        carry_ref[...] = jnp.zeros_like(carry_ref)

    lg = lg_ref[...]
    tm = lg.shape[0]
    lane = lax.broadcasted_iota(jnp.int32, lg.shape, 1).astype(F32)
    neg = -jnp.inf
    big_lane = float(LANES)
    gl = jnp.where(lane < MOE_GROUPS, lg, neg)
    gmax = jnp.max(gl, axis=-1, keepdims=True)
    g_w = 1.0 / jnp.sum(jnp.exp(gl - gmax), axis=-1, keepdims=True)
    g_idx = jnp.min(jnp.where(gl == gmax, lane, big_lane), axis=-1, keepdims=True)
    lo = MOE_GROUPS + MOE_EXPERTS_PER_GROUP * g_idx
    el = jnp.where((lane >= lo) & (lane < lo + MOE_EXPERTS_PER_GROUP), lg, neg)
    m1 = jnp.max(el, axis=-1, keepdims=True)
    i1 = jnp.min(jnp.where(el == m1, lane, big_lane), axis=-1, keepdims=True)
    el2 = jnp.where(lane == i1, neg, el)
    m2 = jnp.max(el2, axis=-1, keepdims=True)
    i2 = jnp.min(jnp.where(el2 == m2, lane, big_lane), axis=-1, keepdims=True)
    e = jnp.exp(m2 - m1)
    w1 = g_w / (1.0 + e)
    w2 = g_w * e / (1.0 + e)
    onehot = jnp.where((lane == i1) | (lane == i2), 1.0, 0.0)
    ri = lax.broadcasted_iota(jnp.int32, (tm, tm), 0)
    ci = lax.broadcasted_iota(jnp.int32, (tm, tm), 1)
    earlier = jnp.where(ci < ri, 1.0, 0.0).astype(BF16)
    rank = _dot(earlier, onehot.astype(BF16)) + carry_ref[...]
    r1 = jnp.sum(jnp.where(lane == i1, rank, 0.0), axis=-1, keepdims=True)
    r2 = jnp.sum(jnp.where(lane == i2, rank, 0.0), axis=-1, keepdims=True)
    carry_ref[...] = carry_ref[...] + jnp.sum(onehot, axis=0, keepdims=True)
    cnt_ref[...] = carry_ref[...]
    out = jnp.where(lane == 0, i1 - MOE_GROUPS, 0.0)
    out = jnp.where(lane == 1, i2 - MOE_GROUPS, out)
    out = jnp.where(lane == 2, r1, out)
    out = jnp.where(lane == 3, r2, out)
    out = jnp.where(lane == 4, w1, out)
    out = jnp.where(lane == 5, w2, out)
    rt_ref[...] = out


def _route(logits):
    t = logits.shape[0]
    tm = TM_ROUTE
    return pl.pallas_call(
        _route_kernel,
        grid=(t // tm,),
        in_specs=[pl.BlockSpec((tm, LANES), lambda i: (i, 0))],
        out_specs=[pl.BlockSpec((tm, LANES), lambda i: (i, 0)),
                   pl.BlockSpec((1, LANES), lambda i: (0, 0))],
        out_shape=[jax.ShapeDtypeStruct((t, LANES), F32),
                   jax.ShapeDtypeStruct((1, LANES), F32)],
        scratch_shapes=[pltpu.VMEM((1, LANES), F32)],
        compiler_params=_cparams(("arbitrary",)),
        name="route",
    )(logits)


def _slotmap_kernel(pos_ref, init_ref, inv_ref, sem):
    i = pl.program_id(0)
    tm = pos_ref.shape[1] // 2

    @pl.when(i == 0)
    def _():
        cp = pltpu.make_async_copy(init_ref, inv_ref, sem)
        cp.start()
        cp.wait()

    def body(r, carry):
        tok = i * tm + r
        inv_ref[pos_ref[0, 2 * r]] = tok
        inv_ref[pos_ref[0, 2 * r + 1]] = tok
        return carry

    lax.fori_loop(0, tm, body, 0, unroll=SLOTMAP_UNROLL)


def _slotmap(pos, init):
    t = pos.shape[0]
    tm = TM_SLOTMAP
    pos3 = pos.reshape(t // tm, 1, 2 * tm)
    return pl.pallas_call(
        _slotmap_kernel,
        grid=(t // tm,),
        in_specs=[pl.BlockSpec((None, 1, 2 * tm), lambda i: (i, 0, 0), memory_space=pltpu.SMEM),
                  pl.BlockSpec(memory_space=pl.ANY)],
        out_specs=pl.BlockSpec(memory_space=pltpu.SMEM),
        out_shape=jax.ShapeDtypeStruct(init.shape, jnp.int32),
        scratch_shapes=[pltpu.SemaphoreType.DMA(())],
        compiler_params=_cparams(("arbitrary",)),
        name="slotmap",
    )(pos3, init)


def _row_copy(src_ref, src_row, dst_ref, dst_row, sem):
    return pltpu.make_async_copy(src_ref.at[pl.ds(src_row, 1)], dst_ref.at[pl.ds(dst_row, 1)], sem)


def _expert_kernel(te_ref, nu_ref, inv0_ref, inv1_ref, inv2_ref, h2_ref, wg_ref, wu_ref, wd_ref,
                   y_ref, xbuf_ref, wgb_ref, wub_ref, wdb_ref, gsem):
    j = pl.program_id(0)
    n_used = nu_ref[0]
    tm = y_ref.shape[0]
    ring = xbuf_ref.shape[0]
    slot = j % ring

    def gather_start(inv_ref, s):
        for r in range(tm):
            cp = _row_copy(h2_ref, inv_ref[0, r], xbuf_ref.at[s], r, gsem.at[s])
            cp.start(priority=r % DMA_PRIORITIES)

    def gather_wait(s):
        pltpu.make_async_copy(h2_ref.at[pl.ds(0, tm)], xbuf_ref.at[s], gsem.at[s]).wait()

    @pl.when(j == 0)
    def _():
        gather_start(inv0_ref, 0)
        gather_start(inv1_ref, 1)

    new_expert = jnp.logical_or(j == 0, te_ref[j] != te_ref[jnp.maximum(j - 1, 0)])

    @pl.when(jnp.logical_and(j < n_used, new_expert))
    def _():
        wgb_ref[...] = wg_ref[...].astype(BF16)
        wub_ref[...] = wu_ref[...].astype(BF16)
        wdb_ref[...] = wd_ref[...].astype(BF16)

    @pl.when(j < n_used)
    def _():
        gather_wait(slot)
        xl, xr = [v.astype(BF16) for v in _unpack_bf16_pairs(xbuf_ref[slot])]
        gather_start(inv2_ref, (j + 2) % ring)
        half = xl.shape[1]
        gate = _dot(xl, wgb_ref[0:half, :]) + _dot(xr, wgb_ref[half:, :])
        up = _dot(xl, wub_ref[0:half, :]) + _dot(xr, wub_ref[half:, :])
        y = _dot((_silu(gate) * up).astype(BF16), wdb_ref[...])
        y_ref[...] = _pack_bf16_pairs(y)

    @pl.when(j >= n_used)
    def _():
        y_ref[...] = jnp.zeros_like(y_ref)

    @pl.when(j == n_used)
    def _():
        gather_wait(slot)
        gather_wait((j + 1) % ring)


def _experts(tile_expert, n_used, inv, h2, w_gate, w_up, w_down):
    _, d, hdim = w_gate.shape
    assert h2.shape[1] == d // 2 and h2.dtype == jnp.int32, "rows arrive as packed bf16 column pairs"
    tm = TM_EXPERT
    n_tiles = inv.shape[0] // tm
    inv3 = inv.reshape(n_tiles, 1, tm)
    last_used = lambda j, nu: jnp.minimum(j, nu[0] - 1)
    grid_spec = pltpu.PrefetchScalarGridSpec(
        num_scalar_prefetch=2,
        grid=(n_tiles + 1,),
        in_specs=[pl.BlockSpec((None, 1, tm), lambda j, te, nu: (jnp.minimum(j, n_tiles - 1), 0, 0),
                               memory_space=pltpu.SMEM),
                  pl.BlockSpec((None, 1, tm), lambda j, te, nu: (jnp.minimum(j + 1, n_tiles - 1), 0, 0),
                               memory_space=pltpu.SMEM),
                  pl.BlockSpec((None, 1, tm), lambda j, te, nu: (jnp.minimum(j + 2, n_tiles - 1), 0, 0),
                               memory_space=pltpu.SMEM),
                  pl.BlockSpec(memory_space=pl.ANY),
                  pl.BlockSpec((None, d, hdim), lambda j, te, nu: (te[last_used(j, nu)], 0, 0)),
                  pl.BlockSpec((None, d, hdim), lambda j, te, nu: (te[last_used(j, nu)], 0, 0)),
                  pl.BlockSpec((None, hdim, d), lambda j, te, nu: (te[last_used(j, nu)], 0, 0))],
        out_specs=pl.BlockSpec((tm, d // 2), lambda j, te, nu: (j, 0)),
        scratch_shapes=[pltpu.VMEM((GATHER_RING, tm, d // 2), jnp.int32),
                        pltpu.VMEM((d, hdim), BF16), pltpu.VMEM((d, hdim), BF16),
                        pltpu.VMEM((hdim, d), BF16), pltpu.SemaphoreType.DMA((GATHER_RING,))],
    )
    return pl.pallas_call(
        _expert_kernel,
        grid_spec=grid_spec,
        out_shape=jax.ShapeDtypeStruct(((n_tiles + 1) * tm, d // 2), jnp.int32),
        compiler_params=_cparams(("arbitrary",)),
        name="experts",
    )(tile_expert, n_used, inv3, inv3, inv3, h2, w_gate, w_up, w_down)


def _combine_kernel(pos0_ref, pos1_ref, pos2_ref, y_ref, rt_ref, x1_ref, g2_ref, lng_ref, lnb_ref, o_ref,
                    buf_ref, sem):
    i = pl.program_id(0)
    tm = x1_ref.shape[0]
    ring = buf_ref.shape[0]
    slot = i % ring

    def gather_start(pos_ref, s):
        for r in range(tm):
            for k in range(2):
                cp = _row_copy(y_ref, pos_ref[0, 2 * r + k], buf_ref.at[s, k], r, sem.at[s])
                cp.start(priority=k % DMA_PRIORITIES)

    def gather_wait(s):
        for k in range(2):
            pltpu.make_async_copy(y_ref.at[pl.ds(0, tm)], buf_ref.at[s, k], sem.at[s]).wait()

    @pl.when(i == 0)
    def _():
        gather_start(pos0_ref, 0)
        gather_start(pos1_ref, 1)

    gather_wait(slot)
    l0, r0 = _unpack_bf16_pairs(buf_ref[slot, 0])
    l1, r1 = _unpack_bf16_pairs(buf_ref[slot, 1])
    gather_start(pos2_ref, (i + 2) % ring)
    rt = rt_ref[...]
    w0, w1 = rt[:, 4:5], rt[:, 5:6]
    y_moe = jnp.concatenate([w0 * l0 + w1 * l1, w0 * r0 + w1 * r1], axis=1)
    v = DEEPNORM_ALPHA * x1_ref[...] + g2_ref[...] * y_moe
    o_ref[...] = _layernorm(v) * lng_ref[...] + lnb_ref[...]

    @pl.when(i == pl.num_programs(0) - 1)
    def _():
        gather_wait((i + 1) % ring)
        gather_wait((i + 2) % ring)


def _combine(pos, y_sorted, rt, x1, g2, ln_g, ln_b, seq):
    t, d = x1.shape
    tm = TM_COMBINE
    tiles_per_seq = seq // tm
    nt = t // tm
    pos3 = pos.reshape(nt, 1, 2 * tm)
    return pl.pallas_call(
        _combine_kernel,
        grid=(nt,),
        in_specs=[pl.BlockSpec((None, 1, 2 * tm), lambda i: (i, 0, 0), memory_space=pltpu.SMEM),
                  pl.BlockSpec((None, 1, 2 * tm), lambda i: (jnp.minimum(i + 1, nt - 1), 0, 0),
                               memory_space=pltpu.SMEM),
                  pl.BlockSpec((None, 1, 2 * tm), lambda i: (jnp.minimum(i + 2, nt - 1), 0, 0),
                               memory_space=pltpu.SMEM),
                  pl.BlockSpec(memory_space=pl.ANY),
                  pl.BlockSpec((tm, LANES), lambda i: (i, 0)),
                  pl.BlockSpec((tm, d), lambda i: (i, 0)),
                  pl.BlockSpec((None, 1, d), lambda i: (i // tiles_per_seq, 0, 0)),
                  pl.BlockSpec((1, d), lambda i: (0, 0)),
                  pl.BlockSpec((1, d), lambda i: (0, 0))],
        out_specs=pl.BlockSpec((tm, d), lambda i: (i, 0)),
        out_shape=jax.ShapeDtypeStruct((t, d), F32),
        scratch_shapes=[pltpu.VMEM((GATHER_RING, 2, tm, d // 2), jnp.int32),
                        pltpu.SemaphoreType.DMA((GATHER_RING,))],
        compiler_params=_cparams(("arbitrary",)),
        name="combine_postln",
    )(pos3, pos3, pos3, y_sorted, rt, x1, g2, ln_g, ln_b)


def _layer(x2, c_pad, batch, seq, w_ada, b_ada, w_in, conv_w, conv_b, a_log_f, a_log_b, dt_bias_f,
           dt_bias_b, d_skip, ssd_norm_g, w_ssd_out, pool_w, pool_scale, w_pool_out, gate_b, w_o,
           ln1_g, ln1_b, router_wg, router_bg, router_we, router_be, exp_w_gate, exp_w_up,
           exp_w_down, ln2_g, ln2_b):
    t, d = x2.shape
    d_inner = SSD_HEADS * SSD_HEAD_DIM
    conv_dim = d_inner + 2 * SSD_GROUPS * SSD_STATE
    pool_width = pool_scale.shape[0]

    mod = _ada(c_pad, w_ada, b_ada[None, :])[:batch]
    sh1, sc1, g1, sh2, sc2, g2 = [m[:, None, :] for m in jnp.split(mod, 6, axis=-1)]

    o_xbc = d_inner
    o_dt = o_xbc + conv_dim
    o_pool = o_dt + 2 * SSD_HEADS
    w_in_b = w_in.astype(BF16)
    w_main = jnp.concatenate([w_in_b[:, :o_dt], w_in_b[:, o_pool:]], axis=1)
    w_dt = jnp.pad(w_in[:, o_dt:o_pool], ((0, 0), (0, LANES - 2 * SSD_HEADS)))
    pool_col = o_dt
    gate_col = o_dt + pool_width
    big, dt_raw = _inproj(x2, sh1, sc1, w_main, w_dt, seq)

    xc = _conv(big, o_xbc, conv_w, conv_b[None, :], seq)

    pad_row = lambda f, b: jnp.pad(jnp.concatenate([f, b]), (0, LANES - 2 * SSD_HEADS))[None, :]
    alog_row = pad_row(a_log_f, a_log_b)
    bias_row = pad_row(dt_bias_f, dt_bias_b)
    dskip_row = jnp.repeat(d_skip, SSD_HEAD_DIM)[None, :]
    y_bwd = _ssd(xc, dt_raw, alog_row, bias_row, batch, reverse=True)
    yn = _ssd(xc, dt_raw, alog_row, bias_row, batch, reverse=False,
              final_inputs=(big, y_bwd, dskip_row, ssd_norm_g[None, :]))

    w_router = jnp.pad(jnp.concatenate([router_wg, router_we], axis=1),
                       ((0, 0), (0, LANES - MOE_GROUPS - MOE_EXPERTS)))
    r_bias = jnp.pad(jnp.concatenate([router_bg, router_be]), (0, LANES - MOE_GROUPS - MOE_EXPERTS))[None, :]
    x1, h2, logits = _mix(yn, big, x2, w_ssd_out.astype(BF16), pool_w.astype(BF16),
                          w_pool_out.astype(BF16), w_o.astype(BF16), w_router, gate_b[None, :],
                          pool_scale[None, :], ln1_g[None, :], ln1_b[None, :], r_bias, g1, sh2, sc2,
                          seq, pool_col, gate_col)

    rt, counts = _route(logits)

    tme = TM_EXPERT
    cnt = counts[0, MOE_GROUPS:MOE_GROUPS + MOE_EXPERTS].astype(jnp.int32)
    padded = ((cnt + tme - 1) // tme) * tme
    ends = jnp.cumsum(padded)
    off = ends - padded
    eid = rt[:, 0:2].astype(jnp.int32)
    pos = rt[:, 2:4].astype(jnp.int32) + jnp.sum(
        jnp.where(eid[:, :, None] == jnp.arange(MOE_EXPERTS, dtype=jnp.int32), off, 0), axis=-1)
    p_rows = 2 * t + MOE_EXPERTS * tme
    n_tiles = p_rows // tme
    tile_ends = ends // tme
    tile_expert = jnp.minimum(
        jnp.sum(jnp.arange(n_tiles + 1, dtype=jnp.int32)[:, None] >= tile_ends[None, :], axis=1),
        MOE_EXPERTS - 1).astype(jnp.int32)
    n_used = tile_ends[-1:].astype(jnp.int32)

    inv = _slotmap(pos, jnp.zeros((p_rows,), jnp.int32))
    y_sorted = _experts(tile_expert, n_used, inv, h2, exp_w_gate, exp_w_up, exp_w_down)
    return _combine(pos, y_sorted, rt, x1, g2, ln2_g[None, :], ln2_b[None, :], seq)


def kernel(x, c, w_ada, b_ada, w_in, conv_w, conv_b, a_log_f, a_log_b, dt_bias_f, dt_bias_b, d_skip,
           ssd_norm_g, w_ssd_out, pool_w, pool_scale, w_pool_out, gate_b, w_o, ln1_g, ln1_b,
           router_wg, router_bg, router_we, router_be, exp_w_gate, exp_w_up, exp_w_down, ln2_g, ln2_b):
    batch, seq, d = x.shape
    x2 = x.reshape(batch * seq, d)
    c_pad = jnp.pad(c, ((0, 8 - batch), (0, 0)))
    params = (w_ada, b_ada, w_in, conv_w, conv_b, a_log_f, a_log_b, dt_bias_f, dt_bias_b, d_skip,
              ssd_norm_g, w_ssd_out, pool_w, pool_scale, w_pool_out, gate_b, w_o, ln1_g, ln1_b,
              router_wg, router_bg, router_we, router_be, exp_w_gate, exp_w_up, exp_w_down, ln2_g, ln2_b)
    for l in range(w_ada.shape[0]):
        x2 = _layer(x2, c_pad, batch, seq, *[p[l] for p in params])
    return x2.reshape(batch, seq, d)
```

```python
import functools

import jax
import jax.numpy as jnp
from jax import lax
from jax.experimental import pallas as pl
from jax.experimental.pallas import tpu as pltpu

F32 = jnp.float32
BF16 = jnp.bfloat16
HIGHEST = lax.Precision.HIGHEST

SSD_HEAD_DIM = 64
SSD_GROUPS = 8
SSD_HEADS_PER_GROUP = 4
SSD_HEADS = SSD_GROUPS * SSD_HEADS_PER_GROUP
SSD_STATE = 128
SSD_CONV = 5
SSD_CHUNK = 128
SSD_NORM_EPS = 1e-5
POOL_WINDOWS = (2, 4, 8, 16)
MOE_GROUPS = 4
MOE_EXPERTS_PER_GROUP = 8
MOE_EXPERTS = MOE_GROUPS * MOE_EXPERTS_PER_GROUP
DEPTH = 1
DEEPNORM_ALPHA = (2.0 * DEPTH) ** 0.25
LN_EPS = 1e-5

LANES = 128
HALO = 16
VMEM_LIMIT = 48 * 1024 * 1024
SLOTMAP_UNROLL = 8
DMA_PRIORITIES = 2
GATHER_RING = 3

TM_INPROJ = 1024
TN_INPROJ = 3072
TM_CONV = 1024
CONV_SUB = 128
SSD_STEP_CHUNKS = 4
TM_MIX = 512
POOL_SUB = 128
TM_ROUTE = 512
TM_SLOTMAP = 1024
TM_EXPERT = 256
TM_COMBINE = 512


def _dot(a, b):
    return jnp.dot(a, b, preferred_element_type=F32)


def _split_hi_lo(v):
    hi = v.astype(BF16)
    lo = (v - hi.astype(F32)).astype(BF16)
    return hi, lo


def _split3(v):
    hi = v.astype(BF16)
    r = v - hi.astype(F32)
    mid = r.astype(BF16)
    lo = (r - mid.astype(F32)).astype(BF16)
    return hi, mid, lo


def _dot3(a, b):
    a_hi, a_lo = _split_hi_lo(a)
    b_hi, b_lo = _split_hi_lo(b)
    return _dot(a_hi, b_hi) + _dot(a_lo, b_hi) + _dot(a_hi, b_lo)


def _layernorm(v):
    mu = jnp.mean(v, axis=-1, keepdims=True)
    vc = v - mu
    var = jnp.mean(vc * vc, axis=-1, keepdims=True)
    return vc * lax.rsqrt(var + LN_EPS)


def _silu(v):
    return v * jax.nn.sigmoid(v)


def _pack_bf16_pairs(v):
    m = v.shape[1] // 2
    hi = lax.bitcast_convert_type(v[:, :m].astype(BF16).astype(F32), jnp.int32)
    lo = lax.bitcast_convert_type(v[:, m:].astype(BF16).astype(F32), jnp.int32)
    return hi | lax.shift_right_logical(lo, 16)


def _unpack_bf16_pairs(p):
    hi = lax.bitcast_convert_type(p & jnp.int32(-65536), F32)
    lo = lax.bitcast_convert_type(lax.shift_left(p, 16), F32)
    return hi, lo


def _cparams(sem):
    return pltpu.CompilerParams(dimension_semantics=sem, vmem_limit_bytes=VMEM_LIMIT)


def _ada_kernel(c_ref, w_ref, b_ref, o_ref):
    o_ref[...] = jnp.dot(_silu(c_ref[...]), w_ref[...], precision=HIGHEST,
                         preferred_element_type=F32) + b_ref[...]


def _ada(c_pad, w, b):
    d, n = w.shape
    tn = 1024
    return pl.pallas_call(
        _ada_kernel,
        grid=(n // tn,),
        in_specs=[pl.BlockSpec((c_pad.shape[0], d), lambda j: (0, 0)),
                  pl.BlockSpec((d, tn), lambda j: (0, j)),
                  pl.BlockSpec((1, tn), lambda j: (0, j))],
        out_specs=pl.BlockSpec((c_pad.shape[0], tn), lambda j: (0, j)),
        out_shape=jax.ShapeDtypeStruct((c_pad.shape[0], n), F32),
        compiler_params=_cparams(("arbitrary",)),
        name="ada_mod",
    )(c_pad, w, b)


def _inproj_kernel(x_ref, sh_ref, sc_ref, w_ref, wdt_ref, o_ref, dt_ref, h_ref):
    @pl.when(pl.program_id(1) == 0)
    def _():
        h = _layernorm(x_ref[...]) * (1.0 + sc_ref[...]) + sh_ref[...]
        h_ref[...] = h.astype(BF16)
        dt_ref[...] = _dot3(h, wdt_ref[...])

    o_ref[...] = _dot(h_ref[...], w_ref[...]).astype(BF16)


def _inproj(x2, sh, sc, w_main, w_dt, seq):
    t, d = x2.shape
    n = w_main.shape[1]
    tm, tn = TM_INPROJ, TN_INPROJ
    tiles_per_seq = seq // tm
    return pl.pallas_call(
        _inproj_kernel,
        grid=(t // tm, n // tn),
        in_specs=[pl.BlockSpec((tm, d), lambda i, j: (i, 0)),
                  pl.BlockSpec((None, 1, d), lambda i, j: (i // tiles_per_seq, 0, 0)),
                  pl.BlockSpec((None, 1, d), lambda i, j: (i // tiles_per_seq, 0, 0)),
                  pl.BlockSpec((d, tn), lambda i, j: (0, j)),
                  pl.BlockSpec((d, LANES), lambda i, j: (0, 0))],
        out_specs=[pl.BlockSpec((tm, tn), lambda i, j: (i, j)),
                   pl.BlockSpec((tm, LANES), lambda i, j: (i, 0))],
        out_shape=[jax.ShapeDtypeStruct((t, n), BF16),
                   jax.ShapeDtypeStruct((t, LANES), F32)],
        scratch_shapes=[pltpu.VMEM((tm, d), BF16)],
        compiler_params=_cparams(("arbitrary", "arbitrary")),
        name="ln_inproj",
    )(x2, sh, sc, w_main, w_dt)


def _conv_kernel(tiles_per_seq, cur_ref, prev_ref, next_ref, w_ref, b_ref, o_ref, ext_ref):
    i = pl.program_id(0)
    tm = cur_ref.shape[0]
    first = (i % tiles_per_seq) == 0
    last = (i % tiles_per_seq) == tiles_per_seq - 1
    ext_ref[0:HALO, :] = jnp.where(first, jnp.zeros_like(prev_ref), prev_ref[...])
    ext_ref[HALO:HALO + tm, :] = cur_ref[...]
    ext_ref[HALO + tm:, :] = jnp.where(last, jnp.zeros_like(next_ref), next_ref[...])
    pad = SSD_CONV // 2
    sub = CONV_SUB
    win = sub + 2 * HALO
    ri = lax.broadcasted_iota(jnp.int32, (sub, win), 0)
    ci = lax.broadcasted_iota(jnp.int32, (sub, win), 1)
    picks = [jnp.where(ci == ri + HALO + k - pad, 1.0, 0.0).astype(BF16) for k in range(SSD_CONV)]
    for m in range(tm // sub):
        window = ext_ref[m * sub:m * sub + win, :]
        acc = b_ref[...] + w_ref[pad:pad + 1, :] * ext_ref[HALO + m * sub:HALO + (m + 1) * sub, :].astype(F32)
        for k in range(SSD_CONV):
            if k != pad:
                acc = acc + w_ref[k:k + 1, :] * _dot(picks[k], window)
        o_ref[m * sub:(m + 1) * sub, :] = _silu(acc).astype(BF16)


def _conv(big, col_off, conv_w, conv_b, seq):
    t = big.shape[0]
    cdim = conv_w.shape[1]
    tm, tc = TM_CONV, 1024
    tiles_per_seq = seq // tm
    cb0 = col_off // tc
    hb = tm // HALO
    nhb = t // HALO
    return pl.pallas_call(
        functools.partial(_conv_kernel, tiles_per_seq),
        grid=(t // tm, cdim // tc),
        in_specs=[pl.BlockSpec((tm, tc), lambda i, j: (i, cb0 + j)),
                  pl.BlockSpec((HALO, tc), lambda i, j: (jnp.maximum(i * hb - 1, 0), cb0 + j)),
                  pl.BlockSpec((HALO, tc), lambda i, j: (jnp.minimum((i + 1) * hb, nhb - 1), cb0 + j)),
                  pl.BlockSpec((SSD_CONV, tc), lambda i, j: (0, j)),
                  pl.BlockSpec((1, tc), lambda i, j: (0, j))],
        out_specs=pl.BlockSpec((tm, tc), lambda i, j: (i, j)),
        out_shape=jax.ShapeDtypeStruct((t, cdim), BF16),
        scratch_shapes=[pltpu.VMEM((tm + 2 * HALO, tc), BF16)],
        compiler_params=_cparams(("arbitrary", "arbitrary")),
        name="conv_silu",
    )(big, big, big, conv_w, conv_b)


def _tri_cumsum(tri, v):
    hi, mid, lo = _split3(v)
    return _dot(tri, hi) + _dot(tri, mid) + _dot(tri, lo)


def _expand_heads(v, e_ref):
    hi, lo = _split_hi_lo(v)
    return _dot(jnp.concatenate([hi, lo], axis=1), e_ref[...])


def _state_step(g, x_g, b_ref, c_ref, st_ref, w_exp, dec_exp, end_row):
    gw = x_g.shape[1]
    cols = slice(g * gw, (g + 1) * gw)
    bg = b_ref[:, g * SSD_STATE:(g + 1) * SSD_STATE]
    cg = c_ref[:, g * SSD_STATE:(g + 1) * SSD_STATE]
    h_in = st_ref[g]
    y_off = _dot(cg, h_in.astype(BF16)) * dec_exp[:, cols]
    xw = (x_g * w_exp[:, cols]).astype(BF16)
    s_new = lax.dot_general(bg, xw, (((0,), (0,)), ((), ())), preferred_element_type=F32)
    st_ref[g] = h_in * dec_exp[end_row:end_row + 1, cols] + s_new
    return y_off


def _chunk_views(refs, ch):
    return [r.at[pl.ds(ch * SSD_CHUNK, SSD_CHUNK)] for r in refs]


def _ssd_bwd_kernel(x_ref, b_ref, c_ref, dt_ref, alog_ref, bias_ref, e64_ref, o_ref, st_ref):
    @pl.when(pl.program_id(1) == 0)
    def _():
        st_ref[...] = jnp.zeros_like(st_ref)

    for ch in reversed(range(x_ref.shape[0] // SSD_CHUNK)):
        x_v, b_v, c_v, dt_v, o_v = _chunk_views((x_ref, b_ref, c_ref, dt_ref, o_ref), ch)
        _ssd_bwd_chunk(x_v, b_v, c_v, dt_v, alog_ref, bias_ref, e64_ref, o_v, st_ref)


def _ssd_bwd_chunk(x_ref, b_ref, c_ref, dt_ref, alog_ref, bias_ref, e64_ref, o_ref, st_ref):
    q = x_ref.shape[0]
    gw = SSD_HEADS_PER_GROUP * SSD_HEAD_DIM
    dt = jax.nn.softplus(dt_ref[...] + bias_ref[...])
    da = dt * (-jnp.exp(alog_ref[...]))
    li = lax.broadcasted_iota(jnp.int32, (q, q), 0)
    si = lax.broadcasted_iota(jnp.int32, (q, q), 1)
    cs = _tri_cumsum(jnp.where(si >= li, 1.0, 0.0).astype(BF16), da)
    w_exp = _expand_heads(dt * jnp.exp(cs[0:1, :] - cs), e64_ref)
    dec_exp = _expand_heads(jnp.exp(cs), e64_ref)
    for g in range(SSD_GROUPS):
        cols = slice(g * gw, (g + 1) * gw)
        x_g = x_ref[:, cols].astype(F32)
        o_ref[:, cols] = _state_step(g, x_g, b_ref, c_ref, st_ref, w_exp, dec_exp, 0).astype(BF16)


def _ssd_fwd_kernel(x_ref, b_ref, c_ref, dt_ref, alog_ref, bias_ref, e64_ref, ecs_ref, z_ref,
                    yb_ref, dskip_ref, ng_ref, o_ref, st_ref):
    @pl.when(pl.program_id(1) == 0)
    def _():
        st_ref[...] = jnp.zeros_like(st_ref)

    for ch in range(x_ref.shape[0] // SSD_CHUNK):
        x_v, b_v, c_v, dt_v, z_v, yb_v, o_v = _chunk_views(
            (x_ref, b_ref, c_ref, dt_ref, z_ref, yb_ref, o_ref), ch)
        _ssd_fwd_chunk(x_v, b_v, c_v, dt_v, alog_ref, bias_ref, e64_ref, ecs_ref, z_v, yb_v, dskip_ref,
                       ng_ref, o_v, st_ref)


def _ssd_fwd_chunk(x_ref, b_ref, c_ref, dt_ref, alog_ref, bias_ref, e64_ref, ecs_ref, z_ref,
                   yb_ref, dskip_ref, ng_ref, o_ref, st_ref):
    q = x_ref.shape[0]
    p = SSD_HEAD_DIM
    hpg = SSD_HEADS_PER_GROUP
    gw = hpg * p
    dt = jax.nn.softplus(dt_ref[...] + bias_ref[...])
    da = dt * (-jnp.exp(alog_ref[...]))
    li = lax.broadcasted_iota(jnp.int32, (q, q), 0)
    si = lax.broadcasted_iota(jnp.int32, (q, q), 1)
    before = si <= li
    after = si >= li
    lane = lax.broadcasted_iota(jnp.int32, (q, LANES), 1)
    tri2 = jnp.concatenate([jnp.where(before, 1.0, 0.0), jnp.where(after, 1.0, 0.0)], axis=1).astype(BF16)
    da2 = jnp.concatenate([jnp.where(lane < SSD_HEADS, da, 0.0),
                           jnp.where((lane >= SSD_HEADS) & (lane < 2 * SSD_HEADS), da, 0.0)], axis=0)
    cs = _tri_cumsum(tri2, da2)
    csp_t = (cs - jnp.log(dt)).T
    hi, mid, lo = _split3(cs)
    nhd = 2 * SSD_HEADS
    v_cs = jnp.concatenate([hi[:, :nhd].astype(F32), mid[:, :nhd].astype(F32), lo[:, :nhd].astype(F32),
                            jnp.zeros((q, nhd), F32)], axis=1).astype(BF16)
    to_end = jnp.where(lane < SSD_HEADS, cs[q - 1:q, :] - cs, 0.0)
    w_exp = _expand_heads(dt * jnp.exp(to_end), e64_ref)
    dec_exp = _expand_heads(jnp.exp(cs), e64_ref)
    half =lax.broadcasted_iota(jnp.int32, (q, 2 * p), 1) < p

    for g in range(SSD_GROUPS):
        cols = slice(g * gw, (g + 1) * gw)
        bg = b_ref[:, g * SSD_STATE:(g + 1) * SSD_STATE]
        cg = c_ref[:, g * SSD_STATE:(g + 1) * SSD_STATE]
        cb = lax.dot_general(cg, bg, (((1,), (1,)), ((), ())), preferred_element_type=F32)
        colb = _dot(v_cs, ecs_ref[:, g * 2 * hpg * q:(g + 1) * 2 * hpg * q])
        x_g = x_ref[:, cols].astype(F32)
        ms = []
        for r in range(hpg):
            hh = g * hpg + r
            seg_f = colb[:, r * q:(r + 1) * q] - csp_t[hh:hh + 1, :]
            seg_b = colb[:, (hpg + r) * q:(hpg + r + 1) * q] - csp_t[SSD_HEADS + hh:SSD_HEADS + hh + 1, :]
            decay = jnp.exp(jnp.where(before, seg_f, -jnp.inf)) + jnp.exp(jnp.where(after, seg_b, -jnp.inf))
            ms.append((cb * decay).astype(BF16))
        ys = []
        for pr in range(hpg // 2):
            xp = x_g[:, 2 * pr * p:2 * (pr + 1) * p]
            rhs = jnp.concatenate([jnp.where(half, xp, 0.0), jnp.where(half, 0.0, xp)], axis=0).astype(BF16)
            ys.append(_dot(jnp.concatenate([ms[2 * pr], ms[2 * pr + 1]], axis=1), rhs))
        y_g = jnp.concatenate(ys, axis=1)
        y_g = y_g + _state_step(g, x_g, b_ref, c_ref, st_ref, w_exp, dec_exp, q - 1)
        y_g = y_g + yb_ref[:, cols].astype(F32) + x_g * dskip_ref[:, cols]
        yg = y_g * _silu(z_ref[:, cols].astype(F32))
        ms_g = jnp.mean(yg * yg, axis=-1, keepdims=True)
        o_ref[:, cols] = (yg * lax.rsqrt(ms_g + SSD_NORM_EPS) * ng_ref[:, cols]).astype(BF16)


def _head_expansion(col0):
    j = jnp.arange(2 * LANES)[:, None]
    c = jnp.arange(SSD_HEADS * SSD_HEAD_DIM)[None, :]
    return ((j % LANES - col0) == c // SSD_HEAD_DIM).astype(BF16)


def _score_expansion():
    nhd = 2 * SSD_HEADS
    row = jnp.arange(4 * nhd)[:, None]
    col = jnp.arange(nhd * SSD_CHUNK)[None, :]
    j = row % nhd
    head = j % SSD_HEADS
    blk = ((head // SSD_HEADS_PER_GROUP) * 2 + j // SSD_HEADS) * SSD_HEADS_PER_GROUP + head % SSD_HEADS_PER_GROUP
    return ((row < 3 * nhd) & (blk == col // SSD_CHUNK)).astype(BF16)


def _ssd(xc, dt_raw, alog_row, bias_row, batch, reverse, final_inputs=None):
    t = xc.shape[0]
    q = SSD_STEP_CHUNKS * SSD_CHUNK
    nc = t // batch // q
    d_inner = SSD_HEADS * SSD_HEAD_DIM
    bcw = SSD_GROUPS * SSD_STATE

    def row(b, c):
        return b * nc + ((nc - 1 - c) if reverse else c)

    full = lambda a: pl.BlockSpec(a.shape, lambda b, c: (0,) * a.ndim)
    e64 = _head_expansion(SSD_HEADS if reverse else 0)
    in_specs = [pl.BlockSpec((q, d_inner), lambda b, c: (row(b, c), 0)),
                pl.BlockSpec((q, bcw), lambda b, c: (row(b, c), d_inner // bcw)),
                pl.BlockSpec((q, bcw), lambda b, c: (row(b, c), d_inner // bcw + 1)),
                pl.BlockSpec((q, LANES), lambda b, c: (row(b, c), 0)),
                full(alog_row), full(bias_row), full(e64)]
    args = [xc, xc, xc, dt_raw, alog_row, bias_row, e64]
    if reverse:
        body = _ssd_bwd_kernel
    else:
        body = _ssd_fwd_kernel
        big, y_bwd, dskip_row, ng_row = final_inputs
        ecs = _score_expansion()
        in_specs += [full(ecs),
                     pl.BlockSpec((q, d_inner), lambda b, c: (row(b, c), 0)),
                     pl.BlockSpec((q, d_inner), lambda b, c: (row(b, c), 0)),
                     full(dskip_row), full(ng_row)]
        args += [ecs, big, y_bwd, dskip_row, ng_row]
    return pl.pallas_call(
        body,
        grid=(batch, nc),
        in_specs=in_specs,
        out_specs=pl.BlockSpec((q, d_inner), lambda b, c: (row(b, c), 0)),
        out_shape=jax.ShapeDtypeStruct((t, d_inner), BF16),
        scratch_shapes=[pltpu.VMEM((SSD_GROUPS, SSD_STATE, SSD_HEADS_PER_GROUP * SSD_HEAD_DIM), F32)],
        compiler_params=_cparams(("arbitrary", "arbitrary")),
        name="ssd_bwd" if reverse else "ssd_fwd",
    )(*args)


def _mix_kernel(tiles_per_seq, seq, yn_ref, u_ref, up_ref, un_ref, ga_ref, gb_ref, x_ref,
                wssd_ref, pw_ref, wpo_ref, wo_ref, wr_ref,
                gateb_ref, pscale_ref, ln1g_ref, ln1b_ref, rb_ref, g1_ref, sh2_ref, sc2_ref,
                x1_ref, h2_ref, lg_ref, ext_ref):
    i = pl.program_id(0)
    tm, width = u_ref.shape
    first = (i % tiles_per_seq) == 0
    last = (i % tiles_per_seq) == tiles_per_seq - 1
    u = u_ref[...].astype(F32)
    ext_ref[0:HALO, :] = jnp.where(first, jnp.zeros_like(up_ref), up_ref[...])
    ext_ref[HALO:HALO + tm, :] = u_ref[...]
    ext_ref[HALO + tm:, :] = jnp.where(last, jnp.zeros_like(un_ref), un_ref[...])
    tpos = (i % tiles_per_seq) * tm + lax.broadcasted_iota(jnp.int32, (tm, 1), 0)
    gd = width // len(POOL_WINDOWS)
    sub = POOL_SUB
    win = sub + 2 * HALO
    ri = lax.broadcasted_iota(jnp.int32, (sub, win), 0) + HALO
    ci = lax.broadcasted_iota(jnp.int32, (sub, win), 1)
    mixed = []
    for gi, w in enumerate(POOL_WINDOWS):
        cols = slice(gi * gd, (gi + 1) * gd)
        band = jnp.where((ci >= ri - w // 2) & (ci < ri + w // 2), 1.0, 0.0).astype(BF16)
        s = jnp.concatenate([_dot(band, ext_ref[m * sub:m * sub + win, cols]) for m in range(tm // sub)], axis=0)
        cnt = (jnp.minimum(tpos + w // 2, seq) - jnp.maximum(tpos - w // 2, 0)).astype(F32)
        diff = s / cnt - u[:, cols]
        mixed.append(_dot(diff.astype(BF16), pw_ref[gi]))
    mixed = jnp.concatenate(mixed, axis=1) * pscale_ref[...]
    y_pool = _dot(mixed.astype(BF16), wpo_ref[...])
    y_ssd = _dot(yn_ref[...], wssd_ref[...])
    d = y_ssd.shape[1]
    g_ssd = jax.nn.sigmoid(ga_ref[...].astype(F32) + gateb_ref[:, 0:d])
    g_pool = jax.nn.sigmoid(gb_ref[...].astype(F32) + gateb_ref[:, d:2 * d])
    mix = _dot((g_ssd * y_ssd + g_pool * y_pool).astype(BF16), wo_ref[...])
    x1 = _layernorm(DEEPNORM_ALPHA * x_ref[...] + g1_ref[...] * mix) * ln1g_ref[...] + ln1b_ref[...]
    x1_ref[...] = x1
    h2 = _layernorm(x1) * (1.0 + sc2_ref[...]) + sh2_ref[...]
    h2_ref[...] = _pack_bf16_pairs(h2)
    lg_ref[...] = _dot3(h2, wr_ref[...]) + rb_ref[...]


def _mix(yn, big, x2, w_ssd_out, pool_w, w_pool_out, w_o, w_router, gate_b, pool_scale, ln1_g, ln1_b,
         r_bias, g1, sh2, sc2, seq, pool_col, gate_col):
    t, d = x2.shape
    tm = TM_MIX
    tiles_per_seq = seq // tm
    hb = tm // HALO
    nhb = t // HALO
    pcb = pool_col // d
    gcb = gate_col // d
    full = lambda a: pl.BlockSpec(a.shape, lambda i: (0,) * a.ndim)
    per_batch = pl.BlockSpec((None, 1, d), lambda i: (i // tiles_per_seq, 0, 0))
    return pl.pallas_call(
        functools.partial(_mix_kernel, tiles_per_seq, seq),
        grid=(t // tm,),
        in_specs=[pl.BlockSpec((tm, yn.shape[1]), lambda i: (i, 0)),
                  pl.BlockSpec((tm, d), lambda i: (i, pcb)),
                  pl.BlockSpec((HALO, d), lambda i: (jnp.maximum(i * hb - 1, 0), pcb)),
                  pl.BlockSpec((HALO, d), lambda i: (jnp.minimum((i + 1) * hb, nhb - 1), pcb)),
                  pl.BlockSpec((tm, d), lambda i: (i, gcb)),
                  pl.BlockSpec((tm, d), lambda i: (i, gcb + 1)),
                  pl.BlockSpec((tm, d), lambda i: (i, 0)),
                  full(w_ssd_out), full(pool_w), full(w_pool_out), full(w_o), full(w_router),
                  full(gate_b), full(pool_scale), full(ln1_g), full(ln1_b), full(r_bias),
                  per_batch, per_batch, per_batch],
        out_specs=[pl.BlockSpec((tm, d), lambda i: (i, 0)),
                   pl.BlockSpec((tm, d // 2), lambda i: (i, 0)),
                   pl.BlockSpec((tm, LANES), lambda i: (i, 0))],
        out_shape=[jax.ShapeDtypeStruct((t, d), F32),
                   jax.ShapeDtypeStruct((t, d // 2), jnp.int32),
                   jax.ShapeDtypeStruct((t, LANES), F32)],
        scratch_shapes=[pltpu.VMEM((tm + 2 * HALO, d), BF16)],
        compiler_params=_cparams(("arbitrary",)),
        name="mix_postln",
    )(yn, big, big, big, big, big, x2, w_ssd_out, pool_w, w_pool_out, w_o, w_router,
      gate_b, pool_scale, ln1_g, ln1_b, r_bias, g1, sh2, sc2)


def _route_kernel(lg_ref, rt_ref, cnt_ref, carry_ref):
    @pl.when(pl.program_id(0) == 0)
    def _():
        carry_ref[...] = jnp.zeros_like(carry_ref)

    lg = lg_ref[...]
    tm = lg.shape[0]
    lane = lax.broadcasted_iota(jnp.int32, lg.shape, 1).astype(F32)
    neg = -jnp.inf
    big_lane = float(LANES)
    gl = jnp.where(lane < MOE_GROUPS, lg, neg)
    gmax = jnp.max(gl, axis=-1, keepdims=True)
    g_w = 1.0 / jnp.sum(jnp.exp(gl - gmax), axis=-1, keepdims=True)
    g_idx = jnp.min(jnp.where(gl == gmax, lane, big_lane), axis=-1, keepdims=True)
    lo = MOE_GROUPS + MOE_EXPERTS_PER_GROUP * g_idx
    el = jnp.where((lane >= lo) & (lane < lo + MOE_EXPERTS_PER_GROUP), lg, neg)
    m1 = jnp.max(el, axis=-1, keepdims=True)
    i1 = jnp.min(jnp.where(el == m1, lane, big_lane), axis=-1, keepdims=True)
    el2 = jnp.where(lane == i1, neg, el)
    m2 = jnp.max(el2, axis=-1, keepdims=True)
    i2 = jnp.min(jnp.where(el2 == m2, lane, big_lane), axis=-1, keepdims=True)
    e = jnp.exp(m2 - m1)
    w1 = g_w / (1.0 + e)
    w2 = g_w * e / (1.0 + e)
    onehot = jnp.where((lane == i1) | (lane == i2), 1.0, 0.0)
    ri = lax.broadcasted_iota(jnp.int32, (tm, tm), 0)
    ci = lax.broadcasted_iota(jnp.int32, (tm, tm), 1)
    earlier = jnp.where(ci < ri, 1.0, 0.0).astype(BF16)
    rank = _dot(earlier, onehot.astype(BF16)) + carry_ref[...]
    r1 = jnp.sum(jnp.where(lane == i1, rank, 0.0), axis=-1, keepdims=True)
    r2 = jnp.sum(jnp.where(lane == i2, rank, 0.0), axis=-1, keepdims=True)
    carry_ref[...] = carry_ref[...] + jnp.sum(onehot, axis=0, keepdims=True)
    cnt_ref[...] = carry_ref[...]
    out = jnp.where(lane == 0, i1 - MOE_GROUPS, 0.0)
    out = jnp.where(lane == 1, i2 - MOE_GROUPS, out)
    out = jnp.where(lane == 2, r1, out)
    out = jnp.where(lane == 3, r2, out)
    out = jnp.where(lane == 4, w1, out)
    out = jnp.where(lane == 5, w2, out)
    rt_ref[...] = out


def _route(logits):
    t = logits.shape[0]
    tm = TM_ROUTE
    return pl.pallas_call(
        _route_kernel,
        grid=(t // tm,),
        in_specs=[pl.BlockSpec((tm, LANES), lambda i: (i, 0))],
        out_specs=[pl.BlockSpec((tm, LANES), lambda i: (i, 0)),
                   pl.BlockSpec((1, LANES), lambda i: (0, 0))],
        out_shape=[jax.ShapeDtypeStruct((t, LANES), F32),
                   jax.ShapeDtypeStruct((1, LANES), F32)],
        scratch_shapes=[pltpu.VMEM((1, LANES), F32)],
        compiler_params=_cparams(("arbitrary",)),
        name="route",
    )(logits)


def _slotmap_kernel(pos_ref, init_ref, inv_ref, sem):
    i = pl.program_id(0)
    tm = pos_ref.shape[1] // 2

    @pl.when(i == 0)
    def _():
        cp = pltpu.make_async_copy(init_ref, inv_ref, sem)
        cp.start()
        cp.wait()

    def body(r, carry):
        tok = i * tm + r
        inv_ref[pos_ref[0, 2 * r]] = tok
        inv_ref[pos_ref[0, 2 * r + 1]] = tok
        return carry

    lax.fori_loop(0, tm, body, 0, unroll=SLOTMAP_UNROLL)


def _slotmap(pos, init):
    t = pos.shape[0]
    tm = TM_SLOTMAP
    pos3 = pos.reshape(t // tm, 1, 2 * tm)
    return pl.pallas_call(
        _slotmap_kernel,
        grid=(t // tm,),
        in_specs=[pl.BlockSpec((None, 1, 2 * tm), lambda i: (i, 0, 0), memory_space=pltpu.SMEM),
                  pl.BlockSpec(memory_space=pl.ANY)],
        out_specs=pl.BlockSpec(memory_space=pltpu.SMEM),
        out_shape=jax.ShapeDtypeStruct(init.shape, jnp.int32),
        scratch_shapes=[pltpu.SemaphoreType.DMA(())],
        compiler_params=_cparams(("arbitrary",)),
        name="slotmap",
    )(pos3, init)


def _row_copy(src_ref, src_row, dst_ref, dst_row, sem):
    return pltpu.make_async_copy(src_ref.at[pl.ds(src_row, 1)], dst_ref.at[pl.ds(dst_row, 1)], sem)


def _expert_kernel(te_ref, nu_ref, inv0_ref, inv1_ref, inv2_ref, h2_ref, wg_ref, wu_ref, wd_ref,
                   y_ref, xbuf_ref, wgb_ref, wub_ref, wdb_ref, gsem):
    j = pl.program_id(0)
    n_used = nu_ref[0]
    tm = y_ref.shape[0]
    ring = xbuf_ref.shape[0]
    slot = j % ring

    def gather_start(inv_ref, s):
        for r in range(tm):
            cp = _row_copy(h2_ref, inv_ref[0, r], xbuf_ref.at[s], r, gsem.at[s])
            cp.start(priority=r % DMA_PRIORITIES)

    def gather_wait(s):
        pltpu.make_async_copy(h2_ref.at[pl.ds(0, tm)], xbuf_ref.at[s], gsem.at[s]).wait()

    @pl.when(j == 0)
    def _():
        gather_start(inv0_ref, 0)
        gather_start(inv1_ref, 1)

    new_expert = jnp.logical_or(j == 0, te_ref[j] != te_ref[jnp.maximum(j - 1, 0)])

    @pl.when(jnp.logical_and(j < n_used, new_expert))
    def _():
        wgb_ref[...] = wg_ref[...].astype(BF16)
        wub_ref[...] = wu_ref[...].astype(BF16)
        wdb_ref[...] = wd_ref[...].astype(BF16)

    @pl.when(j < n_used)
    def _():
        gather_wait(slot)
        xl, xr = [v.astype(BF16) for v in _unpack_bf16_pairs(xbuf_ref[slot])]
        gather_start(inv2_ref, (j + 2) % ring)
        half = xl.shape[1]
        gate = _dot(xl, wgb_ref[0:half, :]) + _dot(xr, wgb_ref[half:, :])
        up = _dot(xl, wub_ref[0:half, :]) + _dot(xr, wub_ref[half:, :])
        y = _dot((_silu(gate) * up).astype(BF16), wdb_ref[...])
        y_ref[...] = _pack_bf16_pairs(y)

    @pl.when(j >= n_used)
    def _():
        y_ref[...] = jnp.zeros_like(y_ref)

    @pl.when(j == n_used)
    def _():
        gather_wait(slot)
        gather_wait((j + 1) % ring)


def _experts(tile_expert, n_used, inv, h2, w_gate, w_up, w_down):
    _, d, hdim = w_gate.shape
    assert h2.shape[1] == d // 2 and h2.dtype == jnp.int32, "rows arrive as packed bf16 column pairs"
    tm = TM_EXPERT
    n_tiles = inv.shape[0] // tm
    inv3 = inv.reshape(n_tiles, 1, tm)
    last_used = lambda j, nu: jnp.minimum(j, nu[0] - 1)
    grid_spec = pltpu.PrefetchScalarGridSpec(
        num_scalar_prefetch=2,
        grid=(n_tiles + 1,),
        in_specs=[pl.BlockSpec((None, 1, tm), lambda j, te, nu: (jnp.minimum(j, n_tiles - 1), 0, 0),
                               memory_space=pltpu.SMEM),
                  pl.BlockSpec((None, 1, tm), lambda j, te, nu: (jnp.minimum(j + 1, n_tiles - 1), 0, 0),
                               memory_space=pltpu.SMEM),
                  pl.BlockSpec((None, 1, tm), lambda j, te, nu: (jnp.minimum(j + 2, n_tiles - 1), 0, 0),
                               memory_space=pltpu.SMEM),
                  pl.BlockSpec(memory_space=pl.ANY),
                  pl.BlockSpec((None, d, hdim), lambda j, te, nu: (te[last_used(j, nu)], 0, 0)),
                  pl.BlockSpec((None, d, hdim), lambda j, te, nu: (te[last_used(j, nu)], 0, 0)),
                  pl.BlockSpec((None, hdim, d), lambda j, te, nu: (te[last_used(j, nu)], 0, 0))],
        out_specs=pl.BlockSpec((tm, d // 2), lambda j, te, nu: (j, 0)),
        scratch_shapes=[pltpu.VMEM((GATHER_RING, tm, d // 2), jnp.int32),
                        pltpu.VMEM((d, hdim), BF16), pltpu.VMEM((d, hdim), BF16),
                        pltpu.VMEM((hdim, d), BF16), pltpu.SemaphoreType.DMA((GATHER_RING,))],
    )
    return pl.pallas_call(
        _expert_kernel,
        grid_spec=grid_spec,
        out_shape=jax.ShapeDtypeStruct(((n_tiles + 1) * tm, d // 2), jnp.int32),
        compiler_params=_cparams(("arbitrary",)),
        name="experts",
    )(tile_expert, n_used, inv3, inv3, inv3, h2, w_gate, w_up, w_down)


def _combine_kernel(pos0_ref, pos1_ref, pos2_ref, y_ref, rt_ref, x1_ref, g2_ref, lng_ref, lnb_ref, o_ref,
                    buf_ref, sem):
    i = pl.program_id(0)
    tm = x1_ref.shape[0]
    ring = buf_ref.shape[0]
    slot = i % ring

    def gather_start(pos_ref, s):
        for r in range(tm):
            for k in range(2):
                cp = _row_copy(y_ref, pos_ref[0, 2 * r + k], buf_ref.at[s, k], r, sem.at[s])
                cp.start(priority=k % DMA_PRIORITIES)

    def gather_wait(s):
        for k in range(2):
            pltpu.make_async_copy(y_ref.at[pl.ds(0, tm)], buf_ref.at[s, k], sem.at[s]).wait()

    @pl.when(i == 0)
    def _():
        gather_start(pos0_ref, 0)
        gather_start(pos1_ref, 1)

    gather_wait(slot)
    l0, r0 = _unpack_bf16_pairs(buf_ref[slot, 0])
    l1, r1 = _unpack_bf16_pairs(buf_ref[slot, 1])
    gather_start(pos2_ref, (i + 2) % ring)
    rt = rt_ref[...]
    w0, w1 = rt[:, 4:5], rt[:, 5:6]
    y_moe = jnp.concatenate([w0 * l0 + w1 * l1, w0 * r0 + w1 * r1], axis=1)
    v = DEEPNORM_ALPHA * x1_ref[...] + g2_ref[...] * y_moe
    o_ref[...] = _layernorm(v) * lng_ref[...] + lnb_ref[...]

    @pl.when(i == pl.num_programs(0) - 1)
    def _():
        gather_wait((i + 1) % ring)
        gather_wait((i + 2) % ring)


def _combine(pos, y_sorted, rt, x1, g2, ln_g, ln_b, seq):
    t, d = x1.shape
    tm = TM_COMBINE
    tiles_per_seq = seq // tm
    nt = t // tm
    pos3 = pos.reshape(nt, 1, 2 * tm)
    return pl.pallas_call(
        _combine_kernel,
        grid=(nt,),
        in_specs=[pl.BlockSpec((None, 1, 2 * tm), lambda i: (i, 0, 0), memory_space=pltpu.SMEM),
                  pl.BlockSpec((None, 1, 2 * tm), lambda i: (jnp.minimum(i + 1, nt - 1), 0, 0),
                               memory_space=pltpu.SMEM),
                  pl.BlockSpec((None, 1, 2 * tm), lambda i: (jnp.minimum(i + 2, nt - 1), 0, 0),
                               memory_space=pltpu.SMEM),
                  pl.BlockSpec(memory_space=pl.ANY),
                  pl.BlockSpec((tm, LANES), lambda i: (i, 0)),
                  pl.BlockSpec((tm, d), lambda i: (i, 0)),
                  pl.BlockSpec((None, 1, d), lambda i: (i // tiles_per_seq, 0, 0)),
                  pl.BlockSpec((1, d), lambda i: (0, 0)),
                  pl.BlockSpec((1, d), lambda i: (0, 0))],
        out_specs=pl.BlockSpec((tm, d), lambda i: (i, 0)),
        out_shape=jax.ShapeDtypeStruct((t, d), F32),
        scratch_shapes=[pltpu.VMEM((GATHER_RING, 2, tm, d // 2), jnp.int32),
                        pltpu.SemaphoreType.DMA((GATHER_RING,))],
        compiler_params=_cparams(("arbitrary",)),
        name="combine_postln",
    )(pos3, pos3, pos3, y_sorted, rt, x1, g2, ln_g, ln_b)


def _layer(x2, c_pad, batch, seq, w_ada, b_ada, w_in, conv_w, conv_b, a_log_f, a_log_b, dt_bias_f,
           dt_bias_b, d_skip, ssd_norm_g, w_ssd_out, pool_w, pool_scale, w_pool_out, gate_b, w_o,
           ln1_g, ln1_b, router_wg, router_bg, router_we, router_be, exp_w_gate, exp_w_up,
           exp_w_down, ln2_g, ln2_b):
    t, d = x2.shape
    d_inner = SSD_HEADS * SSD_HEAD_DIM
    conv_dim = d_inner + 2 * SSD_GROUPS * SSD_STATE
    pool_width = pool_scale.shape[0]

    mod = _ada(c_pad, w_ada, b_ada[None, :])[:batch]
    sh1, sc1, g1, sh2, sc2, g2 = [m[:, None, :] for m in jnp.split(mod, 6, axis=-1)]

    o_xbc = d_inner
    o_dt = o_xbc + conv_dim
    o_pool = o_dt + 2 * SSD_HEADS
    w_in_b = w_in.astype(BF16)
    w_main = jnp.concatenate([w_in_b[:, :o_dt], w_in_b[:, o_pool:]], axis=1)
    w_dt = jnp.pad(w_in[:, o_dt:o_pool], ((0, 0), (0, LANES - 2 * SSD_HEADS)))
    pool_col = o_dt
    gate_col = o_dt + pool_width
    big, dt_raw = _inproj(x2, sh1, sc1, w_main, w_dt, seq)

    xc = _conv(big, o_xbc, conv_w, conv_b[None, :], seq)

    pad_row = lambda f, b: jnp.pad(jnp.concatenate([f, b]), (0, LANES - 2 * SSD_HEADS))[None, :]
    alog_row = pad_row(a_log_f, a_log_b)
    bias_row = pad_row(dt_bias_f, dt_bias_b)
    dskip_row = jnp.repeat(d_skip, SSD_HEAD_DIM)[None, :]
    y_bwd = _ssd(xc, dt_raw, alog_row, bias_row, batch, reverse=True)
    yn = _ssd(xc, dt_raw, alog_row, bias_row, batch, reverse=False,
              final_inputs=(big, y_bwd, dskip_row, ssd_norm_g[None, :]))

    w_router = jnp.pad(jnp.concatenate([router_wg, router_we], axis=1),
                       ((0, 0), (0, LANES - MOE_GROUPS - MOE_EXPERTS)))
    r_bias = jnp.pad(jnp.concatenate([router_bg, router_be]), (0, LANES - MOE_GROUPS - MOE_EXPERTS))[None, :]
    x1, h2, logits = _mix(yn, big, x2, w_ssd_out.astype(BF16), pool_w.astype(BF16),
                          w_pool_out.astype(BF16), w_o.astype(BF16), w_router, gate_b[None, :],
                          pool_scale[None, :], ln1_g[None, :], ln1_b[None, :], r_bias, g1, sh2, sc2,
                          seq, pool_col, gate_col)

    rt, counts = _route(logits)

    tme = TM_EXPERT
    cnt = counts[0, MOE_GROUPS:MOE_GROUPS + MOE_EXPERTS].astype(jnp.int32)
    padded = ((cnt + tme - 1) // tme) * tme
    ends = jnp.cumsum(padded)
    off = ends - padded
    eid = rt[:, 0:2].astype(jnp.int32)
    pos = rt[:, 2:4].astype(jnp.int32) + jnp.sum(
        jnp.where(eid[:, :, None] == jnp.arange(MOE_EXPERTS, dtype=jnp.int32), off, 0), axis=-1)
    p_rows = 2 * t + MOE_EXPERTS * tme
    n_tiles = p_rows // tme
    tile_ends = ends // tme
    tile_expert = jnp.minimum(
        jnp.sum(jnp.arange(n_tiles + 1, dtype=jnp.int32)[:, None] >= tile_ends[None, :], axis=1),
        MOE_EXPERTS - 1).astype(jnp.int32)
    n_used = tile_ends[-1:].astype(jnp.int32)

    inv = _slotmap(pos, jnp.zeros((p_rows,), jnp.int32))
    y_sorted = _experts(tile_expert, n_used, inv, h2, exp_w_gate, exp_w_up, exp_w_down)
    return _combine(pos, y_sorted, rt, x1, g2, ln2_g[None, :], ln2_b[None, :], seq)


def kernel(x, c, w_ada, b_ada, w_in, conv_w, conv_b, a_log_f, a_log_b, dt_bias_f, dt_bias_b, d_skip,
           ssd_norm_g, w_ssd_out, pool_w, pool_scale, w_pool_out, gate_b, w_o, ln1_g, ln1_b,
           router_wg, router_bg, router_we, router_be, exp_w_gate, exp_w_up, exp_w_down, ln2_g, ln2_b):
    batch, seq, d = x.shape
    x2 = x.reshape(batch * seq, d)
    c_pad = jnp.pad(c, ((0, 8 - batch), (0, 0)))
    params = (w_ada, b_ada, w_in, conv_w, conv_b, a_log_f, a_log_b, dt_bias_f, dt_bias_b, d_skip,
              ssd_norm_g, w_ssd_out, pool_w, pool_scale, w_pool_out, gate_b, w_o, ln1_g, ln1_b,
              router_wg, router_bg, router_we, router_be, exp_w_gate, exp_w_up, exp_w_down, ln2_g, ln2_b)
    for l in range(w_ada.shape[0]):
        x2 = _layer(x2, c_pad, batch, seq, *[p[l] for p in params])
    return x2.reshape(batch, seq, d)
```
